```python
import jax, jax.numpy as jnp
from jax import lax
import numpy as np

D_MODEL = 1024
BATCH = 32
SEQ = 2048
DEPTH = 1

POOL_WIDTH = D_MODEL // 2
POOL_WINDOWS = (2, 4, 8, 16)
POOL_GROUP = POOL_WIDTH // len(POOL_WINDOWS)
N_HEADS = 8
HEAD_DIM = 64
N_KV_GROUPS = 2
HEADS_PER_GROUP = N_HEADS // N_KV_GROUPS
ATTN_WIDTH = N_HEADS * HEAD_DIM
KV_WIDTH = N_KV_GROUPS * HEAD_DIM
CMP_BLOCK = 32
CMP_STRIDE = 16
CMP_HIDDEN = 256
SEL_BLOCK = 64
N_SELECT = 8
WINDOW = 512
Q_CHUNK = 64
N_BRANCH = 3
IN_WIDTH = POOL_WIDTH + ATTN_WIDTH + 6 * KV_WIDTH + N_BRANCH * N_HEADS + 2 * D_MODEL
N_GROUPS = 4
EXPERTS_PER_GROUP = 8
N_EXPERTS = N_GROUPS * EXPERTS_PER_GROUP
TOP_K_IN_GROUP = 2
EXPERT_FF = 512

EPS = 1e-6
NEG = -1e30
FORCE_SCORE = 1e4

kernel_name = 'hybrid_pool_nsa_hmoe_block'


def rmsnorm(x, g):
    xf = x.astype(jnp.float32)
    r = lax.rsqrt(jnp.mean(xf * xf, axis=-1, keepdims=True) + EPS)
    return (xf * r * g.astype(jnp.float32)).astype(x.dtype)


def masked_softmax(s, mask):
    p = jax.nn.softmax(jnp.where(mask, s, NEG), axis=-1)
    return p * mask


def pool_mixer(u, pool_w, pool_scale):
    B, S, _ = u.shape
    t = jnp.arange(S)
    outs = []
    for gi, w in enumerate(POOL_WINDOWS):
        ug = u[..., gi * POOL_GROUP:(gi + 1) * POOL_GROUP].astype(jnp.float32)
        cs = jnp.cumsum(ug, axis=1)
        prev = jnp.pad(cs, ((0, 0), (w, 0), (0, 0)))[:, :S]
        cnt = jnp.minimum(t + 1, w).astype(jnp.float32)[None, :, None]
        outs.append((cs - prev) / cnt - ug)
    p = jnp.stack(outs, axis=2).astype(u.dtype)
    y = jnp.einsum('bsgc,gcd->bsgd', p, pool_w).reshape(B, S, POOL_WIDTH)
    return y * pool_scale


def compress_blocks(kv, pos, w1, b1, w2):
    B, S, G, dh = kv.shape
    r = CMP_BLOCK // CMP_STRIDE
    ch = kv.reshape(B, S // CMP_STRIDE, CMP_STRIDE, G, dh)
    n = S // CMP_STRIDE - r + 1
    blocks = jnp.concatenate([ch[:, i:i + n] for i in range(r)], axis=2)
    blocks = blocks + pos[None, None, :, None, :]
    flat = blocks.transpose(0, 1, 3, 2, 4).reshape(B, n, G, CMP_BLOCK * dh)
    return jax.nn.gelu(flat @ w1 + b1) @ w2


def selection_overlap(n_cmp, n_blk):
    s1 = np.arange(n_cmp)[:, None] * CMP_STRIDE
    s2 = np.arange(n_blk)[None, :] * SEL_BLOCK
    ov = np.clip(np.minimum(s1 + CMP_BLOCK, s2 + SEL_BLOCK) - np.maximum(s1, s2), 0, None)
    return (ov / CMP_BLOCK).astype(np.float32)


def nsa_attention(q, k_cmp, v_cmp, k_sel, v_sel, k_win, v_win, branch_gate,
                  cmp_pos, cmp_w1, cmp_b1, cmp_w2):
    B, S, _ = q.shape
    G, Hg, dh = N_KV_GROUPS, HEADS_PER_GROUP, HEAD_DIM
    scale = HEAD_DIM ** -0.5
    q = q.reshape(B, S, G, Hg, dh)
    kc = compress_blocks(k_cmp.reshape(B, S, G, dh), cmp_pos[0], cmp_w1[0], cmp_b1[0], cmp_w2[0])
    vc = compress_blocks(v_cmp.reshape(B, S, G, dh), cmp_pos[1], cmp_w1[1], cmp_b1[1], cmp_w2[1])
    n_cmp = kc.shape[1]
    n_blk = S // SEL_BLOCK
    n_sel = min(N_SELECT, n_blk)
    overlap = jnp.asarray(selection_overlap(n_cmp, n_blk))
    ks_b = k_sel.reshape(B, n_blk, SEL_BLOCK, G, dh).transpose(0, 3, 1, 2, 4)
    vs_b = v_sel.reshape(B, n_blk, SEL_BLOCK, G, dh).transpose(0, 3, 1, 2, 4)
    pad = ((0, 0), (WINDOW, 0), (0, 0), (0, 0))
    kw_p = jnp.pad(k_win.reshape(B, S, G, dh), pad)
    vw_p = jnp.pad(v_win.reshape(B, S, G, dh), pad)
    gates = jax.nn.sigmoid(branch_gate.astype(jnp.float32)).astype(q.dtype)
    gates = gates.reshape(B, S, N_BRANCH, G, Hg)
    cmp_end = jnp.arange(n_cmp) * CMP_STRIDE + CMP_BLOCK - 1
    blk = jnp.arange(n_blk)
    gather = jax.vmap(jax.vmap(lambda a, i: a[i]))

    def chunk(ci):
        q0 = ci * Q_CHUNK
        qc = lax.dynamic_slice_in_dim(q, q0, Q_CHUNK, axis=1)
        gc = lax.dynamic_slice_in_dim(gates, q0, Q_CHUNK, axis=1)
        t = q0 + jnp.arange(Q_CHUNK)
        s1 = jnp.einsum('bqghd,bngd->bghqn', qc, kc).astype(jnp.float32) * scale
        m1 = cmp_end[None, :] <= t[:, None]
        p1 = masked_softmax(s1, m1)
        o_cmp = jnp.einsum('bghqn,bngd->bqghd', p1.astype(vc.dtype), vc)
        ps = jnp.einsum('bghqn,nj->bgqj', p1, overlap)
        cur = t // SEL_BLOCK
        valid = blk[None, :] <= cur[:, None]
        forced = (blk[None, :] == 0) | (blk[None, :] == cur[:, None]) | (blk[None, :] == cur[:, None] - 1)
        score = jnp.where(forced, FORCE_SCORE, jnp.where(valid, ps, NEG))
        _, idx = lax.top_k(score, n_sel)
        flat_idx = idx.reshape(B, G, Q_CHUNK * n_sel)
        kg = gather(ks_b, flat_idx).reshape(B, G, Q_CHUNK, n_sel * SEL_BLOCK, dh)
        vg = gather(vs_b, flat_idx).reshape(B, G, Q_CHUNK, n_sel * SEL_BLOCK, dh)
        key_pos = (idx[..., None] * SEL_BLOCK + jnp.arange(SEL_BLOCK)).reshape(B, G, Q_CHUNK, n_sel * SEL_BLOCK)
        m2 = (key_pos <= t[None, None, :, None])[:, :, None]
        s2 = jnp.einsum('bqghd,bgqkd->bghqk', qc, kg).astype(jnp.float32) * scale
        p2 = masked_softmax(s2, m2)
        o_sel = jnp.einsum('bghqk,bgqkd->bqghd', p2.astype(vg.dtype), vg)
        kwc = lax.dynamic_slice_in_dim(kw_p, q0, WINDOW + Q_CHUNK, axis=1)
        vwc = lax.dynamic_slice_in_dim(vw_p, q0, WINDOW + Q_CHUNK, axis=1)
        kpos = q0 - WINDOW + jnp.arange(WINDOW + Q_CHUNK)
        m3 = (kpos[None, :] <= t[:, None]) & (kpos[None, :] > t[:, None] - WINDOW) & (kpos[None, :] >= 0)
        s3 = jnp.einsum('bqghd,bkgd->bghqk', qc, kwc).astype(jnp.float32) * scale
        p3 = masked_softmax(s3, m3)
        o_win = jnp.einsum('bghqk,bkgd->bqghd', p3.astype(vwc.dtype), vwc)
        o = (gc[:, :, 0, :, :, None] * o_cmp + gc[:, :, 1, :, :, None] * o_sel
             + gc[:, :, 2, :, :, None] * o_win)
        return o.reshape(B, Q_CHUNK, ATTN_WIDTH)

    out = lax.map(chunk, jnp.arange(S // Q_CHUNK))
    return out.transpose(1, 0, 2, 3).reshape(B, S, ATTN_WIDTH)


def token_mixer(h, w_in, pool_w, pool_scale, cmp_pos, cmp_w1, cmp_b1, cmp_w2,
                w_up_pool, w_up_attn, w_out):
    z = h @ w_in
    cuts = np.cumsum([POOL_WIDTH, ATTN_WIDTH] + [KV_WIDTH] * 6 + [N_BRANCH * N_HEADS])
    u_pool, q, kc, vc, ks, vs, kw, vw, bgate, mgate = jnp.split(z, [int(i) for i in cuts], axis=-1)
    y_pool = pool_mixer(u_pool, pool_w, pool_scale) @ w_up_pool
    y_attn = nsa_attention(q, kc, vc, ks, vs, kw, vw, bgate, cmp_pos, cmp_w1, cmp_b1, cmp_w2) @ w_up_attn
    g = jax.nn.sigmoid(mgate.astype(jnp.float32)).astype(h.dtype)
    return (g[..., :D_MODEL] * y_pool + g[..., D_MODEL:] * y_attn) @ w_out


def hier_moe(h, rg_w, rg_b, re_w, re_b, w_gate, w_up, w_down):
    B, S, D = h.shape
    xt = h.reshape(-1, D)
    n = xt.shape[0]
    lg = (xt @ rg_w + rg_b).astype(jnp.float32)
    gidx = jnp.argmax(lg, axis=-1)
    gp = jnp.max(jax.nn.softmax(lg, axis=-1), axis=-1)
    le = (xt @ re_w + re_b).astype(jnp.float32).reshape(n, N_GROUPS, EXPERTS_PER_GROUP)
    le_sel = le[jnp.arange(n), gidx]
    tv, ti = lax.top_k(le_sel, TOP_K_IN_GROUP)
    w = gp[:, None] * jax.nn.softmax(tv, axis=-1)
    eid = gidx[:, None] * EXPERTS_PER_GROUP + ti
    combine = jnp.einsum('nk,nke->ne', w, jax.nn.one_hot(eid, N_EXPERTS, dtype=jnp.float32)).astype(xt.dtype)
    y = jnp.zeros_like(xt)
    for e in range(N_EXPERTS):
        he = jax.nn.silu(xt @ w_gate[e]) * (xt @ w_up[e])
        y = y + combine[:, e:e + 1] * (he @ w_down[e])
    return y.reshape(B, S, D)


def setup_inputs(seed: int = 0) -> dict:
    key = jax.random.key(seed)
    ks = jax.random.split(key, 26)
    L, D, f32 = DEPTH, D_MODEL, jnp.float32
    nrm = lambda k, shape: jax.random.normal(k, shape, f32)
    lin = lambda k, shape, fan: nrm(k, shape) * fan ** -0.5
    return {
        'x': nrm(ks[0], (BATCH, SEQ, D)),
        'c': nrm(ks[1], (BATCH, D)),
        'ada_w': 0.5 * lin(ks[2], (L, D, 6 * D), D),
        'ada_b': 0.02 * nrm(ks[3], (L, 6 * D)),
        'norm1_g': 1.0 + 0.02 * nrm(ks[4], (L, D)),
        'w_in': lin(ks[5], (L, D, IN_WIDTH), D),
        'pool_w': lin(ks[6], (L, len(POOL_WINDOWS), POOL_GROUP, POOL_GROUP), POOL_GROUP),
        'pool_scale': 1.0 + 0.1 * nrm(ks[7], (L, POOL_WIDTH)),
        'cmp_pos': 0.02 * nrm(ks[8], (L, 2, CMP_BLOCK, HEAD_DIM)),
        'cmp_w1': lin(ks[9], (L, 2, CMP_BLOCK * HEAD_DIM, CMP_HIDDEN), CMP_BLOCK * HEAD_DIM),
        'cmp_b1': 0.02 * nrm(ks[10], (L, 2, CMP_HIDDEN)),
        'cmp_w2': lin(ks[11], (L, 2, CMP_HIDDEN, HEAD_DIM), CMP_HIDDEN),
        'w_up_pool': lin(ks[12], (L, POOL_WIDTH, D), POOL_WIDTH),
        'w_up_attn': lin(ks[13], (L, ATTN_WIDTH, D), ATTN_WIDTH),
        'w_out': lin(ks[14], (L, D, D), D),
        'norm2_g': 1.0 + 0.02 * nrm(ks[15], (L, D)),
        'router_g_w': lin(ks[16], (L, D, N_GROUPS), D),
        'router_g_b': 0.01 * nrm(ks[17], (L, N_GROUPS)),
        'router_e_w': lin(ks[18], (L, D, N_EXPERTS), D),
        'router_e_b': 0.01 * nrm(ks[19], (L, N_EXPERTS)),
        'exp_w_gate': lin(ks[20], (L, N_EXPERTS, D, EXPERT_FF), D),
        'exp_w_up': lin(ks[21], (L, N_EXPERTS, D, EXPERT_FF), D),
        'exp_w_down': lin(ks[22], (L, N_EXPERTS, EXPERT_FF, D), EXPERT_FF),
        'final_g': 1.0 + 0.02 * nrm(ks[23], (D,)),
    }


def reference(x, c, ada_w, ada_b, norm1_g, w_in, pool_w, pool_scale, cmp_pos, cmp_w1, cmp_b1,
              cmp_w2, w_up_pool, w_up_attn, w_out, norm2_g, router_g_w, router_g_b,
              router_e_w, router_e_b, exp_w_gate, exp_w_up, exp_w_down, final_g):
    for l in range(DEPTH):
        mod = c @ ada_w[l] + ada_b[l]
        sh1, sc1, g1, sh2, sc2, g2 = [m[:, None, :] for m in jnp.split(mod, 6, axis=-1)]
        h = rmsnorm(x, norm1_g[l]) * (1.0 + sc1) + sh1
        x = x + g1 * token_mixer(h, w_in[l], pool_w[l], pool_scale[l], cmp_pos[l], cmp_w1[l],
                                 cmp_b1[l], cmp_w2[l], w_up_pool[l], w_up_attn[l], w_out[l])
        h = rmsnorm(x, norm2_g[l]) * (1.0 + sc2) + sh2
        x = x + g2 * hier_moe(h, router_g_w[l], router_g_b[l], router_e_w[l], router_e_b[l],
                              exp_w_gate[l], exp_w_up[l], exp_w_down[l])
    return rmsnorm(x, final_g)
```

```python
import functools

import numpy as np
import jax
import jax.numpy as jnp
from jax import lax
from jax.experimental import pallas as pl
from jax.experimental.pallas import tpu as pltpu

BF = jnp.bfloat16
F32 = jnp.float32
I32 = jnp.int32

D_MODEL = 1024
POOL_WIDTH = 512
POOL_WINDOWS = (2, 4, 8, 16)
POOL_GROUP = 128
POOL_HALO = 16
N_HEADS = 8
HEAD_DIM = 64
N_KV_GROUPS = 2
HEADS_PER_GROUP = 4
ATTN_WIDTH = 512
KV_WIDTH = 128
CMP_BLOCK = 32
CMP_STRIDE = 16
CMP_HIDDEN = 256
SEL_BLOCK = 64
N_SELECT = 8
WINDOW = 512
Q_CHUNK = 64
N_BRANCH = 3
N_GROUPS = 4
EXPERTS_PER_GROUP = 8
N_EXPERTS = 32
EXPERT_FF = 512
EPS = 1e-6
NEG = -1e30
FORCE_SCORE = 1e4
QK_SCALE = HEAD_DIM ** -0.5

LANES = 128
SEL_KEY_TILE = 512
VMEM_LIMIT = 56 * 1024 * 1024

C_POOL = 0
C_Q = C_POOL + POOL_WIDTH
C_KC = C_Q + ATTN_WIDTH
C_VC = C_KC + KV_WIDTH
C_KS = C_VC + KV_WIDTH
C_VS = C_KS + 2 * KV_WIDTH
C_KW = C_VS + 2 * KV_WIDTH
C_VW = C_KW + 2 * KV_WIDTH
C_BG = C_VW + 2 * KV_WIDTH
C_MG = C_BG + LANES
C_END = C_MG + 2 * D_MODEL


def _dot(a, b):
    return jnp.dot(a, b, preferred_element_type=F32)


def _dot_nt(a, b):
    return lax.dot_general(a, b, (((1,), (1,)), ((), ())), preferred_element_type=F32)


def _params(*sem):
    return pltpu.CompilerParams(dimension_semantics=sem, vmem_limit_bytes=VMEM_LIMIT)


def _ada_kernel(c_ref, w_ref, b_ref, o_ref):
    o_ref[...] = _dot(c_ref[...].astype(BF), w_ref[...].astype(BF)) + b_ref[...]


def _ada(c, w, b):
    B, D = c.shape
    n = w.shape[1]
    tn = 1024
    return pl.pallas_call(
        _ada_kernel,
        grid=(n // tn,),
        in_specs=[pl.BlockSpec((B, D), lambda j: (0, 0)),
                  pl.BlockSpec((D, tn), lambda j: (0, j)),
                  pl.BlockSpec((1, tn), lambda j: (0, j))],
        out_specs=pl.BlockSpec((B, tn), lambda j: (0, j)),
        out_shape=jax.ShapeDtypeStruct((B, n), F32),
        compiler_params=_params("arbitrary"),
        name="ada_mod",
    )(c, w, b.reshape(1, n))


def _inproj_kernel(x_ref, mod_ref, g_ref, w_ref, up_ref, q_ref, kc_ref, vc_ref,
                   ks_ref, vs_ref, kw_ref, vw_ref, bg_ref, gm_ref):
    x = x_ref[...]
    r = lax.rsqrt(jnp.mean(x * x, axis=-1, keepdims=True) + EPS)
    h = x * r * g_ref[...] * (1.0 + mod_ref[1:2, :]) + mod_ref[0:1, :]
    hb = h.astype(BF)

    def proj(a, b):
        return _dot(hb, w_ref[:, a:b])

    up_ref[...] = proj(C_POOL, C_Q)
    q_ref[...] = (proj(C_Q, C_KC) * QK_SCALE).astype(BF)
    kc_ref[...] = proj(C_KC, C_VC).astype(BF)
    vc_ref[...] = proj(C_VC, C_KS).astype(BF)
    ks_ref[...] = proj(C_KS, C_VS).astype(BF)
    vs_ref[...] = proj(C_VS, C_KW).astype(BF)
    kw_ref[...] = proj(C_KW, C_VW).astype(BF)
    vw_ref[...] = proj(C_VW, C_BG).astype(BF)
    bg_ref[...] = proj(C_BG, C_MG)
    gm_ref[...] = jax.nn.sigmoid(proj(C_MG, C_END)).astype(BF)


def _inproj(x, mod, g, w):
    B, S, D = x.shape
    tm = 512
    widths = [(POOL_WIDTH, F32), (ATTN_WIDTH, BF), (KV_WIDTH, BF), (KV_WIDTH, BF),
              (2 * KV_WIDTH, BF), (2 * KV_WIDTH, BF), (2 * KV_WIDTH, BF), (2 * KV_WIDTH, BF),
              (LANES, F32), (2 * D_MODEL, BF)]
    row = lambda n: pl.BlockSpec((None, tm, n), lambda b, i: (b, i, 0))
    return pl.pallas_call(
        _inproj_kernel,
        grid=(B, S // tm),
        in_specs=[row(D),
                  pl.BlockSpec((None, 6, D), lambda b, i: (b, 0, 0)),
                  pl.BlockSpec((1, D), lambda b, i: (0, 0)),
                  pl.BlockSpec((D, C_END), lambda b, i: (0, 0))],
        out_specs=[row(n) for n, _ in widths],
        out_shape=[jax.ShapeDtypeStruct((B, S, n), dt) for n, dt in widths],
        compiler_params=_params("arbitrary", "arbitrary"),
        name="norm1_inproj",
    )(x, mod, g, w)


def _gelu_tanh(x):
    return 0.5 * x * (1.0 + jnp.tanh(0.7978845608028654 * (x + 0.044715 * x * x * x)))


def _compress_kernel(x_ref, w1b_ref, pos_ref, w1_ref, b1_ref, w2b_ref, o_ref):
    y = _dot(x_ref[...], w1b_ref[...])
    posc = _dot(pos_ref[...], w1_ref[...])[0:1, :] + b1_ref[...]
    n = y.shape[0]
    acts = []
    for g in range(N_KV_GROUPS):
        first = y[:, g * 2 * CMP_HIDDEN: g * 2 * CMP_HIDDEN + CMP_HIDDEN]
        second = y[:, g * 2 * CMP_HIDDEN + CMP_HIDDEN: (g + 1) * 2 * CMP_HIDDEN]
        pre = first + pltpu.roll(second, n - 1, 0) + posc
        acts.append(_gelu_tanh(pre).astype(BF))
    act = jnp.concatenate(acts, axis=1)
    o_ref[...] = _dot(act, w2b_ref[...]).astype(BF)


def _compress(xk, w1big, pos8, w1, b1, w2big):
    B, n, width = xk.shape
    full = lambda a: pl.BlockSpec(a.shape, lambda b: (0,) * a.ndim)
    return pl.pallas_call(
        _compress_kernel,
        grid=(B,),
        in_specs=[pl.BlockSpec((None, n, width), lambda b: (b, 0, 0)),
                  full(w1big), full(pos8), full(w1), full(b1), full(w2big)],
        out_specs=pl.BlockSpec((None, n, 2 * KV_WIDTH), lambda b: (b, 0, 0)),
        out_shape=jax.ShapeDtypeStruct((B, n, 2 * KV_WIDTH), BF),
        compiler_params=_params("arbitrary"),
        name="compress",
    )(xk, w1big, pos8, w1, b1, w2big)


def _masked_exp(s, mask):
    sm = jnp.where(mask, s, NEG)
    m = jnp.max(sm, axis=-1, keepdims=True)
    p = jnp.where(mask, jnp.exp(sm - m), 0.0)
    return p, jnp.sum(p, axis=-1, keepdims=True)


def _safe_inv(l):
    return jnp.where(l > 0.0, 1.0 / jnp.where(l > 0.0, l, 1.0), 0.0)


def _attn_kernel(q_ref, kc_ref, vc_ref, ks_ref, vs_ref, kw_ref, vw_ref, bg_ref,
                 ov_ref, ex_ref, o_ref):
    ci = pl.program_id(1)
    q0 = ci * Q_CHUNK
    Q, H = Q_CHUNK, HEADS_PER_GROUP
    q = q_ref[...]
    sig = jax.nn.sigmoid(bg_ref[...])
    lane = lax.broadcasted_iota(I32, (Q, LANES), 1)
    lo_half = lane < HEAD_DIM
    t_q = q0 + lax.broadcasted_iota(I32, (Q, 1), 0)
    t_r = jnp.concatenate([t_q] * H, axis=0)

    for g in range(N_KV_GROUPS):
        parts = []
        for h in range(H):
            head = g * H + h
            slab = q[:, (head // 2) * LANES:(head // 2 + 1) * LANES]
            keep = lo_half if head % 2 == 0 else jnp.logical_not(lo_half)
            parts.append(jnp.where(keep, slab, jnp.zeros_like(slab)))
        qs = jnp.concatenate(parts, axis=0)
        gc = slice(g * LANES, (g + 1) * LANES)

        s1 = _dot_nt(qs, kc_ref[:, gc])
        n_idx = lax.broadcasted_iota(I32, s1.shape, 1)
        m1 = (n_idx * CMP_STRIDE + (CMP_BLOCK - 1)) <= t_r
        p1, l1 = _masked_exp(s1, m1)
        inv1 = _safe_inv(l1)
        o1 = _dot(p1.astype(BF), vc_ref[:, gc])

        p1n = p1 * inv1
        psum = p1n[0:Q]
        for h in range(1, H):
            psum = psum + p1n[h * Q:(h + 1) * Q]
        hi = psum.astype(BF)
        lo = (psum - hi.astype(F32)).astype(BF)
        ps = _dot(hi, ov_ref[...]) + _dot(lo, ov_ref[...])
        n_blk = ex_ref.shape[0] * (SEL_KEY_TILE // SEL_BLOCK)
        forced = (lane == 0) | (lane == ci) | (lane == ci - 1)
        score = jnp.where(forced, FORCE_SCORE, jnp.where(lane <= ci, ps, NEG))
        score = jnp.where(lane < n_blk, score, -jnp.inf)
        rank = jnp.zeros((Q, LANES), I32)
        for jp in range(n_blk):
            c = score[:, jp:jp + 1]
            beats = (c > score) | ((c == score) & (lane > jp))
            rank = rank + beats.astype(I32)
        sel = ((rank < N_SELECT) & (lane < n_blk)).astype(BF)

        def sel_body(kt, carry):
            m, l, acc = carry
            k0 = pl.multiple_of(kt * SEL_KEY_TILE, SEL_KEY_TILE)
            kt_ = ks_ref[pl.ds(k0, SEL_KEY_TILE), gc]
            vt_ = vs_ref[pl.ds(k0, SEL_KEY_TILE), gc]
            s = _dot_nt(qs, kt_)
            selk = _dot(sel, ex_ref[kt])
            kpos = k0 + lax.broadcasted_iota(I32, selk.shape, 1)
            mk = (selk > 0.5) & (kpos <= t_q)
            mk = jnp.concatenate([mk] * H, axis=0)
            sm = jnp.where(mk, s, NEG)
            m_new = jnp.maximum(m, jnp.max(sm, axis=-1, keepdims=True))
            alpha = jnp.exp(m - m_new)
            p = jnp.where(mk, jnp.exp(sm - m_new), 0.0)
            l = alpha * l + jnp.sum(p, axis=-1, keepdims=True)
            acc = alpha * acc + _dot(p.astype(BF), vt_)
            return m_new, l, acc

        n_tiles = (ci + SEL_KEY_TILE // SEL_BLOCK) // (SEL_KEY_TILE // SEL_BLOCK)
        init = (jnp.full((H * Q, 1), NEG, F32), jnp.zeros((H * Q, 1), F32),
                jnp.zeros((H * Q, LANES), F32))
        _, l2, o2 = lax.fori_loop(0, n_tiles, sel_body, init)
        inv2 = _safe_inv(l2)

        w0 = pl.multiple_of(jnp.maximum(q0 - WINDOW, 0), Q_CHUNK)
        kwt = kw_ref[pl.ds(w0, WINDOW + Q_CHUNK), gc]
        vwt = vw_ref[pl.ds(w0, WINDOW + Q_CHUNK), gc]
        s3 = _dot_nt(qs, kwt)
        kpos3 = w0 + lax.broadcasted_iota(I32, s3.shape, 1)
        m3 = (kpos3 <= t_r) & (kpos3 > t_r - WINDOW)
        p3, l3 = _masked_exp(s3, m3)
        inv3 = _safe_inv(l3)
        o3 = _dot(p3.astype(BF), vwt)

        heads = []
        for h in range(H):
            rows = slice(h * Q, (h + 1) * Q)
            col = g * H + h
            g1 = sig[:, col:col + 1]
            g2 = sig[:, N_HEADS + col:N_HEADS + col + 1]
            g3 = sig[:, 2 * N_HEADS + col:2 * N_HEADS + col + 1]
            heads.append((g1 * inv1[rows]) * o1[rows] + (g2 * inv2[rows]) * o2[rows]
                         + (g3 * inv3[rows]) * o3[rows])
        for k in range(H // 2):
            slab = jnp.where(lo_half, heads[2 * k], heads[2 * k + 1])
            c0 = (g * (H // 2) + k) * LANES
            o_ref[:, c0:c0 + LANES] = slab.astype(BF)


def _attention(q, kc, vc, ks, vs, kw, vw, bg, ov, ex):
    B, S, _ = q.shape
    nq = S // Q_CHUNK
    per_b = lambda a: pl.BlockSpec((None,) + a.shape[1:], lambda b, i: (b, 0, 0))
    full = lambda a: pl.BlockSpec(a.shape, lambda b, i: (0,) * a.ndim)
    return pl.pallas_call(
        _attn_kernel,
        grid=(B, nq),
        in_specs=[pl.BlockSpec((None, Q_CHUNK, ATTN_WIDTH), lambda b, i: (b, i, 0)),
                  per_b(kc), per_b(vc), per_b(ks), per_b(vs), per_b(kw), per_b(vw),
                  pl.BlockSpec((None, Q_CHUNK, LANES), lambda b, i: (b, i, 0)),
                  full(ov), full(ex)],
        out_specs=pl.BlockSpec((None, Q_CHUNK, ATTN_WIDTH), lambda b, i: (b, i, 0)),
        out_shape=jax.ShapeDtypeStruct((B, S, ATTN_WIDTH), BF),
        compiler_params=_params("arbitrary", "arbitrary"),
        name="nsa_attention",
    )(q, kc, vc, ks, vs, kw, vw, bg, ov, ex)


def _mixer_out_kernel(upc_ref, upp_ref, o_ref, gm_ref, x_ref, mod_ref, pw_ref, psc_ref,
                      wup_ref, wua_ref, wo_ref, g2_ref, wr_ref, br_ref,
                      x1_ref, h2_ref, ridx_ref, rw_ref):
    i = pl.program_id(1)
    tm = upc_ref.shape[0]
    prev = upp_ref[...] * (i > 0).astype(F32)
    ext = jnp.concatenate([prev, upc_ref[...]], axis=0)
    t = i * tm + lax.broadcasted_iota(I32, (tm, 1), 0)
    ys = []
    for gi, w in enumerate(POOL_WINDOWS):
        u = ext[:, gi * POOL_GROUP:(gi + 1) * POOL_GROUP]
        acc = u
        shift = 1
        while shift < w:
            acc = acc + pltpu.roll(acc, shift, 0)
            shift *= 2
        inv_cnt = 1.0 / jnp.minimum(t + 1, w).astype(F32)
        p = acc[POOL_HALO:] * inv_cnt - u[POOL_HALO:]
        ys.append(_dot(p.astype(BF), pw_ref[gi]))
    y = jnp.concatenate(ys, axis=1) * psc_ref[...]
    y_pool = _dot(y.astype(BF), wup_ref[...])
    y_attn = _dot(o_ref[...], wua_ref[...])
    gm = gm_ref[...].astype(F32)
    mix = gm[:, :D_MODEL] * y_pool + gm[:, D_MODEL:] * y_attn
    x1 = x_ref[...] + mod_ref[2:3, :] * _dot(mix.astype(BF), wo_ref[...])
    x1_ref[...] = x1
    r = lax.rsqrt(jnp.mean(x1 * x1, axis=-1, keepdims=True) + EPS)
    h2 = (x1 * r * g2_ref[...] * (1.0 + mod_ref[4:5, :]) + mod_ref[3:4, :]).astype(BF)
    h2_ref[...] = h2

    lt = _dot_nt(wr_ref[...], h2) + br_ref[...]
    lg = lt[0:N_GROUPS]
    gmax = jnp.max(lg, axis=0, keepdims=True)
    gi_ = lax.broadcasted_iota(I32, lg.shape, 0)
    gidx = jnp.min(jnp.where(lg == gmax, gi_, N_GROUPS), axis=0, keepdims=True)
    gp = 1.0 / jnp.sum(jnp.exp(lg - gmax), axis=0, keepdims=True)
    E = EXPERTS_PER_GROUP
    le = jnp.zeros((E, tm), F32)
    for gg in range(N_GROUPS):
        le = jnp.where(gidx == gg, lt[8 + gg * E:8 + (gg + 1) * E], le)
    ei = lax.broadcasted_iota(I32, le.shape, 0)
    v1 = jnp.max(le, axis=0, keepdims=True)
    i1 = jnp.min(jnp.where(le == v1, ei, E), axis=0, keepdims=True)
    rest = jnp.where(ei == i1, -jnp.inf, le)
    v2 = jnp.max(rest, axis=0, keepdims=True)
    i2 = jnp.min(jnp.where(rest == v2, ei, E), axis=0, keepdims=True)
    e = jnp.exp(v2 - v1)
    wa = gp / (1.0 + e)
    wb = gp * e / (1.0 + e)
    row = lax.broadcasted_iota(I32, (8, tm), 0)
    ridx_ref[...] = jnp.where(row == 0, gidx * E + i1, jnp.where(row == 1, gidx * E + i2, 0))
    rw_ref[...] = jnp.where(row == 0, wa, jnp.where(row == 1, wb, 0.0))


def _mixer_out(up, o, gm, x, mod, pw, psc, wup, wua, wo, g2, wr, br):
    B, S, D = x.shape
    tm = 256
    nt = S // tm
    row = lambda n: pl.BlockSpec((None, tm, n), lambda b, i: (b, i, 0))
    full = lambda a: pl.BlockSpec(a.shape, lambda b, i: (0,) * a.ndim)
    per = tm // POOL_HALO
    return pl.pallas_call(
        _mixer_out_kernel,
        grid=(B, nt),
        in_specs=[row(POOL_WIDTH),
                  pl.BlockSpec((None, POOL_HALO, POOL_WIDTH),
                               lambda b, i: (b, jnp.maximum(i * per - 1, 0), 0)),
                  row(ATTN_WIDTH), row(2 * D_MODEL), row(D),
                  pl.BlockSpec((None, 6, D), lambda b, i: (b, 0, 0)),
                  full(pw), full(psc), full(wup), full(wua), full(wo), full(g2),
                  full(wr), full(br)],
        out_specs=[row(D), row(D),
                   pl.BlockSpec((None, 8, tm), lambda b, i: (b * nt + i, 0, 0)),
                   pl.BlockSpec((None, 8, tm), lambda b, i: (b * nt + i, 0, 0))],
        out_shape=[jax.ShapeDtypeStruct((B, S, D), F32),
                   jax.ShapeDtypeStruct((B, S, D), BF),
                   jax.ShapeDtypeStruct((B * nt, 8, tm), I32),
                   jax.ShapeDtypeStruct((B * nt, 8, tm), F32)],
        compiler_params=_params("arbitrary", "arbitrary"),
        name="mixer_out_router",
    )(up, up, o, gm, x, mod, pw, psc, wup, wua, wo, g2, wr, br)


def _moe_kernel(h_ref, cmb_ref, x1_ref, mod_ref, wg_ref, wu_ref, wd_ref, fg_ref, o_ref, acc_ref):
    e = pl.program_id(1)

    @pl.when(e == 0)
    def _():
        acc_ref[...] = jnp.zeros_like(acc_ref)

    h = h_ref[...]
    a = _dot(h, wg_ref[...])
    b = _dot(h, wu_ref[...])
    he = (a * jax.nn.sigmoid(a)) * b
    cmb = cmb_ref[...]
    lane = lax.broadcasted_iota(I32, cmb.shape, 1)
    col = jnp.sum(jnp.where(lane == e, cmb, 0.0), axis=-1, keepdims=True)
    acc_ref[...] += col * _dot(he.astype(BF), wd_ref[...])

    @pl.when(e == pl.num_programs(1) - 1)
    def _():
        x2 = x1_ref[...] + mod_ref[5:6, :] * acc_ref[...]
        r = lax.rsqrt(jnp.mean(x2 * x2, axis=-1, keepdims=True) + EPS)
        o_ref[...] = x2 * r * fg_ref[...]


def _moe(h2, cmb, x1, mod, wg, wu, wd, fg):
    N, D = h2.shape
    B = mod.shape[0]
    tm = 1024
    per_b = (N // B) // tm
    return pl.pallas_call(
        _moe_kernel,
        grid=(N // tm, N_EXPERTS),
        in_specs=[pl.BlockSpec((tm, D), lambda i, e: (i, 0)),
                  pl.BlockSpec((tm, N_EXPERTS), lambda i, e: (i, 0)),
                  pl.BlockSpec((tm, D), lambda i, e: (i, 0)),
                  pl.BlockSpec((None, 6, D), lambda i, e: (i // per_b, 0, 0)),
                  pl.BlockSpec((None, D, EXPERT_FF), lambda i, e: (e, 0, 0)),
                  pl.BlockSpec((None, D, EXPERT_FF), lambda i, e: (e, 0, 0)),
                  pl.BlockSpec((None, EXPERT_FF, D), lambda i, e: (e, 0, 0)),
                  pl.BlockSpec((1, D), lambda i, e: (0, 0))],
        out_specs=pl.BlockSpec((tm, D), lambda i, e: (i, 0)),
        out_shape=jax.ShapeDtypeStruct((N, D), F32),
        scratch_shapes=[pltpu.VMEM((tm, D), F32)],
        compiler_params=_params("arbitrary", "arbitrary"),
        name="moe_dense",
    )(h2, cmb, x1, mod, wg, wu, wd, fg)


def _dup_groups(w):
    a, b = w[:, :HEAD_DIM], w[:, HEAD_DIM:]
    return jnp.concatenate([a, a, b, b], axis=1)


def _arrange_w_in(w):
    cuts = np.cumsum([0, POOL_WIDTH, ATTN_WIDTH] + [KV_WIDTH] * 6 + [N_BRANCH * N_HEADS, 2 * D_MODEL])
    pieces = [w[:, int(cuts[i]):int(cuts[i + 1])] for i in range(len(cuts) - 1)]
    pool, q, kc, vc, ks, vs, kw, vw, bg, mg = pieces
    bg = jnp.pad(bg, ((0, 0), (0, LANES - bg.shape[1])))
    out = jnp.concatenate([pool, q, kc, vc, _dup_groups(ks), _dup_groups(vs),
                           _dup_groups(kw), _dup_groups(vw), bg, mg], axis=1)
    return out.astype(BF)


def _compress_weights(pos, w1, b1, w2):
    eye = jnp.eye(N_KV_GROUPS, dtype=F32)
    halves = CMP_BLOCK // CMP_STRIDE
    w1r = w1.reshape(halves, CMP_STRIDE, HEAD_DIM, CMP_HIDDEN)
    w1big = jnp.einsum('hidc,gk->igdkhc', w1r, eye).reshape(
        CMP_STRIDE * N_KV_GROUPS * HEAD_DIM, N_KV_GROUPS * halves * CMP_HIDDEN)
    w2big = jnp.einsum('cd,gk,r->gckrd', w2, eye, jnp.ones((2,), F32)).reshape(
        N_KV_GROUPS * CMP_HIDDEN, N_KV_GROUPS * 2 * HEAD_DIM)
    pos8 = jnp.broadcast_to(pos.reshape(1, CMP_BLOCK * HEAD_DIM), (8, CMP_BLOCK * HEAD_DIM))
    return (w1big.astype(BF), pos8.astype(BF), w1.astype(BF), b1.reshape(1, CMP_HIDDEN),
            w2big.astype(BF))


def _selection_tables(S):
    n_chunks = S // CMP_STRIDE
    n_cmp = n_chunks - CMP_BLOCK // CMP_STRIDE + 1
    n_blk = S // SEL_BLOCK
    s1 = np.arange(n_cmp)[:, None] * CMP_STRIDE
    s2 = np.arange(n_blk)[None, :] * SEL_BLOCK
    ovl = np.clip(np.minimum(s1 + CMP_BLOCK, s2 + SEL_BLOCK) - np.maximum(s1, s2), 0, None) / CMP_BLOCK
    ov = np.zeros((n_chunks, LANES), np.float32)
    ov[:n_cmp, :n_blk] = ovl
    n_tiles = S // SEL_KEY_TILE
    ex = np.zeros((n_tiles, LANES, SEL_KEY_TILE), np.float32)
    for kt in range(n_tiles):
        blk = (kt * SEL_KEY_TILE + np.arange(SEL_KEY_TILE)) // SEL_BLOCK
        ex[kt, blk, np.arange(SEL_KEY_TILE)] = 1.0
    return jnp.asarray(ov, BF), jnp.asarray(ex, BF)


def kernel(x, c, ada_w, ada_b, norm1_g, w_in, pool_w, pool_scale, cmp_pos, cmp_w1, cmp_b1, cmp_w2,
           w_up_pool, w_up_attn, w_out, norm2_g, router_g_w, router_g_b, router_e_w, router_e_b,
           exp_w_gate, exp_w_up, exp_w_down, final_g):
    B, S, D = x.shape
    N = B * S
    for l in range(ada_w.shape[0]):
        mod = _ada(c, ada_w[l], ada_b[l]).reshape(B, 6, D)
        (up, q, kc, vc, ks, vs, kw, vw, bg, gm) = _inproj(
            x, mod, norm1_g[l].reshape(1, D), _arrange_w_in(w_in[l]))
        n_chunks = S // CMP_STRIDE
        kcc = _compress(kc.reshape(B, n_chunks, CMP_STRIDE * KV_WIDTH),
                        *_compress_weights(cmp_pos[l, 0], cmp_w1[l, 0], cmp_b1[l, 0], cmp_w2[l, 0]))
        vcc = _compress(vc.reshape(B, n_chunks, CMP_STRIDE * KV_WIDTH),
                        *_compress_weights(cmp_pos[l, 1], cmp_w1[l, 1], cmp_b1[l, 1], cmp_w2[l, 1]))
        ov, ex = _selection_tables(S)
        o = _attention(q, kcc, vcc, ks, vs, kw, vw, bg, ov, ex)
        wr = jnp.zeros((8 + N_EXPERTS, D), F32)
        wr = wr.at[0:N_GROUPS].set(router_g_w[l].T).at[8:].set(router_e_w[l].T).astype(BF)
        br = jnp.zeros((8 + N_EXPERTS, 1), F32)
        br = br.at[0:N_GROUPS, 0].set(router_g_b[l]).at[8:, 0].set(router_e_b[l])
        x1, h2, ridx, rw = _mixer_out(
            up, o, gm, x, mod, pool_w[l].astype(BF), pool_scale[l].reshape(1, POOL_WIDTH),
            w_up_pool[l].astype(BF), w_up_attn[l].astype(BF), w_out[l].astype(BF),
            norm2_g[l].reshape(1, D), wr, br)
        tmr = ridx.shape[-1]
        ea = ridx[:, 0, :].reshape(N, 1)
        eb = ridx[:, 1, :].reshape(N, 1)
        wa = rw[:, 0, :].reshape(N, 1)
        wb = rw[:, 1, :].reshape(N, 1)
        del tmr
        eids = jnp.arange(N_EXPERTS, dtype=I32)[None, :]
        cmb = jnp.where(ea == eids, wa, 0.0) + jnp.where(eb == eids, wb, 0.0)
        y = _moe(h2.reshape(N, D), cmb, x1.reshape(N, D), mod, exp_w_gate[l].astype(BF),
                 exp_w_up[l].astype(BF), exp_w_down[l].astype(BF), final_g.reshape(1, D))
        x = y.reshape(B, S, D)
    return x
```

```python
import functools

import numpy as np
import jax
import jax.numpy as jnp
from jax import lax
from jax.experimental import pallas as pl
from jax.experimental.pallas import tpu as pltpu

BF = jnp.bfloat16
F32 = jnp.float32
I32 = jnp.int32

D_MODEL = 1024
POOL_WIDTH = 512
POOL_WINDOWS = (2, 4, 8, 16)
POOL_GROUP = 128
POOL_HALO = 16
N_HEADS = 8
HEAD_DIM = 64
N_KV_GROUPS = 2
HEADS_PER_GROUP = 4
ATTN_WIDTH = 512
KV_WIDTH = 128
CMP_BLOCK = 32
CMP_STRIDE = 16
CMP_HIDDEN = 256
SEL_BLOCK = 64
N_SELECT = 8
WINDOW = 512
Q_CHUNK = 64
N_BRANCH = 3
N_GROUPS = 4
EXPERTS_PER_GROUP = 8
N_EXPERTS = 32
EXPERT_FF = 512
EPS = 1e-6
NEG = -1e30
FORCE_SCORE = 1e4
QK_SCALE = HEAD_DIM ** -0.5

PAIRS_PER_GROUP = EXPERTS_PER_GROUP * (EXPERTS_PER_GROUP - 1) // 2
N_CLASS = N_GROUPS * PAIRS_PER_GROUP
N_CLASS_PAD = 128
ROUTE_TILE = 256
EXPERT_TILE_LOG2 = 8
EXPERT_TILE = 1 << EXPERT_TILE_LOG2
DISPATCH_ROWS = 1024
COMBINE_ROWS = 256

LANES = 128
ROW_TILES = D_MODEL // LANES
SEL_KEY_TILE = 512
VMEM_LIMIT = 56 * 1024 * 1024

C_POOL = 0
C_Q = C_POOL + POOL_WIDTH
C_KC = C_Q + ATTN_WIDTH
C_VC = C_KC + KV_WIDTH
C_KS = C_VC + KV_WIDTH
C_VS = C_KS + 2 * KV_WIDTH
C_KW = C_VS + 2 * KV_WIDTH
C_VW = C_KW + 2 * KV_WIDTH
C_BG = C_VW + 2 * KV_WIDTH
C_MG = C_BG + LANES
C_END = C_MG + 2 * D_MODEL


def _dot(a, b):
    return jnp.dot(a, b, preferred_element_type=F32)


def _dot_nt(a, b):
    return lax.dot_general(a, b, (((1,), (1,)), ((), ())), preferred_element_type=F32)


def _params(*sem):
    return pltpu.CompilerParams(dimension_semantics=sem, vmem_limit_bytes=VMEM_LIMIT)


def _ada_kernel(c_ref, w_ref, b_ref, o_ref):
    o_ref[...] = _dot(c_ref[...].astype(BF), w_ref[...].astype(BF)) + b_ref[...]


def _ada(c, w, b):
    B, D = c.shape
    n = w.shape[1]
    tn = 1024
    return pl.pallas_call(
        _ada_kernel,
        grid=(n // tn,),
        in_specs=[pl.BlockSpec((B, D), lambda j: (0, 0)),
                  pl.BlockSpec((D, tn), lambda j: (0, j)),
                  pl.BlockSpec((1, tn), lambda j: (0, j))],
        out_specs=pl.BlockSpec((B, tn), lambda j: (0, j)),
        out_shape=jax.ShapeDtypeStruct((B, n), F32),
        compiler_params=_params("arbitrary"),
        name="ada_mod",
    )(c, w, b.reshape(1, n))


def _inproj_kernel(x_ref, mod_ref, g_ref, w_ref, up_ref, q_ref, kc_ref, vc_ref,
                   ks_ref, vs_ref, kw_ref, vw_ref, bg_ref, gm_ref):
    x = x_ref[...]
    r = lax.rsqrt(jnp.mean(x * x, axis=-1, keepdims=True) + EPS)
    h = x * r * g_ref[...] * (1.0 + mod_ref[1:2, :]) + mod_ref[0:1, :]
    hb = h.astype(BF)

    def proj(a, b):
        return _dot(hb, w_ref[:, a:b])

    up_ref[...] = proj(C_POOL, C_Q)
    q_ref[...] = (proj(C_Q, C_KC) * QK_SCALE).astype(BF)
    kc_ref[...] = proj(C_KC, C_VC).astype(BF)
    vc_ref[...] = proj(C_VC, C_KS).astype(BF)
    ks_ref[...] = proj(C_KS, C_VS).astype(BF)
    vs_ref[...] = proj(C_VS, C_KW).astype(BF)
    kw_ref[...] = proj(C_KW, C_VW).astype(BF)
    vw_ref[...] = proj(C_VW, C_BG).astype(BF)
    bg_ref[...] = proj(C_BG, C_MG)
    gm_ref[...] = jax.nn.sigmoid(proj(C_MG, C_END)).astype(BF)


def _inproj(x, mod, g, w):
    B, S, D = x.shape
    tm = 512
    widths = [(POOL_WIDTH, F32), (ATTN_WIDTH, BF), (KV_WIDTH, BF), (KV_WIDTH, BF),
              (2 * KV_WIDTH, BF), (2 * KV_WIDTH, BF), (2 * KV_WIDTH, BF), (2 * KV_WIDTH, BF),
              (LANES, F32), (2 * D_MODEL, BF)]
    row = lambda n: pl.BlockSpec((None, tm, n), lambda b, i: (b, i, 0))
    return pl.pallas_call(
        _inproj_kernel,
        grid=(B, S // tm),
        in_specs=[row(D),
                  pl.BlockSpec((None, 6, D), lambda b, i: (b, 0, 0)),
                  pl.BlockSpec((1, D), lambda b, i: (0, 0)),
                  pl.BlockSpec((D, C_END), lambda b, i: (0, 0))],
        out_specs=[row(n) for n, _ in widths],
        out_shape=[jax.ShapeDtypeStruct((B, S, n), dt) for n, dt in widths],
        compiler_params=_params("arbitrary", "arbitrary"),
        name="norm1_inproj",
    )(x, mod, g, w)


def _gelu_tanh(x):
    return 0.5 * x * (1.0 + jnp.tanh(0.7978845608028654 * (x + 0.044715 * x * x * x)))


def _compress_kernel(x_ref, w1b_ref, pos_ref, w1_ref, b1_ref, w2b_ref, o_ref):
    y = _dot(x_ref[...], w1b_ref[...])
    posc = _dot(pos_ref[...], w1_ref[...])[0:1, :] + b1_ref[...]
    n = y.shape[0]
    acts = []
    for g in range(N_KV_GROUPS):
        first = y[:, g * 2 * CMP_HIDDEN: g * 2 * CMP_HIDDEN + CMP_HIDDEN]
        second = y[:, g * 2 * CMP_HIDDEN + CMP_HIDDEN: (g + 1) * 2 * CMP_HIDDEN]
        pre = first + pltpu.roll(second, n - 1, 0) + posc
        acts.append(_gelu_tanh(pre).astype(BF))
    act = jnp.concatenate(acts, axis=1)
    o_ref[...] = _dot(act, w2b_ref[...]).astype(BF)


def _compress(xk, w1big, pos8, w1, b1, w2big):
    B, n, width = xk.shape
    full = lambda a: pl.BlockSpec(a.shape, lambda b: (0,) * a.ndim)
    return pl.pallas_call(
        _compress_kernel,
        grid=(B,),
        in_specs=[pl.BlockSpec((None, n, width), lambda b: (b, 0, 0)),
                  full(w1big), full(pos8), full(w1), full(b1), full(w2big)],
        out_specs=pl.BlockSpec((None, n, 2 * KV_WIDTH), lambda b: (b, 0, 0)),
        out_shape=jax.ShapeDtypeStruct((B, n, 2 * KV_WIDTH), BF),
        compiler_params=_params("arbitrary"),
        name="compress",
    )(xk, w1big, pos8, w1, b1, w2big)


def _masked_exp(s, mask):
    sm = jnp.where(mask, s, NEG)
    m = jnp.max(sm, axis=-1, keepdims=True)
    p = jnp.where(mask, jnp.exp(sm - m), 0.0)
    return p, jnp.sum(p, axis=-1, keepdims=True)


def _safe_inv(l):
    return jnp.where(l > 0.0, 1.0 / jnp.where(l > 0.0, l, 1.0), 0.0)


def _attn_kernel(q_ref, kc_ref, vc_ref, ks_ref, vs_ref, kw_ref, vw_ref, bg_ref,
                 ov_ref, ex_ref, o_ref):
    ci = pl.program_id(1)
    q0 = ci * Q_CHUNK
    Q, H = Q_CHUNK, HEADS_PER_GROUP
    q = q_ref[...]
    sig = jax.nn.sigmoid(bg_ref[...])
    lane = lax.broadcasted_iota(I32, (Q, LANES), 1)
    lo_half = lane < HEAD_DIM
    t_q = q0 + lax.broadcasted_iota(I32, (Q, 1), 0)
    t_r = jnp.concatenate([t_q] * H, axis=0)

    for g in range(N_KV_GROUPS):
        parts = []
        for h in range(H):
            head = g * H + h
            slab = q[:, (head // 2) * LANES:(head // 2 + 1) * LANES]
            keep = lo_half if head % 2 == 0 else jnp.logical_not(lo_half)
            parts.append(jnp.where(keep, slab, jnp.zeros_like(slab)))
        qs = jnp.concatenate(parts, axis=0)
        gc = slice(g * LANES, (g + 1) * LANES)

        s1 = _dot_nt(qs, kc_ref[:, gc])
        n_idx = lax.broadcasted_iota(I32, s1.shape, 1)
        m1 = (n_idx * CMP_STRIDE + (CMP_BLOCK - 1)) <= t_r
        p1, l1 = _masked_exp(s1, m1)
        inv1 = _safe_inv(l1)
        o1 = _dot(p1.astype(BF), vc_ref[:, gc])

        p1n = p1 * inv1
        psum = p1n[0:Q]
        for h in range(1, H):
            psum = psum + p1n[h * Q:(h + 1) * Q]
        hi = psum.astype(BF)
        lo = (psum - hi.astype(F32)).astype(BF)
        ps = _dot(hi, ov_ref[...]) + _dot(lo, ov_ref[...])
        n_blk = ex_ref.shape[0] * (SEL_KEY_TILE // SEL_BLOCK)
        forced = (lane == 0) | (lane == ci) | (lane == ci - 1)
        score = jnp.where(forced, FORCE_SCORE, jnp.where(lane <= ci, ps, NEG))
        score = jnp.where(lane < n_blk, score, -jnp.inf)
        rank = jnp.zeros((Q, LANES), I32)
        for jp in range(n_blk):
            c = score[:, jp:jp + 1]
            beats = (c > score) | ((c == score) & (lane > jp))
            rank = rank + beats.astype(I32)
        sel = ((rank < N_SELECT) & (lane < n_blk)).astype(BF)

        def sel_body(kt, carry):
            m, l, acc = carry
            k0 = pl.multiple_of(kt * SEL_KEY_TILE, SEL_KEY_TILE)
            kt_ = ks_ref[pl.ds(k0, SEL_KEY_TILE), gc]
            vt_ = vs_ref[pl.ds(k0, SEL_KEY_TILE), gc]
            s = _dot_nt(qs, kt_)
            selk = _dot(sel, ex_ref[kt])
            kpos = k0 + lax.broadcasted_iota(I32, selk.shape, 1)
            mk = (selk > 0.5) & (kpos <= t_q)
            mk = jnp.concatenate([mk] * H, axis=0)
            sm = jnp.where(mk, s, NEG)
            m_new = jnp.maximum(m, jnp.max(sm, axis=-1, keepdims=True))
            alpha = jnp.exp(m - m_new)
            p = jnp.where(mk, jnp.exp(sm - m_new), 0.0)
            l = alpha * l + jnp.sum(p, axis=-1, keepdims=True)
            acc = alpha * acc + _dot(p.astype(BF), vt_)
            return m_new, l, acc

        n_tiles = (ci + SEL_KEY_TILE // SEL_BLOCK) // (SEL_KEY_TILE // SEL_BLOCK)
        init = (jnp.full((H * Q, 1), NEG, F32), jnp.zeros((H * Q, 1), F32),
                jnp.zeros((H * Q, LANES), F32))
        _, l2, o2 = lax.fori_loop(0, n_tiles, sel_body, init)
        inv2 = _safe_inv(l2)

        w0 = pl.multiple_of(jnp.maximum(q0 - WINDOW, 0), Q_CHUNK)
        kwt = kw_ref[pl.ds(w0, WINDOW + Q_CHUNK), gc]
        vwt = vw_ref[pl.ds(w0, WINDOW + Q_CHUNK), gc]
        s3 = _dot_nt(qs, kwt)
        kpos3 = w0 + lax.broadcasted_iota(I32, s3.shape, 1)
        m3 = (kpos3 <= t_r) & (kpos3 > t_r - WINDOW)
        p3, l3 = _masked_exp(s3, m3)
        inv3 = _safe_inv(l3)
        o3 = _dot(p3.astype(BF), vwt)

        heads = []
        for h in range(H):
            rows = slice(h * Q, (h + 1) * Q)
            col = g * H + h
            g1 = sig[:, col:col + 1]
            g2 = sig[:, N_HEADS + col:N_HEADS + col + 1]
            g3 = sig[:, 2 * N_HEADS + col:2 * N_HEADS + col + 1]
            heads.append((g1 * inv1[rows]) * o1[rows] + (g2 * inv2[rows]) * o2[rows]
                         + (g3 * inv3[rows]) * o3[rows])
        for k in range(H // 2):
            slab = jnp.where(lo_half, heads[2 * k], heads[2 * k + 1])
            c0 = (g * (H // 2) + k) * LANES
            o_ref[:, c0:c0 + LANES] = slab.astype(BF)


def _attention(q, kc, vc, ks, vs, kw, vw, bg, ov, ex):
    B, S, _ = q.shape
    nq = S // Q_CHUNK
    per_b = lambda a: pl.BlockSpec((None,) + a.shape[1:], lambda b, i: (b, 0, 0))
    full = lambda a: pl.BlockSpec(a.shape, lambda b, i: (0,) * a.ndim)
    return pl.pallas_call(
        _attn_kernel,
        grid=(B, nq),
        in_specs=[pl.BlockSpec((None, Q_CHUNK, ATTN_WIDTH), lambda b, i: (b, i, 0)),
                  per_b(kc), per_b(vc), per_b(ks), per_b(vs), per_b(kw), per_b(vw),
                  pl.BlockSpec((None, Q_CHUNK, LANES), lambda b, i: (b, i, 0)),
                  full(ov), full(ex)],
        out_specs=pl.BlockSpec((None, Q_CHUNK, ATTN_WIDTH), lambda b, i: (b, i, 0)),
        out_shape=jax.ShapeDtypeStruct((B, S, ATTN_WIDTH), BF),
        compiler_params=_params("arbitrary", "arbitrary"),
        name="nsa_attention",
    )(q, kc, vc, ks, vs, kw, vw, bg, ov, ex)


def _mixer_out_kernel(upc_ref, upp_ref, o_ref, gm_ref, x_ref, mod_ref, pw_ref, psc_ref,
                      wup_ref, wua_ref, wo_ref, g2_ref, wr_ref, br_ref, tri_ref,
                      x1_ref, h2t_ref, crk_ref, rw_ref, cnt_ref, cnt_scr):
    i = pl.program_id(1)

    @pl.when((pl.program_id(0) == 0) & (i == 0))
    def _():
        cnt_scr[...] = jnp.zeros_like(cnt_scr)

    tm = upc_ref.shape[0]
    prev = upp_ref[...] * (i > 0).astype(F32)
    ext = jnp.concatenate([prev, upc_ref[...]], axis=0)
    t = i * tm + lax.broadcasted_iota(I32, (tm, 1), 0)
    ys = []
    for gi, w in enumerate(POOL_WINDOWS):
        u = ext[:, gi * POOL_GROUP:(gi + 1) * POOL_GROUP]
        acc = u
        shift = 1
        while shift < w:
            acc = acc + pltpu.roll(acc, shift, 0)
            shift *= 2
        inv_cnt = 1.0 / jnp.minimum(t + 1, w).astype(F32)
        p = acc[POOL_HALO:] * inv_cnt - u[POOL_HALO:]
        ys.append(_dot(p.astype(BF), pw_ref[gi]))
    y = jnp.concatenate(ys, axis=1) * psc_ref[...]
    y_pool = _dot(y.astype(BF), wup_ref[...])
    y_attn = _dot(o_ref[...], wua_ref[...])
    gm = gm_ref[...].astype(F32)
    mix = gm[:, :D_MODEL] * y_pool + gm[:, D_MODEL:] * y_attn
    x1 = x_ref[...] + mod_ref[2:3, :] * _dot(mix.astype(BF), wo_ref[...])
    x1_ref[...] = x1
    r = lax.rsqrt(jnp.mean(x1 * x1, axis=-1, keepdims=True) + EPS)
    h2f = x1 * r * g2_ref[...] * (1.0 + mod_ref[4:5, :]) + mod_ref[3:4, :]
    h2 = h2f.astype(BF)
    for k in range(ROW_TILES):
        h2t_ref[pl.ds(k, tm, stride=ROW_TILES), :] = h2f[:, k * LANES:(k + 1) * LANES]

    lt = _dot_nt(wr_ref[...], h2) + br_ref[...]
    lg = lt[0:N_GROUPS]
    gmax = jnp.max(lg, axis=0, keepdims=True)
    gi_ = lax.broadcasted_iota(I32, lg.shape, 0)
    gidx = jnp.min(jnp.where(lg == gmax, gi_, N_GROUPS), axis=0, keepdims=True)
    gp = 1.0 / jnp.sum(jnp.exp(lg - gmax), axis=0, keepdims=True)
    E = EXPERTS_PER_GROUP
    le = jnp.zeros((E, tm), F32)
    for gg in range(N_GROUPS):
        le = jnp.where(gidx == gg, lt[8 + gg * E:8 + (gg + 1) * E], le)
    ei = lax.broadcasted_iota(I32, le.shape, 0)
    v1 = jnp.max(le, axis=0, keepdims=True)
    i1 = jnp.min(jnp.where(le == v1, ei, E), axis=0, keepdims=True)
    rest = jnp.where(ei == i1, -jnp.inf, le)
    v2 = jnp.max(rest, axis=0, keepdims=True)
    i2 = jnp.min(jnp.where(rest == v2, ei, E), axis=0, keepdims=True)
    e = jnp.exp(v2 - v1)
    wa = gp / (1.0 + e)
    wb = gp * e / (1.0 + e)
    lo = jnp.minimum(i1, i2)
    hi = jnp.maximum(i1, i2)
    pair = lax.shift_right_logical(lo * (2 * E - 1 - lo), 1) + hi - lo - 1
    cls = gidx * PAIRS_PER_GROUP + pair
    first_lo = i1 < i2
    w_lo = jnp.where(first_lo, wa, wb)
    w_hi = jnp.where(first_lo, wb, wa)
    oh = lax.broadcasted_iota(I32, (N_CLASS_PAD, tm), 0) == cls
    before = _dot(oh.astype(BF), tri_ref[...]) + cnt_scr[:, 0:1]
    rank = jnp.sum(jnp.where(oh, before, 0.0), axis=0, keepdims=True).astype(I32)
    cnt_scr[...] = cnt_scr[...] + jnp.sum(oh.astype(F32), axis=1, keepdims=True)
    cnt_ref[...] = cnt_scr[...]
    row = lax.broadcasted_iota(I32, (8, tm), 0)
    crk_ref[...] = jnp.where(row == 0, cls, jnp.where(row == 1, rank, 0))
    rw_ref[...] = jnp.where(row == 0, w_lo, jnp.where(row == 1, w_hi, 0.0))


def _mixer_out(up, o, gm, x, mod, pw, psc, wup, wua, wo, g2, wr, br):
    B, S, D = x.shape
    tm = ROUTE_TILE
    nt = S // tm
    tri = jnp.asarray(np.triu(np.ones((tm, tm), np.float32), k=1), BF)
    row = lambda n: pl.BlockSpec((None, tm, n), lambda b, i: (b, i, 0))
    full = lambda a: pl.BlockSpec(a.shape, lambda b, i: (0,) * a.ndim)
    per = tm // POOL_HALO
    return pl.pallas_call(
        _mixer_out_kernel,
        grid=(B, nt),
        in_specs=[row(POOL_WIDTH),
                  pl.BlockSpec((None, POOL_HALO, POOL_WIDTH),
                               lambda b, i: (b, jnp.maximum(i * per - 1, 0), 0)),
                  row(ATTN_WIDTH), row(2 * D_MODEL), row(D),
                  pl.BlockSpec((None, 6, D), lambda b, i: (b, 0, 0)),
                  full(pw), full(psc), full(wup), full(wua), full(wo), full(g2),
                  full(wr), full(br), full(tri)],
        out_specs=[row(D),
                   pl.BlockSpec((tm * ROW_TILES, LANES), lambda b, i: (b * nt + i, 0)),
                   pl.BlockSpec((None, 8, tm), lambda b, i: (b * nt + i, 0, 0)),
                   pl.BlockSpec((None, 8, tm), lambda b, i: (b * nt + i, 0, 0)),
                   pl.BlockSpec((N_CLASS_PAD, LANES), lambda b, i: (0, 0))],
        out_shape=[jax.ShapeDtypeStruct((B, S, D), F32),
                   jax.ShapeDtypeStruct((B * S * ROW_TILES, LANES), F32),
                   jax.ShapeDtypeStruct((B * nt, 8, tm), I32),
                   jax.ShapeDtypeStruct((B * nt, 8, tm), F32),
                   jax.ShapeDtypeStruct((N_CLASS_PAD, LANES), F32)],
        scratch_shapes=[pltpu.VMEM((N_CLASS_PAD, LANES), F32)],
        compiler_params=_params("arbitrary", "arbitrary"),
        name="mixer_out_router",
    )(up, up, o, gm, x, mod, pw, psc, wup, wua, wo, g2, wr, br, tri)


def _plan_kernel(crk_ref, cnt_ref, etab_ref, pos_ref, tmap_ref):
    C = N_CLASS_PAD
    cnt = cnt_ref[:, 0:1].astype(I32)
    ntile = lax.shift_right_logical(cnt + (EXPERT_TILE - 1), EXPERT_TILE_LOG2)
    ntile_f = ntile.astype(F32)
    r = lax.broadcasted_iota(I32, (C, C), 0)
    c = lax.broadcasted_iota(I32, (C, C), 1)
    lower = (c < r).astype(BF)
    first = _dot(lower, jnp.broadcast_to(ntile_f, (C, LANES)).astype(BF))[:, 0:1]
    last = first + ntile_f
    total = jnp.sum(ntile_f, axis=0, keepdims=True)
    off = (first * EXPERT_TILE).astype(I32)

    def body(i, carry):
        cls = crk_ref[i, 0:1, :]
        rank = crk_ref[i, 1:2, :]
        oh = lax.broadcasted_iota(I32, (C, cls.shape[1]), 0) == cls
        pos_ref[pl.ds(i, 1), :] = jnp.sum(jnp.where(oh, off, 0), axis=0, keepdims=True) + rank
        return carry

    lax.fori_loop(0, crk_ref.shape[0], body, 0)

    nj = tmap_ref.shape[1]
    j = lax.broadcasted_iota(I32, (1, nj), 1).astype(F32)
    jj = jnp.minimum(j, total - 1.0)
    tcls = jnp.sum((last <= jj).astype(I32), axis=0, keepdims=True)
    oh2 = lax.broadcasted_iota(I32, (C, nj), 0) == tcls
    elo = jnp.sum(jnp.where(oh2, etab_ref[:, 0:1], 0), axis=0, keepdims=True)
    ehi = jnp.sum(jnp.where(oh2, etab_ref[:, 1:2], 0), axis=0, keepdims=True)
    row = lax.broadcasted_iota(I32, (8, nj), 0)
    tmap_ref[...] = jnp.where(
        row == 0, elo, jnp.where(row == 1, ehi, jnp.where(
            row == 2, (j < total).astype(I32), jnp.where(row == 3, jj.astype(I32), 0))))


def _plan(crk, cnt, n_tiles_pad):
    nt, _, tm = crk.shape
    etab = np.zeros((N_CLASS_PAD, LANES), np.int32)
    cid = 0
    for g in range(N_GROUPS):
        for lo in range(EXPERTS_PER_GROUP):
            for hi in range(lo + 1, EXPERTS_PER_GROUP):
                etab[cid, 0] = g * EXPERTS_PER_GROUP + lo
                etab[cid, 1] = g * EXPERTS_PER_GROUP + hi
                cid += 1
    etab = jnp.asarray(etab)
    full = lambda a: pl.BlockSpec(a.shape, lambda i: (0,) * a.ndim)
    return pl.pallas_call(
        _plan_kernel,
        grid=(1,),
        in_specs=[full(crk), full(cnt), full(etab)],
        out_specs=[pl.BlockSpec((nt, tm), lambda i: (0, 0)),
                   pl.BlockSpec((8, n_tiles_pad), lambda i: (0, 0))],
        out_shape=[jax.ShapeDtypeStruct((nt, tm), I32),
                   jax.ShapeDtypeStruct((8, n_tiles_pad), I32)],
        compiler_params=_params("arbitrary"),
        name="moe_plan",
    )(crk, cnt, etab)


def _dispatch_kernel(pos_ref, h_ref, xs_in_ref, xs_ref, sem):
    del xs_in_ref
    i = pl.program_id(0)
    rows = h_ref.shape[0] // ROW_TILES

    def issue(r, carry):
        p = pos_ref[i * rows + r]
        pltpu.make_async_copy(h_ref.at[pl.ds(pl.multiple_of(r * ROW_TILES, ROW_TILES), ROW_TILES)],
                              xs_ref.at[pl.ds(pl.multiple_of(p * ROW_TILES, ROW_TILES), ROW_TILES)],
                              sem).start()
        return carry

    lax.fori_loop(0, rows, issue, 0)
    pltpu.make_async_copy(h_ref, xs_ref.at[pl.ds(0, rows * ROW_TILES)], sem).wait()


def _dispatch(pos, h2t, n_sorted_rows):
    rows = DISPATCH_ROWS
    n = h2t.shape[0] // ROW_TILES
    zeros = jnp.zeros((n_sorted_rows * ROW_TILES, LANES), F32)
    return pl.pallas_call(
        _dispatch_kernel,
        grid_spec=pltpu.PrefetchScalarGridSpec(
            num_scalar_prefetch=1,
            grid=(n // rows,),
            in_specs=[pl.BlockSpec((rows * ROW_TILES, LANES), lambda i, p: (i, 0)),
                      pl.BlockSpec(memory_space=pl.ANY)],
            out_specs=pl.BlockSpec(memory_space=pl.ANY),
            scratch_shapes=[pltpu.SemaphoreType.DMA]),
        out_shape=jax.ShapeDtypeStruct(zeros.shape, F32),
        input_output_aliases={2: 0},
        compiler_params=_params("arbitrary"),
        name="moe_dispatch",
    )(pos, h2t, zeros)


def _expert_kernel(elo_ref, ehi_ref, valid_ref, blk_ref, x_ref, wg0, wu0, wd0, wg1, wu1, wd1, o_ref):
    del elo_ref, ehi_ref, blk_ref
    T = EXPERT_TILE
    valid = valid_ref[pl.program_id(0)] > 0

    @pl.when(jnp.logical_not(valid))
    def _():
        o_ref[...] = jnp.zeros_like(o_ref)

    @pl.when(valid)
    def _():
        x = jnp.concatenate([x_ref[pl.ds(k, T, stride=ROW_TILES), :] for k in range(ROW_TILES)],
                            axis=1).astype(BF)
        for half, (wg, wu, wd) in enumerate(((wg0, wu0, wd0), (wg1, wu1, wd1))):
            a = _dot(x, wg[...])
            b = _dot(x, wu[...])
            he = (a * jax.nn.sigmoid(a)) * b
            y = _dot(he.astype(BF), wd[...])
            for k in range(ROW_TILES):
                o_ref[pl.ds(half * ROW_TILES + k, T, stride=2 * ROW_TILES), :] = (
                    y[:, k * LANES:(k + 1) * LANES])


def _experts(tmap, xs, wg, wu, wd, n_tiles):
    T = EXPERT_TILE
    D, F = D_MODEL, EXPERT_FF
    lo = lambda shape: pl.BlockSpec((None,) + shape, lambda j, elo, ehi, v, blk: (elo[j], 0, 0))
    hi = lambda shape: pl.BlockSpec((None,) + shape, lambda j, elo, ehi, v, blk: (ehi[j], 0, 0))
    return pl.pallas_call(
        _expert_kernel,
        grid_spec=pltpu.PrefetchScalarGridSpec(
            num_scalar_prefetch=4,
            grid=(n_tiles,),
            in_specs=[pl.BlockSpec((T * ROW_TILES, LANES), lambda j, elo, ehi, v, blk: (blk[j], 0)),
                      lo((D, F)), lo((D, F)), lo((F, D)), hi((D, F)), hi((D, F)), hi((F, D))],
            out_specs=pl.BlockSpec((T * 2 * ROW_TILES, LANES), lambda j, elo, ehi, v, blk: (j, 0))),
        out_shape=jax.ShapeDtypeStruct((n_tiles * T * 2 * ROW_TILES, LANES), F32),
        compiler_params=_params("arbitrary"),
        name="moe_experts",
    )(tmap[0], tmap[1], tmap[2], tmap[3], xs, wg, wu, wd, wg, wu, wd)


def _combine_kernel(pos_ref, ys_ref, w_ref, x1_ref, mod_ref, fg_ref, o_ref, buf, sem):
    i = pl.program_id(0)
    tm = x1_ref.shape[0]
    R2 = 2 * ROW_TILES

    def gather(step, slot):
        def issue(r, carry):
            p = pos_ref[step * tm + r]
            pltpu.make_async_copy(ys_ref.at[pl.ds(pl.multiple_of(p * R2, R2), R2)],
                                  buf.at[slot, pl.ds(pl.multiple_of(r * R2, R2), R2)],
                                  sem.at[slot]).start()
            return carry
        lax.fori_loop(0, tm, issue, 0)

    @pl.when(i == 0)
    def _():
        gather(0, 0)

    slot = i % 2

    @pl.when(i + 1 < pl.num_programs(0))
    def _():
        gather(i + 1, 1 - slot)

    pltpu.make_async_copy(ys_ref.at[pl.ds(0, tm * R2)], buf.at[slot], sem.at[slot]).wait()
    ylo = jnp.concatenate([buf[slot, pl.ds(k, tm, stride=R2), :] for k in range(ROW_TILES)], axis=1)
    yhi = jnp.concatenate([buf[slot, pl.ds(ROW_TILES + k, tm, stride=R2), :] for k in range(ROW_TILES)],
                          axis=1)
    y = w_ref[:, 0:1] * ylo + w_ref[:, 1:2] * yhi
    x2 = x1_ref[...] + mod_ref[5:6, :] * y
    r = lax.rsqrt(jnp.mean(x2 * x2, axis=-1, keepdims=True) + EPS)
    o_ref[...] = x2 * r * fg_ref[...]


def _combine(pos, ys, w, x1, mod, fg):
    N, D = x1.shape
    B = mod.shape[0]
    tm = COMBINE_ROWS
    per_b = (N // B) // tm
    return pl.pallas_call(
        _combine_kernel,
        grid_spec=pltpu.PrefetchScalarGridSpec(
            num_scalar_prefetch=1,
            grid=(N // tm,),
            in_specs=[pl.BlockSpec(memory_space=pl.ANY),
                      pl.BlockSpec((tm, 2), lambda i, p: (i, 0)),
                      pl.BlockSpec((tm, D), lambda i, p: (i, 0)),
                      pl.BlockSpec((None, 6, D), lambda i, p: (i // per_b, 0, 0)),
                      pl.BlockSpec((1, D), lambda i, p: (0, 0))],
            out_specs=pl.BlockSpec((tm, D), lambda i, p: (i, 0)),
            scratch_shapes=[pltpu.VMEM((2, tm * 2 * ROW_TILES, LANES), F32),
                            pltpu.SemaphoreType.DMA((2,))]),
        out_shape=jax.ShapeDtypeStruct((N, D), F32),
        compiler_params=_params("arbitrary"),
        name="moe_combine",
    )(pos, ys, w, x1, mod, fg)


def _dup_groups(w):
    a, b = w[:, :HEAD_DIM], w[:, HEAD_DIM:]
    return jnp.concatenate([a, a, b, b], axis=1)


def _arrange_w_in(w):
    cuts = np.cumsum([0, POOL_WIDTH, ATTN_WIDTH] + [KV_WIDTH] * 6 + [N_BRANCH * N_HEADS, 2 * D_MODEL])
    pieces = [w[:, int(cuts[i]):int(cuts[i + 1])] for i in range(len(cuts) - 1)]
    pool, q, kc, vc, ks, vs, kw, vw, bg, mg = pieces
    bg = jnp.pad(bg, ((0, 0), (0, LANES - bg.shape[1])))
    out = jnp.concatenate([pool, q, kc, vc, _dup_groups(ks), _dup_groups(vs),
                           _dup_groups(kw), _dup_groups(vw), bg, mg], axis=1)
    return out.astype(BF)


def _compress_weights(pos, w1, b1, w2):
    eye = jnp.eye(N_KV_GROUPS, dtype=F32)
    halves = CMP_BLOCK // CMP_STRIDE
    w1r = w1.reshape(halves, CMP_STRIDE, HEAD_DIM, CMP_HIDDEN)
    w1big = jnp.einsum('hidc,gk->igdkhc', w1r, eye).reshape(
        CMP_STRIDE * N_KV_GROUPS * HEAD_DIM, N_KV_GROUPS * halves * CMP_HIDDEN)
    w2big = jnp.einsum('cd,gk,r->gckrd', w2, eye, jnp.ones((2,), F32)).reshape(
        N_KV_GROUPS * CMP_HIDDEN, N_KV_GROUPS * 2 * HEAD_DIM)
    pos8 = jnp.broadcast_to(pos.reshape(1, CMP_BLOCK * HEAD_DIM), (8, CMP_BLOCK * HEAD_DIM))
    return (w1big.astype(BF), pos8.astype(BF), w1.astype(BF), b1.reshape(1, CMP_HIDDEN),
            w2big.astype(BF))


def _selection_tables(S):
    n_chunks = S // CMP_STRIDE
    n_cmp = n_chunks - CMP_BLOCK // CMP_STRIDE + 1
    n_blk = S // SEL_BLOCK
    s1 = np.arange(n_cmp)[:, None] * CMP_STRIDE
    s2 = np.arange(n_blk)[None, :] * SEL_BLOCK
    ovl = np.clip(np.minimum(s1 + CMP_BLOCK, s2 + SEL_BLOCK) - np.maximum(s1, s2), 0, None) / CMP_BLOCK
    ov = np.zeros((n_chunks, LANES), np.float32)
    ov[:n_cmp, :n_blk] = ovl
    n_tiles = S // SEL_KEY_TILE
    ex = np.zeros((n_tiles, LANES, SEL_KEY_TILE), np.float32)
    for kt in range(n_tiles):
        blk = (kt * SEL_KEY_TILE + np.arange(SEL_KEY_TILE)) // SEL_BLOCK
        ex[kt, blk, np.arange(SEL_KEY_TILE)] = 1.0
    return jnp.asarray(ov, BF), jnp.asarray(ex, BF)


def kernel(x, c, ada_w, ada_b, norm1_g, w_in, pool_w, pool_scale, cmp_pos, cmp_w1, cmp_b1, cmp_w2,
           w_up_pool, w_up_attn, w_out, norm2_g, router_g_w, router_g_b, router_e_w, router_e_b,
           exp_w_gate, exp_w_up, exp_w_down, final_g):
    B, S, D = x.shape
    N = B * S
    assert ada_w.shape[0] == 1, "the final norm is fused into the last layer's combine step"
    for l in range(ada_w.shape[0]):
        mod = _ada(c, ada_w[l], ada_b[l]).reshape(B, 6, D)
        (up, q, kc, vc, ks, vs, kw, vw, bg, gm) = _inproj(
            x, mod, norm1_g[l].reshape(1, D), _arrange_w_in(w_in[l]))
        n_chunks = S // CMP_STRIDE
        kcc = _compress(kc.reshape(B, n_chunks, CMP_STRIDE * KV_WIDTH),
                        *_compress_weights(cmp_pos[l, 0], cmp_w1[l, 0], cmp_b1[l, 0], cmp_w2[l, 0]))
        vcc = _compress(vc.reshape(B, n_chunks, CMP_STRIDE * KV_WIDTH),
                        *_compress_weights(cmp_pos[l, 1], cmp_w1[l, 1], cmp_b1[l, 1], cmp_w2[l, 1]))
        ov, ex = _selection_tables(S)
        o = _attention(q, kcc, vcc, ks, vs, kw, vw, bg, ov, ex)
        wr = jnp.zeros((8 + N_EXPERTS, D), F32)
        wr = wr.at[0:N_GROUPS].set(router_g_w[l].T).at[8:].set(router_e_w[l].T).astype(BF)
        br = jnp.zeros((8 + N_EXPERTS, 1), F32)
        br = br.at[0:N_GROUPS, 0].set(router_g_b[l]).at[8:, 0].set(router_e_b[l])
        x1, h2t, crk, rw, cnt = _mixer_out(
            up, o, gm, x, mod, pool_w[l].astype(BF), pool_scale[l].reshape(1, POOL_WIDTH),
            w_up_pool[l].astype(BF), w_up_attn[l].astype(BF), w_out[l].astype(BF),
            norm2_g[l].reshape(1, D), wr, br)
        n_tiles = N // EXPERT_TILE + N_CLASS
        n_tiles_pad = -(-n_tiles // LANES) * LANES
        pos, tmap = _plan(crk, cnt, n_tiles_pad)
        pos = pos.reshape(N)
        xs = _dispatch(pos, h2t, n_tiles * EXPERT_TILE)
        ys = _experts(tmap, xs, exp_w_gate[l].astype(BF), exp_w_up[l].astype(BF),
                      exp_w_down[l].astype(BF), n_tiles)
        w = jnp.stack([rw[:, 0, :].reshape(N), rw[:, 1, :].reshape(N)], axis=1)
        y = _combine(pos, ys, w, x1.reshape(N, D), mod, final_g.reshape(1, D))
        x = y.reshape(B, S, D)
    return x
```

```python
import functools

import numpy as np
import jax
import jax.numpy as jnp
from jax import lax
from jax.experimental import pallas as pl
from jax.experimental.pallas import tpu as pltpu

BF = jnp.bfloat16
F32 = jnp.float32
I32 = jnp.int32

D_MODEL = 1024
POOL_WIDTH = 512
POOL_WINDOWS = (2, 4, 8, 16)
POOL_GROUP = 128
POOL_HALO = 16
N_HEADS = 8
HEAD_DIM = 64
N_KV_GROUPS = 2
HEADS_PER_GROUP = 4
ATTN_WIDTH = 512
KV_WIDTH = 128
CMP_BLOCK = 32
CMP_STRIDE = 16
CMP_HIDDEN = 256
SEL_BLOCK = 64
N_SELECT = 8
WINDOW = 512
Q_CHUNK = 64
N_BRANCH = 3
N_GROUPS = 4
EXPERTS_PER_GROUP = 8
N_EXPERTS = 32
EXPERT_FF = 512
EPS = 1e-6
NEG = -1e30
FORCE_SCORE = 1e4
QK_SCALE = HEAD_DIM ** -0.5

PAIRS_PER_GROUP = EXPERTS_PER_GROUP * (EXPERTS_PER_GROUP - 1) // 2
N_CLASS = N_GROUPS * PAIRS_PER_GROUP
N_CLASS_PAD = 128
ROUTE_TILE = 256
EXPERT_TILE_LOG2 = 8
EXPERT_TILE = 1 << EXPERT_TILE_LOG2
DISPATCH_ROWS = 1024
COMBINE_ROWS = 256

LANES = 128
ROW_TILES = D_MODEL // LANES
SEL_KEY_TILE = 512
VMEM_LIMIT = 56 * 1024 * 1024

C_POOL = 0
C_Q = C_POOL + POOL_WIDTH
C_KC = C_Q + N_HEADS * LANES
C_VC = C_KC + KV_WIDTH
C_KS = C_VC + KV_WIDTH
C_VS = C_KS + 2 * KV_WIDTH
C_KW = C_VS + 2 * KV_WIDTH
C_VW = C_KW + 2 * KV_WIDTH
C_BG = C_VW + 2 * KV_WIDTH
C_MG = C_BG + LANES
C_END = C_MG + 2 * D_MODEL


def _dot(a, b):
    return jnp.dot(a, b, preferred_element_type=F32)


def _dot_nt(a, b):
    return lax.dot_general(a, b, (((1,), (1,)), ((), ())), preferred_element_type=F32)


def _params(*sem):
    return pltpu.CompilerParams(dimension_semantics=sem, vmem_limit_bytes=VMEM_LIMIT)


def _ada_kernel(c_ref, w_ref, b_ref, o_ref):
    o_ref[...] = _dot(c_ref[...].astype(BF), w_ref[...].astype(BF)) + b_ref[...]


def _ada(c, w, b):
    B, D = c.shape
    n = w.shape[1]
    tn = 1024
    return pl.pallas_call(
        _ada_kernel,
        grid=(n // tn,),
        in_specs=[pl.BlockSpec((B, D), lambda j: (0, 0)),
                  pl.BlockSpec((D, tn), lambda j: (0, j)),
                  pl.BlockSpec((1, tn), lambda j: (0, j))],
        out_specs=pl.BlockSpec((B, tn), lambda j: (0, j)),
        out_shape=jax.ShapeDtypeStruct((B, n), F32),
        compiler_params=_params("arbitrary"),
        name="ada_mod",
    )(c, w, b.reshape(1, n))


def _inproj_kernel(x_ref, mod_ref, g_ref, w_ref, blk_ref, up_ref, q_ref, kc_ref, vc_ref,
                   ks_ref, vs_ref, kw_ref, vw_ref, bg_ref, gm_ref):
    x = x_ref[...]
    r = lax.rsqrt(jnp.mean(x * x, axis=-1, keepdims=True) + EPS)
    h = x * r * g_ref[...] * (1.0 + mod_ref[1:2, :]) + mod_ref[0:1, :]
    hb = h.astype(BF)

    def proj(a, b):
        return _dot(hb, w_ref[:, a:b])

    up_ref[...] = proj(C_POOL, C_Q)
    q_ref[...] = (proj(C_Q, C_KC) * QK_SCALE).astype(BF)
    kc_ref[...] = proj(C_KC, C_VC).astype(BF)
    vc_ref[...] = proj(C_VC, C_KS).astype(BF)
    ks_ref[...] = (proj(C_KS, C_VS) + blk_ref[...]).astype(BF)
    vs_ref[...] = proj(C_VS, C_KW).astype(BF)
    kw_ref[...] = proj(C_KW, C_VW).astype(BF)
    vw_ref[...] = proj(C_VW, C_BG).astype(BF)
    bg_ref[...] = proj(C_BG, C_MG)
    gm_ref[...] = jax.nn.sigmoid(proj(C_MG, C_END)).astype(BF)


def _inproj(x, mod, g, w):
    B, S, D = x.shape
    tm = 512
    blk = np.zeros((S, 2 * LANES), np.float32)
    for gg in range(N_KV_GROUPS):
        blk[np.arange(S), gg * LANES + HEAD_DIM + np.arange(S) // SEL_BLOCK] = 1.0
    blk = jnp.asarray(blk)
    widths = [(POOL_WIDTH, F32), (N_HEADS * LANES, BF), (KV_WIDTH, BF), (KV_WIDTH, BF),
              (2 * KV_WIDTH, BF), (2 * KV_WIDTH, BF), (2 * KV_WIDTH, BF), (2 * KV_WIDTH, BF),
              (LANES, F32), (2 * D_MODEL, BF)]
    row = lambda n: pl.BlockSpec((None, tm, n), lambda b, i: (b, i, 0))
    return pl.pallas_call(
        _inproj_kernel,
        grid=(B, S // tm),
        in_specs=[row(D),
                  pl.BlockSpec((None, 6, D), lambda b, i: (b, 0, 0)),
                  pl.BlockSpec((1, D), lambda b, i: (0, 0)),
                  pl.BlockSpec((D, C_END), lambda b, i: (0, 0)),
                  pl.BlockSpec((tm, 2 * LANES), lambda b, i: (i, 0))],
        out_specs=[row(n) for n, _ in widths],
        out_shape=[jax.ShapeDtypeStruct((B, S, n), dt) for n, dt in widths],
        compiler_params=_params("arbitrary", "arbitrary"),
        name="norm1_inproj",
    )(x, mod, g, w, blk)


def _gelu_tanh(x):
    return 0.5 * x * (1.0 + jnp.tanh(0.7978845608028654 * (x + 0.044715 * x * x * x)))


def _compress_kernel(x_ref, w1b_ref, pos_ref, w1_ref, b1_ref, w2b_ref, o_ref):
    y = _dot(x_ref[...], w1b_ref[...])
    posc = _dot(pos_ref[...], w1_ref[...])[0:1, :] + b1_ref[...]
    n = y.shape[0]
    acts = []
    for g in range(N_KV_GROUPS):
        first = y[:, g * 2 * CMP_HIDDEN: g * 2 * CMP_HIDDEN + CMP_HIDDEN]
        second = y[:, g * 2 * CMP_HIDDEN + CMP_HIDDEN: (g + 1) * 2 * CMP_HIDDEN]
        pre = first + pltpu.roll(second, n - 1, 0) + posc
        acts.append(_gelu_tanh(pre).astype(BF))
    act = jnp.concatenate(acts, axis=1)
    o_ref[...] = _dot(act, w2b_ref[...]).astype(BF)


def _compress(xk, w1big, pos8, w1, b1, w2big):
    B, n, width = xk.shape
    full = lambda a: pl.BlockSpec(a.shape, lambda b: (0,) * a.ndim)
    return pl.pallas_call(
        _compress_kernel,
        grid=(B,),
        in_specs=[pl.BlockSpec((None, n, width), lambda b: (b, 0, 0)),
                  full(w1big), full(pos8), full(w1), full(b1), full(w2big)],
        out_specs=pl.BlockSpec((None, n, 2 * KV_WIDTH), lambda b: (b, 0, 0)),
        out_shape=jax.ShapeDtypeStruct((B, n, 2 * KV_WIDTH), BF),
        compiler_params=_params("arbitrary"),
        name="compress",
    )(xk, w1big, pos8, w1, b1, w2big)


def _masked_exp(s, mask):
    sm = jnp.where(mask, s, NEG)
    m = jnp.max(sm, axis=-1, keepdims=True)
    p = jnp.where(mask, jnp.exp(sm - m), 0.0)
    return p, jnp.sum(p, axis=-1, keepdims=True)


def _safe_inv(l):
    return jnp.where(l > 0.0, 1.0 / jnp.where(l > 0.0, l, 1.0), 0.0)


def _softmax_tile(s, m_old):
    m_new = jnp.maximum(m_old, jnp.max(s, axis=-1, keepdims=True))
    p = jnp.exp(s - m_new)
    return m_new, p, jnp.sum(p, axis=-1, keepdims=True)


def _attn_kernel(q_ref, kc_ref, vc_ref, ks_ref, vs_ref, kw_ref, vw_ref, bg_ref, ovt_ref, o_ref):
    ci = pl.program_id(1)
    q0 = ci * Q_CHUNK
    Q, H, G = Q_CHUNK, HEADS_PER_GROUP, N_KV_GROUPS
    R = H * Q
    n_blk = ovt_ref.shape[0]
    sig = jax.nn.sigmoid(bg_ref[...])
    t_q = q0 + lax.broadcasted_iota(I32, (Q, 1), 0)
    t_r = jnp.concatenate([t_q] * H, axis=0)

    def rows4(a):
        return jnp.concatenate([a] * H, axis=0)

    def q_rows(g):
        return jnp.concatenate(
            [q_ref[:, (g * H + h) * LANES:(g * H + h + 1) * LANES] for h in range(H)], axis=0)

    gcs = [slice(g * LANES, (g + 1) * LANES) for g in range(G)]
    qp = [q_rows(g) for g in range(G)]

    w0 = pl.multiple_of(jnp.maximum(q0 - WINDOW, 0), Q_CHUNK)
    s3 = [_dot_nt(qp[g], kw_ref[pl.ds(w0, WINDOW + Q_CHUNK), gcs[g]]) for g in range(G)]
    s1 = [_dot_nt(qp[g], kc_ref[:, gcs[g]]) for g in range(G)]

    n_idx = lax.broadcasted_iota(I32, s1[0].shape, 1)
    m1 = (n_idx * CMP_STRIDE + (CMP_BLOCK - 1)) <= t_r
    o1, inv1, psums = [], [], []
    for g in range(G):
        p1, l1 = _masked_exp(s1[g], m1)
        iv = _safe_inv(l1)
        o1.append(_dot(p1.astype(BF), vc_ref[:, gcs[g]]))
        inv1.append(iv)
        p1n = p1 * iv
        psum = p1n[0:Q]
        for h in range(1, H):
            psum = psum + p1n[h * Q:(h + 1) * Q]
        psums.append(psum)
    psum = jnp.concatenate(psums, axis=0)
    hi = psum.astype(BF)
    lo = (psum - hi.astype(F32)).astype(BF)
    ps_t = _dot_nt(ovt_ref[...], hi) + _dot_nt(ovt_ref[...], lo)

    kpos3 = w0 + lax.broadcasted_iota(I32, (Q, WINDOW + Q_CHUNK), 1)
    bias3 = rows4(jnp.where((kpos3 <= t_q) & (kpos3 > t_q - WINDOW), 0.0, NEG))
    win = []
    for g in range(G):
        _, p3, l3 = _softmax_tile(s3[g] + bias3, jnp.full((R, 1), NEG, F32))
        win.append((_dot(p3.astype(BF), vw_ref[pl.ds(w0, WINDOW + Q_CHUNK), gcs[g]]), l3))

    early, gate_sel = [], []
    for g in range(G):
        o3, l3 = win[g]
        inv3 = 1.0 / l3
        parts, gates = [], []
        for h in range(H):
            rows = slice(h * Q, (h + 1) * Q)
            col = g * H + h
            g1 = sig[:, col:col + 1]
            g3 = sig[:, 2 * N_HEADS + col:2 * N_HEADS + col + 1]
            parts.append((g1 * inv1[g][rows]) * o1[g][rows] + (g3 * inv3[rows]) * o3[rows])
            gates.append(jnp.broadcast_to(sig[:, N_HEADS + col:N_HEADS + col + 1], (Q, LANES)))
        early.append(parts)
        gate_sel.append(gates)

    j = lax.broadcasted_iota(I32, ps_t.shape, 0)
    forced = (j == 0) | (j == ci) | (j == ci - 1)
    score = jnp.where(forced, FORCE_SCORE, jnp.where(j <= ci, ps_t, NEG))
    rank = jnp.zeros(ps_t.shape, I32)
    for jp in range(n_blk):
        c = score[jp:jp + 1, :]
        beats = (c > score) | ((c == score) & (j > jp))
        rank = rank + beats.astype(I32)
    bias_t = jnp.where(rank < N_SELECT, 0.0, NEG)
    pad_t = jnp.concatenate([jnp.zeros((HEAD_DIM, G * Q), F32), bias_t,
                             jnp.zeros((LANES - HEAD_DIM - n_blk, G * Q), F32)], axis=0)
    sel_bias = pad_t.T.astype(BF)

    qa = [qp[g] + rows4(sel_bias[g * Q:(g + 1) * Q]) for g in range(G)]

    def sel_tile(k0, carry, causal):
        s = [_dot_nt(qa[g], ks_ref[pl.ds(k0, SEL_KEY_TILE), gcs[g]]) for g in range(G)]
        if causal:
            kpos = k0 + lax.broadcasted_iota(I32, (Q, SEL_KEY_TILE), 1)
            bias = rows4(jnp.where(kpos <= t_q, 0.0, NEG))
            s = [sg + bias for sg in s]
        out = []
        for g in range(G):
            m, l, acc = carry[g]
            m_new, p, psum_ = _softmax_tile(s[g], m)
            alpha = jnp.exp(m - m_new)
            acc = alpha * acc + _dot(p.astype(BF), vs_ref[pl.ds(k0, SEL_KEY_TILE), gcs[g]])
            out.append((m_new, alpha * l + psum_, acc))
        return tuple(out)

    def sweep(n_tiles):
        def run():
            carry = tuple((jnp.full((R, 1), NEG, F32), jnp.zeros((R, 1), F32),
                           jnp.zeros((R, LANES), F32)) for _ in range(G))
            for kt in range(n_tiles - 1):
                carry = sel_tile(kt * SEL_KEY_TILE, carry, False)
            return sel_tile((n_tiles - 1) * SEL_KEY_TILE, carry, True)
        return run

    blocks_per_tile = SEL_KEY_TILE // SEL_BLOCK
    max_tiles = n_blk // blocks_per_tile
    carry = lax.switch(ci // blocks_per_tile, [sweep(n) for n in range(1, max_tiles + 1)])

    lo_half = lax.broadcasted_iota(I32, (Q, LANES), 1) < HEAD_DIM
    for g in range(G):
        _, l2, o2 = carry[g]
        inv2 = 1.0 / l2
        heads = []
        for h in range(H):
            rows = slice(h * Q, (h + 1) * Q)
            heads.append(early[g][h] + (gate_sel[g][h] * inv2[rows]) * o2[rows])
        for k in range(H // 2):
            slab = jnp.where(lo_half, heads[2 * k], heads[2 * k + 1])
            c0 = (g * (H // 2) + k) * LANES
            o_ref[:, c0:c0 + LANES] = slab.astype(BF)


def _attention(q, kc, vc, ks, vs, kw, vw, bg, ovt):
    B, S, _ = q.shape
    nq = S // Q_CHUNK
    per_b = lambda a: pl.BlockSpec((None,) + a.shape[1:], lambda b, i: (b, 0, 0))
    full = lambda a: pl.BlockSpec(a.shape, lambda b, i: (0,) * a.ndim)
    return pl.pallas_call(
        _attn_kernel,
        grid=(B, nq),
        in_specs=[pl.BlockSpec((None, Q_CHUNK, N_HEADS * LANES), lambda b, i: (b, i, 0)),
                  per_b(kc), per_b(vc), per_b(ks), per_b(vs), per_b(kw), per_b(vw),
                  pl.BlockSpec((None, Q_CHUNK, LANES), lambda b, i: (b, i, 0)),
                  full(ovt)],
        out_specs=pl.BlockSpec((None, Q_CHUNK, ATTN_WIDTH), lambda b, i: (b, i, 0)),
        out_shape=jax.ShapeDtypeStruct((B, S, ATTN_WIDTH), BF),
        compiler_params=_params("arbitrary", "arbitrary"),
        name="nsa_attention",
    )(q, kc, vc, ks, vs, kw, vw, bg, ovt)


def _mixer_out_kernel(upc_ref, upp_ref, o_ref, gm_ref, x_ref, mod_ref, pw_ref, psc_ref,
                      wup_ref, wua_ref, wo_ref, g2_ref, wr_ref, br_ref, tri_ref,
                      x1_ref, h2t_ref, crk_ref, rw_ref, cnt_ref, cnt_scr):
    i = pl.program_id(1)

    @pl.when((pl.program_id(0) == 0) & (i == 0))
    def _():
        cnt_scr[...] = jnp.zeros_like(cnt_scr)

    tm = upc_ref.shape[0]
    prev = upp_ref[...] * (i > 0).astype(F32)
    ext = jnp.concatenate([prev, upc_ref[...]], axis=0)
    t = i * tm + lax.broadcasted_iota(I32, (tm, 1), 0)
    ys = []
    for gi, w in enumerate(POOL_WINDOWS):
        u = ext[:, gi * POOL_GROUP:(gi + 1) * POOL_GROUP]
        acc = u
        shift = 1
        while shift < w:
            acc = acc + pltpu.roll(acc, shift, 0)
            shift *= 2
        inv_cnt = 1.0 / jnp.minimum(t + 1, w).astype(F32)
        p = acc[POOL_HALO:] * inv_cnt - u[POOL_HALO:]
        ys.append(_dot(p.astype(BF), pw_ref[gi]))
    y = jnp.concatenate(ys, axis=1) * psc_ref[...]
    y_pool = _dot(y.astype(BF), wup_ref[...])
    y_attn = _dot(o_ref[...], wua_ref[...])
    gm = gm_ref[...].astype(F32)
    mix = gm[:, :D_MODEL] * y_pool + gm[:, D_MODEL:] * y_attn
    x1 = x_ref[...] + mod_ref[2:3, :] * _dot(mix.astype(BF), wo_ref[...])
    x1_ref[...] = x1
    r = lax.rsqrt(jnp.mean(x1 * x1, axis=-1, keepdims=True) + EPS)
    h2f = x1 * r * g2_ref[...] * (1.0 + mod_ref[4:5, :]) + mod_ref[3:4, :]
    h2 = h2f.astype(BF)
    for k in range(ROW_TILES):
        h2t_ref[pl.ds(k, tm, stride=ROW_TILES), :] = h2f[:, k * LANES:(k + 1) * LANES]

    lt = _dot_nt(wr_ref[...], h2) + br_ref[...]
    lg = lt[0:N_GROUPS]
    gmax = jnp.max(lg, axis=0, keepdims=True)
    gi_ = lax.broadcasted_iota(I32, lg.shape, 0)
    gidx = jnp.min(jnp.where(lg == gmax, gi_, N_GROUPS), axis=0, keepdims=True)
    gp = 1.0 / jnp.sum(jnp.exp(lg - gmax), axis=0, keepdims=True)
    E = EXPERTS_PER_GROUP
    le = jnp.zeros((E, tm), F32)
    for gg in range(N_GROUPS):
        le = jnp.where(gidx == gg, lt[8 + gg * E:8 + (gg + 1) * E], le)
    ei = lax.broadcasted_iota(I32, le.shape, 0)
    v1 = jnp.max(le, axis=0, keepdims=True)
    i1 = jnp.min(jnp.where(le == v1, ei, E), axis=0, keepdims=True)
    rest = jnp.where(ei == i1, -jnp.inf, le)
    v2 = jnp.max(rest, axis=0, keepdims=True)
    i2 = jnp.min(jnp.where(rest == v2, ei, E), axis=0, keepdims=True)
    e = jnp.exp(v2 - v1)
    wa = gp / (1.0 + e)
    wb = gp * e / (1.0 + e)
    lo = jnp.minimum(i1, i2)
    hi = jnp.maximum(i1, i2)
    pair = lax.shift_right_logical(lo * (2 * E - 1 - lo), 1) + hi - lo - 1
    cls = gidx * PAIRS_PER_GROUP + pair
    first_lo = i1 < i2
    w_lo = jnp.where(first_lo, wa, wb)
    w_hi = jnp.where(first_lo, wb, wa)
    oh = lax.broadcasted_iota(I32, (N_CLASS_PAD, tm), 0) == cls
    before = _dot(oh.astype(BF), tri_ref[...]) + cnt_scr[:, 0:1]
    rank = jnp.sum(jnp.where(oh, before, 0.0), axis=0, keepdims=True).astype(I32)
    cnt_scr[...] = cnt_scr[...] + jnp.sum(oh.astype(F32), axis=1, keepdims=True)
    cnt_ref[...] = cnt_scr[...]
    row = lax.broadcasted_iota(I32, (8, tm), 0)
    crk_ref[...] = jnp.where(row == 0, cls, jnp.where(row == 1, rank, 0))
    rw_ref[...] = jnp.where(row == 0, w_lo, jnp.where(row == 1, w_hi, 0.0))


def _mixer_out(up, o, gm, x, mod, pw, psc, wup, wua, wo, g2, wr, br):
    B, S, D = x.shape
    tm = ROUTE_TILE
    nt = S // tm
    tri = jnp.asarray(np.triu(np.ones((tm, tm), np.float32), k=1), BF)
    row = lambda n: pl.BlockSpec((None, tm, n), lambda b, i: (b, i, 0))
    full = lambda a: pl.BlockSpec(a.shape, lambda b, i: (0,) * a.ndim)
    per = tm // POOL_HALO
    return pl.pallas_call(
        _mixer_out_kernel,
        grid=(B, nt),
        in_specs=[row(POOL_WIDTH),
                  pl.BlockSpec((None, POOL_HALO, POOL_WIDTH),
                               lambda b, i: (b, jnp.maximum(i * per - 1, 0), 0)),
                  row(ATTN_WIDTH), row(2 * D_MODEL), row(D),
                  pl.BlockSpec((None, 6, D), lambda b, i: (b, 0, 0)),
                  full(pw), full(psc), full(wup), full(wua), full(wo), full(g2),
                  full(wr), full(br), full(tri)],
        out_specs=[row(D),
                   pl.BlockSpec((tm * ROW_TILES, LANES), lambda b, i: (b * nt + i, 0)),
                   pl.BlockSpec((None, 8, tm), lambda b, i: (b * nt + i, 0, 0)),
                   pl.BlockSpec((None, 8, tm), lambda b, i: (b * nt + i, 0, 0)),
                   pl.BlockSpec((N_CLASS_PAD, LANES), lambda b, i: (0, 0))],
        out_shape=[jax.ShapeDtypeStruct((B, S, D), F32),
                   jax.ShapeDtypeStruct((B * S * ROW_TILES, LANES), F32),
                   jax.ShapeDtypeStruct((B * nt, 8, tm), I32),
                   jax.ShapeDtypeStruct((B * nt, 8, tm), F32),
                   jax.ShapeDtypeStruct((N_CLASS_PAD, LANES), F32)],
        scratch_shapes=[pltpu.VMEM((N_CLASS_PAD, LANES), F32)],
        compiler_params=_params("arbitrary", "arbitrary"),
        name="mixer_out_router",
    )(up, up, o, gm, x, mod, pw, psc, wup, wua, wo, g2, wr, br, tri)


def _plan_kernel(crk_ref, cnt_ref, etab_ref, pos_ref, tmap_ref):
    C = N_CLASS_PAD
    cnt = cnt_ref[:, 0:1].astype(I32)
    ntile = lax.shift_right_logical(cnt + (EXPERT_TILE - 1), EXPERT_TILE_LOG2)
    ntile_f = ntile.astype(F32)
    r = lax.broadcasted_iota(I32, (C, C), 0)
    c = lax.broadcasted_iota(I32, (C, C), 1)
    lower = (c < r).astype(BF)
    first = _dot(lower, jnp.broadcast_to(ntile_f, (C, LANES)).astype(BF))[:, 0:1]
    last = first + ntile_f
    total = jnp.sum(ntile_f, axis=0, keepdims=True)
    off = (first * EXPERT_TILE).astype(I32)

    def body(i, carry):
        cls = crk_ref[i, 0:1, :]
        rank = crk_ref[i, 1:2, :]
        oh = lax.broadcasted_iota(I32, (C, cls.shape[1]), 0) == cls
        pos_ref[pl.ds(i, 1), :] = jnp.sum(jnp.where(oh, off, 0), axis=0, keepdims=True) + rank
        return carry

    lax.fori_loop(0, crk_ref.shape[0], body, 0)

    nj = tmap_ref.shape[1]
    j = lax.broadcasted_iota(I32, (1, nj), 1).astype(F32)
    jj = jnp.minimum(j, total - 1.0)
    tcls = jnp.sum((last <= jj).astype(I32), axis=0, keepdims=True)
    oh2 = lax.broadcasted_iota(I32, (C, nj), 0) == tcls
    elo = jnp.sum(jnp.where(oh2, etab_ref[:, 0:1], 0), axis=0, keepdims=True)
    ehi = jnp.sum(jnp.where(oh2, etab_ref[:, 1:2], 0), axis=0, keepdims=True)
    row = lax.broadcasted_iota(I32, (8, nj), 0)
    tmap_ref[...] = jnp.where(
        row == 0, elo, jnp.where(row == 1, ehi, jnp.where(
            row == 2, (j < total).astype(I32), jnp.where(row == 3, jj.astype(I32), 0))))


def _plan(crk, cnt, n_tiles_pad):
    nt, _, tm = crk.shape
    etab = np.zeros((N_CLASS_PAD, LANES), np.int32)
    cid = 0
    for g in range(N_GROUPS):
        for lo in range(EXPERTS_PER_GROUP):
            for hi in range(lo + 1, EXPERTS_PER_GROUP):
                etab[cid, 0] = g * EXPERTS_PER_GROUP + lo
                etab[cid, 1] = g * EXPERTS_PER_GROUP + hi
                cid += 1
    etab = jnp.asarray(etab)
    full = lambda a: pl.BlockSpec(a.shape, lambda i: (0,) * a.ndim)
    return pl.pallas_call(
        _plan_kernel,
        grid=(1,),
        in_specs=[full(crk), full(cnt), full(etab)],
        out_specs=[pl.BlockSpec((nt, tm), lambda i: (0, 0)),
                   pl.BlockSpec((8, n_tiles_pad), lambda i: (0, 0))],
        out_shape=[jax.ShapeDtypeStruct((nt, tm), I32),
                   jax.ShapeDtypeStruct((8, n_tiles_pad), I32)],
        compiler_params=_params("arbitrary"),
        name="moe_plan",
    )(crk, cnt, etab)


def _dispatch_kernel(pos_ref, h_ref, xs_in_ref, xs_ref, sem):
    del xs_in_ref
    i = pl.program_id(0)
    rows = h_ref.shape[0] // ROW_TILES

    def issue(r, carry):
        p = pos_ref[i * rows + r]
        pltpu.make_async_copy(h_ref.at[pl.ds(pl.multiple_of(r * ROW_TILES, ROW_TILES), ROW_TILES)],
                              xs_ref.at[pl.ds(pl.multiple_of(p * ROW_TILES, ROW_TILES), ROW_TILES)],
                              sem).start()
        return carry

    lax.fori_loop(0, rows, issue, 0)
    pltpu.make_async_copy(h_ref, xs_ref.at[pl.ds(0, rows * ROW_TILES)], sem).wait()


def _dispatch(pos, h2t, n_sorted_rows):
    rows = DISPATCH_ROWS
    n = h2t.shape[0] // ROW_TILES
    zeros = jnp.zeros((n_sorted_rows * ROW_TILES, LANES), F32)
    return pl.pallas_call(
        _dispatch_kernel,
        grid_spec=pltpu.PrefetchScalarGridSpec(
            num_scalar_prefetch=1,
            grid=(n // rows,),
            in_specs=[pl.BlockSpec((rows * ROW_TILES, LANES), lambda i, p: (i, 0)),
                      pl.BlockSpec(memory_space=pl.ANY)],
            out_specs=pl.BlockSpec(memory_space=pl.ANY),
            scratch_shapes=[pltpu.SemaphoreType.DMA]),
        out_shape=jax.ShapeDtypeStruct(zeros.shape, F32),
        input_output_aliases={2: 0},
        compiler_params=_params("arbitrary"),
        name="moe_dispatch",
    )(pos, h2t, zeros)


def _expert_kernel(elo_ref, ehi_ref, valid_ref, blk_ref, x_ref, wg0, wu0, wd0, wg1, wu1, wd1, o_ref):
    del elo_ref, ehi_ref, blk_ref
    T = EXPERT_TILE
    valid = valid_ref[pl.program_id(0)] > 0

    @pl.when(jnp.logical_not(valid))
    def _():
        o_ref[...] = jnp.zeros_like(o_ref)

    @pl.when(valid)
    def _():
        x = jnp.concatenate([x_ref[pl.ds(k, T, stride=ROW_TILES), :] for k in range(ROW_TILES)],
                            axis=1).astype(BF)
        for half, (wg, wu, wd) in enumerate(((wg0, wu0, wd0), (wg1, wu1, wd1))):
            a = _dot(x, wg[...])
            b = _dot(x, wu[...])
            he = (a * jax.nn.sigmoid(a)) * b
            y = _dot(he.astype(BF), wd[...])
            for k in range(ROW_TILES):
                o_ref[pl.ds(half * ROW_TILES + k, T, stride=2 * ROW_TILES), :] = (
                    y[:, k * LANES:(k + 1) * LANES])


def _experts(tmap, xs, wg, wu, wd, n_tiles):
    T = EXPERT_TILE
    D, F = D_MODEL, EXPERT_FF
    lo = lambda shape: pl.BlockSpec((None,) + shape, lambda j, elo, ehi, v, blk: (elo[j], 0, 0))
    hi = lambda shape: pl.BlockSpec((None,) + shape, lambda j, elo, ehi, v, blk: (ehi[j], 0, 0))
    return pl.pallas_call(
        _expert_kernel,
        grid_spec=pltpu.PrefetchScalarGridSpec(
            num_scalar_prefetch=4,
            grid=(n_tiles,),
            in_specs=[pl.BlockSpec((T * ROW_TILES, LANES), lambda j, elo, ehi, v, blk: (blk[j], 0)),
                      lo((D, F)), lo((D, F)), lo((F, D)), hi((D, F)), hi((D, F)), hi((F, D))],
            out_specs=pl.BlockSpec((T * 2 * ROW_TILES, LANES), lambda j, elo, ehi, v, blk: (j, 0))),
        out_shape=jax.ShapeDtypeStruct((n_tiles * T * 2 * ROW_TILES, LANES), F32),
        compiler_params=_params("arbitrary"),
        name="moe_experts",
    )(tmap[0], tmap[1], tmap[2], tmap[3], xs, wg, wu, wd, wg, wu, wd)


def _combine_kernel(pos_ref, ys_ref, w_ref, x1_ref, mod_ref, fg_ref, o_ref, buf, sem):
    i = pl.program_id(0)
    tm = x1_ref.shape[0]
    R2 = 2 * ROW_TILES

    def gather(step, slot):
        def issue(r, carry):
            p = pos_ref[step * tm + r]
            pltpu.make_async_copy(ys_ref.at[pl.ds(pl.multiple_of(p * R2, R2), R2)],
                                  buf.at[slot, pl.ds(pl.multiple_of(r * R2, R2), R2)],
                                  sem.at[slot]).start()
            return carry
        lax.fori_loop(0, tm, issue, 0)

    @pl.when(i == 0)
    def _():
        gather(0, 0)

    slot = i % 2

    @pl.when(i + 1 < pl.num_programs(0))
    def _():
        gather(i + 1, 1 - slot)

    pltpu.make_async_copy(ys_ref.at[pl.ds(0, tm * R2)], buf.at[slot], sem.at[slot]).wait()
    ylo = jnp.concatenate([buf[slot, pl.ds(k, tm, stride=R2), :] for k in range(ROW_TILES)], axis=1)
    yhi = jnp.concatenate([buf[slot, pl.ds(ROW_TILES + k, tm, stride=R2), :] for k in range(ROW_TILES)],
                          axis=1)
    y = w_ref[:, 0:1] * ylo + w_ref[:, 1:2] * yhi
    x2 = x1_ref[...] + mod_ref[5:6, :] * y
    r = lax.rsqrt(jnp.mean(x2 * x2, axis=-1, keepdims=True) + EPS)
    o_ref[...] = x2 * r * fg_ref[...]


def _combine(pos, ys, w, x1, mod, fg):
    N, D = x1.shape
    B = mod.shape[0]
    tm = COMBINE_ROWS
    per_b = (N // B) // tm
    return pl.pallas_call(
        _combine_kernel,
        grid_spec=pltpu.PrefetchScalarGridSpec(
            num_scalar_prefetch=1,
            grid=(N // tm,),
            in_specs=[pl.BlockSpec(memory_space=pl.ANY),
                      pl.BlockSpec((tm, 2), lambda i, p: (i, 0)),
                      pl.BlockSpec((tm, D), lambda i, p: (i, 0)),
                      pl.BlockSpec((None, 6, D), lambda i, p: (i // per_b, 0, 0)),
                      pl.BlockSpec((1, D), lambda i, p: (0, 0))],
            out_specs=pl.BlockSpec((tm, D), lambda i, p: (i, 0)),
            scratch_shapes=[pltpu.VMEM((2, tm * 2 * ROW_TILES, LANES), F32),
                            pltpu.SemaphoreType.DMA((2,))]),
        out_shape=jax.ShapeDtypeStruct((N, D), F32),
        compiler_params=_params("arbitrary"),
        name="moe_combine",
    )(pos, ys, w, x1, mod, fg)


def _dup_groups(w):
    a, b = w[:, :HEAD_DIM], w[:, HEAD_DIM:]
    return jnp.concatenate([a, a, b, b], axis=1)


def _pad_heads(w):
    d, n = w.shape
    w = w.reshape(d, n // HEAD_DIM, HEAD_DIM)
    return jnp.pad(w, ((0, 0), (0, 0), (0, LANES - HEAD_DIM))).reshape(d, n // HEAD_DIM * LANES)


def _arrange_w_in(w):
    cuts = np.cumsum([0, POOL_WIDTH, ATTN_WIDTH] + [KV_WIDTH] * 6 + [N_BRANCH * N_HEADS, 2 * D_MODEL])
    pieces = [w[:, int(cuts[i]):int(cuts[i + 1])] for i in range(len(cuts) - 1)]
    pool, q, kc, vc, ks, vs, kw, vw, bg, mg = pieces
    bg = jnp.pad(bg, ((0, 0), (0, LANES - bg.shape[1])))
    out = jnp.concatenate([pool, _pad_heads(q), kc, vc, _pad_heads(ks), _dup_groups(vs),
                           _pad_heads(kw), _dup_groups(vw), bg, mg], axis=1)
    return out.astype(BF)


def _compress_weights(pos, w1, b1, w2, dup):
    second = jnp.asarray([1.0, 1.0 if dup else 0.0], F32)
    eye = jnp.eye(N_KV_GROUPS, dtype=F32)
    halves = CMP_BLOCK // CMP_STRIDE
    w1r = w1.reshape(halves, CMP_STRIDE, HEAD_DIM, CMP_HIDDEN)
    w1big = jnp.einsum('hidc,gk->igdkhc', w1r, eye).reshape(
        CMP_STRIDE * N_KV_GROUPS * HEAD_DIM, N_KV_GROUPS * halves * CMP_HIDDEN)
    w2big = jnp.einsum('cd,gk,r->gckrd', w2, eye, second).reshape(
        N_KV_GROUPS * CMP_HIDDEN, N_KV_GROUPS * 2 * HEAD_DIM)
    pos8 = jnp.broadcast_to(pos.reshape(1, CMP_BLOCK * HEAD_DIM), (8, CMP_BLOCK * HEAD_DIM))
    return (w1big.astype(BF), pos8.astype(BF), w1.astype(BF), b1.reshape(1, CMP_HIDDEN),
            w2big.astype(BF))


def _selection_tables(S):
    n_chunks = S // CMP_STRIDE
    n_cmp = n_chunks - CMP_BLOCK // CMP_STRIDE + 1
    n_blk = S // SEL_BLOCK
    s1 = np.arange(n_cmp)[:, None] * CMP_STRIDE
    s2 = np.arange(n_blk)[None, :] * SEL_BLOCK
    ovl = np.clip(np.minimum(s1 + CMP_BLOCK, s2 + SEL_BLOCK) - np.maximum(s1, s2), 0, None) / CMP_BLOCK
    ovt = np.zeros((n_blk, n_chunks), np.float32)
    ovt[:, :n_cmp] = ovl.T
    return jnp.asarray(ovt, BF)


def kernel(x, c, ada_w, ada_b, norm1_g, w_in, pool_w, pool_scale, cmp_pos, cmp_w1, cmp_b1, cmp_w2,
           w_up_pool, w_up_attn, w_out, norm2_g, router_g_w, router_g_b, router_e_w, router_e_b,
           exp_w_gate, exp_w_up, exp_w_down, final_g):
    B, S, D = x.shape
    N = B * S
    assert ada_w.shape[0] == 1, "the final norm is fused into the last layer's combine step"
    for l in range(ada_w.shape[0]):
        mod = _ada(c, ada_w[l], ada_b[l]).reshape(B, 6, D)
        (up, q, kc, vc, ks, vs, kw, vw, bg, gm) = _inproj(
            x, mod, norm1_g[l].reshape(1, D), _arrange_w_in(w_in[l]))
        n_chunks = S // CMP_STRIDE
        kcc = _compress(kc.reshape(B, n_chunks, CMP_STRIDE * KV_WIDTH),
                        *_compress_weights(cmp_pos[l, 0], cmp_w1[l, 0], cmp_b1[l, 0], cmp_w2[l, 0], False))
        vcc = _compress(vc.reshape(B, n_chunks, CMP_STRIDE * KV_WIDTH),
                        *_compress_weights(cmp_pos[l, 1], cmp_w1[l, 1], cmp_b1[l, 1], cmp_w2[l, 1], True))
        o = _attention(q, kcc, vcc, ks, vs, kw, vw, bg, _selection_tables(S))
        wr = jnp.zeros((8 + N_EXPERTS, D), F32)
        wr = wr.at[0:N_GROUPS].set(router_g_w[l].T).at[8:].set(router_e_w[l].T).astype(BF)
        br = jnp.zeros((8 + N_EXPERTS, 1), F32)
        br = br.at[0:N_GROUPS, 0].set(router_g_b[l]).at[8:, 0].set(router_e_b[l])
        x1, h2t, crk, rw, cnt = _mixer_out(
            up, o, gm, x, mod, pool_w[l].astype(BF), pool_scale[l].reshape(1, POOL_WIDTH),
            w_up_pool[l].astype(BF), w_up_attn[l].astype(BF), w_out[l].astype(BF),
            norm2_g[l].reshape(1, D), wr, br)
        n_tiles = N // EXPERT_TILE + N_CLASS
        n_tiles_pad = -(-n_tiles // LANES) * LANES
        pos, tmap = _plan(crk, cnt, n_tiles_pad)
        pos = pos.reshape(N)
        xs = _dispatch(pos, h2t, n_tiles * EXPERT_TILE)
        ys = _experts(tmap, xs, exp_w_gate[l].astype(BF), exp_w_up[l].astype(BF),
                      exp_w_down[l].astype(BF), n_tiles)
        w = jnp.stack([rw[:, 0, :].reshape(N), rw[:, 1, :].reshape(N)], axis=1)
        y = _combine(pos, ys, w, x1.reshape(N, D), mod, final_g.reshape(1, D))
        x = y.reshape(B, S, D)
    return x
```

```python
import functools

import numpy as np
import jax
import jax.numpy as jnp
from jax import lax
from jax.experimental import pallas as pl
from jax.experimental.pallas import tpu as pltpu

BF = jnp.bfloat16
F32 = jnp.float32
I32 = jnp.int32

D_MODEL = 1024
POOL_WIDTH = 512
POOL_WINDOWS = (2, 4, 8, 16)
POOL_GROUP = 128
POOL_HALO = 16
N_HEADS = 8
HEAD_DIM = 64
N_KV_GROUPS = 2
HEADS_PER_GROUP = 4
ATTN_WIDTH = 512
KV_WIDTH = 128
CMP_BLOCK = 32
CMP_STRIDE = 16
CMP_HIDDEN = 256
SEL_BLOCK = 64
N_SELECT = 8
WINDOW = 512
Q_CHUNK = 64
N_BRANCH = 3
N_GROUPS = 4
EXPERTS_PER_GROUP = 8
N_EXPERTS = 32
EXPERT_FF = 512
EPS = 1e-6
NEG = -1e30
FORCE_SCORE = 1e4
QK_SCALE = HEAD_DIM ** -0.5

PAIRS_PER_GROUP = EXPERTS_PER_GROUP * (EXPERTS_PER_GROUP - 1) // 2
N_CLASS = N_GROUPS * PAIRS_PER_GROUP
N_CLASS_PAD = 128
ROUTE_TILE = 512
EXPERT_TILE_LOG2 = 8
EXPERT_TILE = 1 << EXPERT_TILE_LOG2
DISPATCH_ROWS = 1024
COMBINE_ROWS = 256
DMA_UNROLL = 8

LANES = 128
ROW_TILES = D_MODEL // LANES
SEL_KEY_TILE = 512
ATTN_BATCH = 2
VMEM_LIMIT = 56 * 1024 * 1024

C_POOL = 0
C_Q = C_POOL + POOL_WIDTH
C_KC = C_Q + ATTN_WIDTH
C_VC = C_KC + KV_WIDTH
C_KS = C_VC + KV_WIDTH
C_VS = C_KS + KV_WIDTH
C_KW = C_VS + KV_WIDTH
C_VW = C_KW + KV_WIDTH
C_BG = C_VW + KV_WIDTH
C_MG = C_BG + LANES
C_END = C_MG + 2 * D_MODEL


def _dot(a, b):
    return jnp.dot(a, b, preferred_element_type=F32)


def _dot_nt(a, b):
    return lax.dot_general(a, b, (((1,), (1,)), ((), ())), preferred_element_type=F32)


def _params(*sem):
    return pltpu.CompilerParams(dimension_semantics=sem, vmem_limit_bytes=VMEM_LIMIT)


def _ada_kernel(c_ref, w_ref, b_ref, o_ref):
    o_ref[...] = _dot(c_ref[...].astype(BF), w_ref[...].astype(BF)) + b_ref[...]


def _ada(c, w, b):
    B, D = c.shape
    n = w.shape[1]
    tn = 1024
    return pl.pallas_call(
        _ada_kernel,
        grid=(n // tn,),
        in_specs=[pl.BlockSpec((B, D), lambda j: (0, 0)),
                  pl.BlockSpec((D, tn), lambda j: (0, j)),
                  pl.BlockSpec((1, tn), lambda j: (0, j))],
        out_specs=pl.BlockSpec((B, tn), lambda j: (0, j)),
        out_shape=jax.ShapeDtypeStruct((B, n), F32),
        compiler_params=_params("arbitrary"),
        name="ada_mod",
    )(c, w, b.reshape(1, n))


def _inproj_kernel(x_ref, mod_ref, g_ref, w_ref, blk_ref, up_ref, q_ref, kc_ref, vc_ref,
                   ks_ref, vs_ref, kw_ref, vw_ref, bg_ref, gm_ref):
    x = x_ref[...]
    r = lax.rsqrt(jnp.mean(x * x, axis=-1, keepdims=True) + EPS)
    h = x * r * g_ref[...] * (1.0 + mod_ref[1:2, :]) + mod_ref[0:1, :]
    hb = h.astype(BF)

    def proj(a, b):
        return _dot(hb, w_ref[:, a:b])

    def spread(v, dup):
        zero = jnp.zeros((v.shape[0], HEAD_DIM), v.dtype)
        pieces = []
        for i in range(v.shape[1] // HEAD_DIM):
            piece = v[:, i * HEAD_DIM:(i + 1) * HEAD_DIM]
            pieces += [piece, piece if dup else zero]
        return jnp.concatenate(pieces, axis=1)

    up_ref[...] = proj(C_POOL, C_Q)
    q_ref[...] = spread(proj(C_Q, C_KC) * QK_SCALE, False).astype(BF)
    kc_ref[...] = proj(C_KC, C_VC).astype(BF)
    vc_ref[...] = proj(C_VC, C_KS).astype(BF)
    ks_ref[...] = (spread(proj(C_KS, C_VS), False) + blk_ref[...]).astype(BF)
    vs_ref[...] = spread(proj(C_VS, C_KW), True).astype(BF)
    kw_ref[...] = spread(proj(C_KW, C_VW), False).astype(BF)
    vw_ref[...] = spread(proj(C_VW, C_BG), True).astype(BF)
    bg_ref[...] = proj(C_BG, C_MG)
    gm_ref[...] = jax.nn.sigmoid(proj(C_MG, C_END)).astype(BF)


def _inproj(x, mod, g, w):
    B, S, D = x.shape
    tm = 512
    blk = np.zeros((S, 2 * LANES), np.float32)
    for gg in range(N_KV_GROUPS):
        blk[np.arange(S), gg * LANES + HEAD_DIM + np.arange(S) // SEL_BLOCK] = 1.0
    blk = jnp.asarray(blk)
    widths = [(POOL_WIDTH, F32), (N_HEADS * LANES, BF), (KV_WIDTH, BF), (KV_WIDTH, BF),
              (2 * KV_WIDTH, BF), (2 * KV_WIDTH, BF), (2 * KV_WIDTH, BF), (2 * KV_WIDTH, BF),
              (LANES, F32), (2 * D_MODEL, BF)]
    row = lambda n: pl.BlockSpec((None, tm, n), lambda b, i: (b, i, 0))
    return pl.pallas_call(
        _inproj_kernel,
        grid=(B, S // tm),
        in_specs=[row(D),
                  pl.BlockSpec((None, 6, D), lambda b, i: (b, 0, 0)),
                  pl.BlockSpec((1, D), lambda b, i: (0, 0)),
                  pl.BlockSpec((D, C_END), lambda b, i: (0, 0)),
                  pl.BlockSpec((tm, 2 * LANES), lambda b, i: (i, 0))],
        out_specs=[row(n) for n, _ in widths],
        out_shape=[jax.ShapeDtypeStruct((B, S, n), dt) for n, dt in widths],
        compiler_params=_params("arbitrary", "arbitrary"),
        name="norm1_inproj",
    )(x, mod, g, w, blk)


def _gelu_tanh(x):
    return 0.5 * x * (1.0 + jnp.tanh(0.7978845608028654 * (x + 0.044715 * x * x * x)))


def _compress_kernel(x_ref, w1b_ref, pos_ref, w1_ref, b1_ref, w2b_ref, o_ref):
    y = _dot(x_ref[...], w1b_ref[...])
    posc = _dot(pos_ref[...], w1_ref[...])[0:1, :] + b1_ref[...]
    n = y.shape[0]
    acts = []
    for g in range(N_KV_GROUPS):
        first = y[:, g * 2 * CMP_HIDDEN: g * 2 * CMP_HIDDEN + CMP_HIDDEN]
        second = y[:, g * 2 * CMP_HIDDEN + CMP_HIDDEN: (g + 1) * 2 * CMP_HIDDEN]
        pre = first + pltpu.roll(second, n - 1, 0) + posc
        acts.append(_gelu_tanh(pre).astype(BF))
    act = jnp.concatenate(acts, axis=1)
    o_ref[...] = _dot(act, w2b_ref[...]).astype(BF)


def _compress(xk, w1big, pos8, w1, b1, w2big):
    B, n, width = xk.shape
    full = lambda a: pl.BlockSpec(a.shape, lambda b: (0,) * a.ndim)
    return pl.pallas_call(
        _compress_kernel,
        grid=(B,),
        in_specs=[pl.BlockSpec((None, n, width), lambda b: (b, 0, 0)),
                  full(w1big), full(pos8), full(w1), full(b1), full(w2big)],
        out_specs=pl.BlockSpec((None, n, 2 * KV_WIDTH), lambda b: (b, 0, 0)),
        out_shape=jax.ShapeDtypeStruct((B, n, 2 * KV_WIDTH), BF),
        compiler_params=_params("arbitrary"),
        name="compress",
    )(xk, w1big, pos8, w1, b1, w2big)


def _masked_exp(s, mask):
    sm = jnp.where(mask, s, NEG)
    m = jnp.max(sm, axis=-1, keepdims=True)
    p = jnp.where(mask, jnp.exp(sm - m), 0.0)
    return p, jnp.sum(p, axis=-1, keepdims=True)


def _safe_inv(l):
    return jnp.where(l > 0.0, 1.0 / jnp.where(l > 0.0, l, 1.0), 0.0)


def _softmax_tile(s, m_old):
    m_new = jnp.maximum(m_old, jnp.max(s, axis=-1, keepdims=True))
    p = jnp.exp(s - m_new)
    return m_new, p, jnp.sum(p, axis=-1, keepdims=True)


def _attn_kernel(q_ref, kc_ref, vc_ref, ks_ref, vs_ref, kw_ref, vw_ref, bg_ref, ovt_ref, o_ref):
    ci = pl.program_id(1)
    q0 = ci * Q_CHUNK
    Q, H, G = Q_CHUNK, HEADS_PER_GROUP, N_KV_GROUPS
    R = H * Q
    n_blk = ovt_ref.shape[0]
    units = [(bb, g) for bb in range(q_ref.shape[0]) for g in range(G)]
    U = len(units)
    sig = [jax.nn.sigmoid(bg_ref[bb]) for bb in range(q_ref.shape[0])]
    t_q = q0 + lax.broadcasted_iota(I32, (Q, 1), 0)
    t_r = jnp.concatenate([t_q] * H, axis=0)

    def rows4(a):
        return jnp.concatenate([a] * H, axis=0)

    def q_rows(bb, g):
        return jnp.concatenate(
            [q_ref[bb, :, (g * H + h) * LANES:(g * H + h + 1) * LANES] for h in range(H)], axis=0)

    gcs = [slice(g * LANES, (g + 1) * LANES) for g in range(G)]
    qp = [q_rows(bb, g) for bb, g in units]

    w0 = pl.multiple_of(jnp.maximum(q0 - WINDOW, 0), Q_CHUNK)
    s3 = [_dot_nt(qp[u], kw_ref[bb, pl.ds(w0, WINDOW + Q_CHUNK), gcs[g]])
          for u, (bb, g) in enumerate(units)]
    s1 = [_dot_nt(qp[u], kc_ref[bb, :, gcs[g]]) for u, (bb, g) in enumerate(units)]

    n_idx = lax.broadcasted_iota(I32, s1[0].shape, 1)
    m1 = (n_idx * CMP_STRIDE + (CMP_BLOCK - 1)) <= t_r
    o1, inv1, psums = [], [], []
    for u, (bb, g) in enumerate(units):
        p1, l1 = _masked_exp(s1[u], m1)
        iv = _safe_inv(l1)
        o1.append(_dot(p1.astype(BF), vc_ref[bb, :, gcs[g]]))
        inv1.append(iv)
        p1n = p1 * iv
        psum = p1n[0:Q]
        for h in range(1, H):
            psum = psum + p1n[h * Q:(h + 1) * Q]
        psums.append(psum)
    psum = jnp.concatenate(psums, axis=0)
    hi = psum.astype(BF)
    lo = (psum - hi.astype(F32)).astype(BF)
    ps_t = _dot_nt(ovt_ref[...], hi) + _dot_nt(ovt_ref[...], lo)

    kpos3 = w0 + lax.broadcasted_iota(I32, (Q, WINDOW + Q_CHUNK), 1)
    bias3 = rows4(jnp.where((kpos3 <= t_q) & (kpos3 > t_q - WINDOW), 0.0, NEG))
    win = []
    for u, (bb, g) in enumerate(units):
        _, p3, l3 = _softmax_tile(s3[u] + bias3, jnp.full((R, 1), NEG, F32))
        win.append((_dot(p3.astype(BF), vw_ref[bb, pl.ds(w0, WINDOW + Q_CHUNK), gcs[g]]), l3))

    early, gate_sel = [], []
    for u, (bb, g) in enumerate(units):
        o3, l3 = win[u]
        inv3 = 1.0 / l3
        parts, gates = [], []
        for h in range(H):
            rows = slice(h * Q, (h + 1) * Q)
            col = g * H + h
            g1 = sig[bb][:, col:col + 1]
            g3 = sig[bb][:, 2 * N_HEADS + col:2 * N_HEADS + col + 1]
            parts.append((g1 * inv1[u][rows]) * o1[u][rows] + (g3 * inv3[rows]) * o3[rows])
            gates.append(jnp.broadcast_to(sig[bb][:, N_HEADS + col:N_HEADS + col + 1], (Q, LANES)))
        early.append(parts)
        gate_sel.append(gates)

    j = lax.broadcasted_iota(I32, ps_t.shape, 0)
    forced = (j == 0) | (j == ci) | (j == ci - 1)
    score = jnp.where(forced, FORCE_SCORE, jnp.where(j <= ci, ps_t, NEG))
    rank = jnp.zeros(ps_t.shape, I32)
    for jp in range(n_blk):
        c = score[jp:jp + 1, :]
        beats = (c > score) | ((c == score) & (j > jp))
        rank = rank + beats.astype(I32)
    bias_t = jnp.where(rank < N_SELECT, 0.0, NEG)
    pad_t = jnp.concatenate([jnp.zeros((HEAD_DIM, U * Q), F32), bias_t,
                             jnp.zeros((LANES - HEAD_DIM - n_blk, U * Q), F32)], axis=0)
    sel_bias = pad_t.T.astype(BF)

    qa = [qp[u] + rows4(sel_bias[u * Q:(u + 1) * Q]) for u in range(U)]

    def sel_tile(k0, carry, causal):
        s = [_dot_nt(qa[u], ks_ref[bb, pl.ds(k0, SEL_KEY_TILE), gcs[g]])
             for u, (bb, g) in enumerate(units)]
        if causal:
            kpos = k0 + lax.broadcasted_iota(I32, (Q, SEL_KEY_TILE), 1)
            bias = rows4(jnp.where(kpos <= t_q, 0.0, NEG))
            s = [su + bias for su in s]
        out = []
        for u, (bb, g) in enumerate(units):
            m, l, acc = carry[u]
            m_new, p, psum_ = _softmax_tile(s[u], m)
            alpha = jnp.exp(m - m_new)
            acc = alpha * acc + _dot(p.astype(BF), vs_ref[bb, pl.ds(k0, SEL_KEY_TILE), gcs[g]])
            out.append((m_new, alpha * l + psum_, acc))
        return tuple(out)

    def sweep(n_tiles):
        def run():
            carry = tuple((jnp.full((R, 1), NEG, F32), jnp.zeros((R, 1), F32),
                           jnp.zeros((R, LANES), F32)) for _ in range(U))
            for kt in range(n_tiles - 1):
                carry = sel_tile(kt * SEL_KEY_TILE, carry, False)
            return sel_tile((n_tiles - 1) * SEL_KEY_TILE, carry, True)
        return run

    blocks_per_tile = SEL_KEY_TILE // SEL_BLOCK
    max_tiles = n_blk // blocks_per_tile
    carry = lax.switch(ci // blocks_per_tile, [sweep(n) for n in range(1, max_tiles + 1)])

    lo_half = lax.broadcasted_iota(I32, (Q, LANES), 1) < HEAD_DIM
    for u, (bb, g) in enumerate(units):
        _, l2, o2 = carry[u]
        inv2 = 1.0 / l2
        heads = []
        for h in range(H):
            rows = slice(h * Q, (h + 1) * Q)
            heads.append(early[u][h] + (gate_sel[u][h] * inv2[rows]) * o2[rows])
        for k in range(H // 2):
            slab = jnp.where(lo_half, heads[2 * k], heads[2 * k + 1])
            c0 = (g * (H // 2) + k) * LANES
            o_ref[bb, :, c0:c0 + LANES] = slab.astype(BF)


def _attention(q, kc, vc, ks, vs, kw, vw, bg, ovt):
    B, S, _ = q.shape
    nq = S // Q_CHUNK
    nb = ATTN_BATCH if B % ATTN_BATCH == 0 else 1
    per_b = lambda a: pl.BlockSpec((nb,) + a.shape[1:], lambda b, i: (b, 0, 0))
    full = lambda a: pl.BlockSpec(a.shape, lambda b, i: (0,) * a.ndim)
    return pl.pallas_call(
        _attn_kernel,
        grid=(B // nb, nq),
        in_specs=[pl.BlockSpec((nb, Q_CHUNK, N_HEADS * LANES), lambda b, i: (b, i, 0)),
                  per_b(kc), per_b(vc), per_b(ks), per_b(vs), per_b(kw), per_b(vw),
                  pl.BlockSpec((nb, Q_CHUNK, LANES), lambda b, i: (b, i, 0)),
                  full(ovt)],
        out_specs=pl.BlockSpec((nb, Q_CHUNK, ATTN_WIDTH), lambda b, i: (b, i, 0)),
        out_shape=jax.ShapeDtypeStruct((B, S, ATTN_WIDTH), BF),
        compiler_params=_params("arbitrary", "arbitrary"),
        name="nsa_attention",
    )(q, kc, vc, ks, vs, kw, vw, bg, ovt)


def _mixer_out_kernel(upc_ref, upp_ref, o_ref, gm_ref, x_ref, mod_ref, pw_ref, psc_ref,
                      wup_ref, wua_ref, wo_ref, g2_ref, wr_ref, br_ref, tri_ref,
                      x1_ref, h2t_ref, crk_ref, rw_ref, cnt_ref, cnt_scr):
    i = pl.program_id(1)

    @pl.when((pl.program_id(0) == 0) & (i == 0))
    def _():
        cnt_scr[...] = jnp.zeros_like(cnt_scr)

    tm = upc_ref.shape[0]
    prev = upp_ref[...] * (i > 0).astype(F32)
    ext = jnp.concatenate([prev, upc_ref[...]], axis=0)
    t = i * tm + lax.broadcasted_iota(I32, (tm, 1), 0)
    ys = []
    for gi, w in enumerate(POOL_WINDOWS):
        u = ext[:, gi * POOL_GROUP:(gi + 1) * POOL_GROUP]
        acc = u
        shift = 1
        while shift < w:
            acc = acc + pltpu.roll(acc, shift, 0)
            shift *= 2
        inv_cnt = 1.0 / jnp.minimum(t + 1, w).astype(F32)
        p = acc[POOL_HALO:] * inv_cnt - u[POOL_HALO:]
        ys.append(_dot(p.astype(BF), pw_ref[gi]))
    y = jnp.concatenate(ys, axis=1) * psc_ref[...]
    y_pool = _dot(y.astype(BF), wup_ref[...])
    y_attn = _dot(o_ref[...], wua_ref[...])
    gm = gm_ref[...].astype(F32)
    mix = gm[:, :D_MODEL] * y_pool + gm[:, D_MODEL:] * y_attn
    x1 = x_ref[...] + mod_ref[2:3, :] * _dot(mix.astype(BF), wo_ref[...])
    x1_ref[...] = x1
    r = lax.rsqrt(jnp.mean(x1 * x1, axis=-1, keepdims=True) + EPS)
    h2f = x1 * r * g2_ref[...] * (1.0 + mod_ref[4:5, :]) + mod_ref[3:4, :]
    h2 = h2f.astype(BF)
    for k in range(ROW_TILES):
        h2t_ref[pl.ds(k, tm, stride=ROW_TILES), :] = h2f[:, k * LANES:(k + 1) * LANES]

    lt = _dot_nt(wr_ref[...], h2) + br_ref[...]
    lg = lt[0:N_GROUPS]
    gmax = jnp.max(lg, axis=0, keepdims=True)
    gi_ = lax.broadcasted_iota(I32, lg.shape, 0)
    gidx = jnp.min(jnp.where(lg == gmax, gi_, N_GROUPS), axis=0, keepdims=True)
    gp = 1.0 / jnp.sum(jnp.exp(lg - gmax), axis=0, keepdims=True)
    E = EXPERTS_PER_GROUP
    le = jnp.zeros((E, tm), F32)
    for gg in range(N_GROUPS):
        le = jnp.where(gidx == gg, lt[8 + gg * E:8 + (gg + 1) * E], le)
    ei = lax.broadcasted_iota(I32, le.shape, 0)
    v1 = jnp.max(le, axis=0, keepdims=True)
    i1 = jnp.min(jnp.where(le == v1, ei, E), axis=0, keepdims=True)
    rest = jnp.where(ei == i1, -jnp.inf, le)
    v2 = jnp.max(rest, axis=0, keepdims=True)
    i2 = jnp.min(jnp.where(rest == v2, ei, E), axis=0, keepdims=True)
    e = jnp.exp(v2 - v1)
    wa = gp / (1.0 + e)
    wb = gp * e / (1.0 + e)
    lo = jnp.minimum(i1, i2)
    hi = jnp.maximum(i1, i2)
    pair = lax.shift_right_logical(lo * (2 * E - 1 - lo), 1) + hi - lo - 1
    cls = gidx * PAIRS_PER_GROUP + pair
    first_lo = i1 < i2
    w_lo = jnp.where(first_lo, wa, wb)
    w_hi = jnp.where(first_lo, wb, wa)
    oh = lax.broadcasted_iota(I32, (N_CLASS_PAD, tm), 0) == cls
    before = _dot(oh.astype(BF), tri_ref[...]) + cnt_scr[:, 0:1]
    rank = jnp.sum(jnp.where(oh, before, 0.0), axis=0, keepdims=True).astype(I32)
    cnt_scr[...] = cnt_scr[...] + jnp.sum(oh.astype(F32), axis=1, keepdims=True)
    cnt_ref[...] = cnt_scr[...]
    row = lax.broadcasted_iota(I32, (8, tm), 0)
    crk_ref[...] = jnp.where(row == 0, cls, jnp.where(row == 1, rank, 0))
    rw_ref[...] = jnp.where(row == 0, w_lo, jnp.where(row == 1, w_hi, 0.0))


def _mixer_out(up, o, gm, x, mod, pw, psc, wup, wua, wo, g2, wr, br):
    B, S, D = x.shape
    tm = ROUTE_TILE
    nt = S // tm
    tri = jnp.asarray(np.triu(np.ones((tm, tm), np.float32), k=1), BF)
    row = lambda n: pl.BlockSpec((None, tm, n), lambda b, i: (b, i, 0))
    full = lambda a: pl.BlockSpec(a.shape, lambda b, i: (0,) * a.ndim)
    per = tm // POOL_HALO
    return pl.pallas_call(
        _mixer_out_kernel,
        grid=(B, nt),
        in_specs=[row(POOL_WIDTH),
                  pl.BlockSpec((None, POOL_HALO, POOL_WIDTH),
                               lambda b, i: (b, jnp.maximum(i * per - 1, 0), 0)),
                  row(ATTN_WIDTH), row(2 * D_MODEL), row(D),
                  pl.BlockSpec((None, 6, D), lambda b, i: (b, 0, 0)),
                  full(pw), full(psc), full(wup), full(wua), full(wo), full(g2),
                  full(wr), full(br), full(tri)],
        out_specs=[row(D),
                   pl.BlockSpec((tm * ROW_TILES, LANES), lambda b, i: (b * nt + i, 0)),
                   pl.BlockSpec((None, 8, tm), lambda b, i: (b * nt + i, 0, 0)),
                   pl.BlockSpec((None, 8, tm), lambda b, i: (b * nt + i, 0, 0)),
                   pl.BlockSpec((N_CLASS_PAD, LANES), lambda b, i: (0, 0))],
        out_shape=[jax.ShapeDtypeStruct((B, S, D), F32),
                   jax.ShapeDtypeStruct((B * S * ROW_TILES, LANES), F32),
                   jax.ShapeDtypeStruct((B * nt, 8, tm), I32),
                   jax.ShapeDtypeStruct((B * nt, 8, tm), F32),
                   jax.ShapeDtypeStruct((N_CLASS_PAD, LANES), F32)],
        scratch_shapes=[pltpu.VMEM((N_CLASS_PAD, LANES), F32)],
        compiler_params=_params("arbitrary", "arbitrary"),
        name="mixer_out_router",
    )(up, up, o, gm, x, mod, pw, psc, wup, wua, wo, g2, wr, br, tri)


def _plan_kernel(crk_ref, cnt_ref, etab_ref, pos_ref, tmap_ref):
    C = N_CLASS_PAD
    cnt = cnt_ref[:, 0:1].astype(I32)
    ntile = lax.shift_right_logical(cnt + (EXPERT_TILE - 1), EXPERT_TILE_LOG2)
    ntile_f = ntile.astype(F32)
    r = lax.broadcasted_iota(I32, (C, C), 0)
    c = lax.broadcasted_iota(I32, (C, C), 1)
    lower = (c < r).astype(BF)
    first = _dot(lower, jnp.broadcast_to(ntile_f, (C, LANES)).astype(BF))[:, 0:1]
    last = first + ntile_f
    total = jnp.sum(ntile_f, axis=0, keepdims=True)
    off = (first * EXPERT_TILE).astype(I32)

    def body(i, carry):
        cls = crk_ref[i, 0:1, :]
        rank = crk_ref[i, 1:2, :]
        oh = lax.broadcasted_iota(I32, (C, cls.shape[1]), 0) == cls
        pos_ref[pl.ds(i, 1), :] = jnp.sum(jnp.where(oh, off, 0), axis=0, keepdims=True) + rank
        return carry

    lax.fori_loop(0, crk_ref.shape[0], body, 0)

    nj = tmap_ref.shape[1]
    j = lax.broadcasted_iota(I32, (1, nj), 1).astype(F32)
    jj = jnp.minimum(j, total - 1.0)
    tcls = jnp.sum((last <= jj).astype(I32), axis=0, keepdims=True)
    oh2 = lax.broadcasted_iota(I32, (C, nj), 0) == tcls
    elo = jnp.sum(jnp.where(oh2, etab_ref[:, 0:1], 0), axis=0, keepdims=True)
    ehi = jnp.sum(jnp.where(oh2, etab_ref[:, 1:2], 0), axis=0, keepdims=True)
    row = lax.broadcasted_iota(I32, (8, nj), 0)
    tmap_ref[...] = jnp.where(
        row == 0, elo, jnp.where(row == 1, ehi, jnp.where(
            row == 2, (j < total).astype(I32), jnp.where(row == 3, jj.astype(I32), 0))))


def _plan(crk, cnt, n_tiles_pad):
    nt, _, tm = crk.shape
    etab = np.zeros((N_CLASS_PAD, LANES), np.int32)
    cid = 0
    for g in range(N_GROUPS):
        for lo in range(EXPERTS_PER_GROUP):
            for hi in range(lo + 1, EXPERTS_PER_GROUP):
                etab[cid, 0] = g * EXPERTS_PER_GROUP + lo
                etab[cid, 1] = g * EXPERTS_PER_GROUP + hi
                cid += 1
    etab = jnp.asarray(etab)
    full = lambda a: pl.BlockSpec(a.shape, lambda i: (0,) * a.ndim)
    return pl.pallas_call(
        _plan_kernel,
        grid=(1,),
        in_specs=[full(crk), full(cnt), full(etab)],
        out_specs=[pl.BlockSpec((nt, tm), lambda i: (0, 0)),
                   pl.BlockSpec((8, n_tiles_pad), lambda i: (0, 0))],
        out_shape=[jax.ShapeDtypeStruct((nt, tm), I32),
                   jax.ShapeDtypeStruct((8, n_tiles_pad), I32)],
        compiler_params=_params("arbitrary"),
        name="moe_plan",
    )(crk, cnt, etab)


def _dispatch_kernel(pos_ref, h_ref, xs_in_ref, xs_ref, sem):
    del xs_in_ref
    i = pl.program_id(0)
    rows = h_ref.shape[0] // ROW_TILES

    def issue(r8, carry):
        for k in range(DMA_UNROLL):
            r = r8 * DMA_UNROLL + k
            p = pos_ref[i * rows + r]
            pltpu.make_async_copy(
                h_ref.at[pl.ds(pl.multiple_of(r * ROW_TILES, ROW_TILES), ROW_TILES)],
                xs_ref.at[pl.ds(pl.multiple_of(p * ROW_TILES, ROW_TILES), ROW_TILES)],
                sem).start()
        return carry

    lax.fori_loop(0, rows // DMA_UNROLL, issue, 0)
    pltpu.make_async_copy(h_ref, xs_ref.at[pl.ds(0, rows * ROW_TILES)], sem).wait()


def _dispatch(pos, h2t, n_sorted_rows):
    rows = DISPATCH_ROWS
    n = h2t.shape[0] // ROW_TILES
    zeros = jnp.zeros((n_sorted_rows * ROW_TILES, LANES), F32)
    return pl.pallas_call(
        _dispatch_kernel,
        grid_spec=pltpu.PrefetchScalarGridSpec(
            num_scalar_prefetch=1,
            grid=(n // rows,),
            in_specs=[pl.BlockSpec((rows * ROW_TILES, LANES), lambda i, p: (i, 0)),
                      pl.BlockSpec(memory_space=pl.ANY)],
            out_specs=pl.BlockSpec(memory_space=pl.ANY),
            scratch_shapes=[pltpu.SemaphoreType.DMA]),
        out_shape=jax.ShapeDtypeStruct(zeros.shape, F32),
        input_output_aliases={2: 0},
        compiler_params=_params("arbitrary"),
        name="moe_dispatch",
    )(pos, h2t, zeros)


def _expert_kernel(elo_ref, ehi_ref, valid_ref, blk_ref, x_ref, wg0, wu0, wd0, wg1, wu1, wd1, o_ref):
    del elo_ref, ehi_ref, blk_ref
    T = EXPERT_TILE
    valid = valid_ref[pl.program_id(0)] > 0

    @pl.when(jnp.logical_not(valid))
    def _():
        o_ref[...] = jnp.zeros_like(o_ref)

    @pl.when(valid)
    def _():
        x = jnp.concatenate([x_ref[pl.ds(k, T, stride=ROW_TILES), :] for k in range(ROW_TILES)],
                            axis=1).astype(BF)
        for half, (wg, wu, wd) in enumerate(((wg0, wu0, wd0), (wg1, wu1, wd1))):
            a = _dot(x, wg[...])
            b = _dot(x, wu[...])
            he = (a * jax.nn.sigmoid(a)) * b
            y = _dot(he.astype(BF), wd[...])
            for k in range(ROW_TILES):
                o_ref[pl.ds(half * ROW_TILES + k, T, stride=2 * ROW_TILES), :] = (
                    y[:, k * LANES:(k + 1) * LANES])


def _experts(tmap, xs, wg, wu, wd, n_tiles):
    T = EXPERT_TILE
    D, F = D_MODEL, EXPERT_FF
    lo = lambda shape: pl.BlockSpec((None,) + shape, lambda j, elo, ehi, v, blk: (elo[j], 0, 0))
    hi = lambda shape: pl.BlockSpec((None,) + shape, lambda j, elo, ehi, v, blk: (ehi[j], 0, 0))
    return pl.pallas_call(
        _expert_kernel,
        grid_spec=pltpu.PrefetchScalarGridSpec(
            num_scalar_prefetch=4,
            grid=(n_tiles,),
            in_specs=[pl.BlockSpec((T * ROW_TILES, LANES), lambda j, elo, ehi, v, blk: (blk[j], 0)),
                      lo((D, F)), lo((D, F)), lo((F, D)), hi((D, F)), hi((D, F)), hi((F, D))],
            out_specs=pl.BlockSpec((T * 2 * ROW_TILES, LANES), lambda j, elo, ehi, v, blk: (j, 0))),
        out_shape=jax.ShapeDtypeStruct((n_tiles * T * 2 * ROW_TILES, LANES), F32),
        compiler_params=_params("arbitrary"),
        name="moe_experts",
    )(tmap[0], tmap[1], tmap[2], tmap[3], xs, wg, wu, wd, wg, wu, wd)


def _combine_kernel(pos_ref, ys_ref, w_ref, x1_ref, mod_ref, fg_ref, o_ref, buf, sem):
    i = pl.program_id(0)
    tm = x1_ref.shape[0]
    R2 = 2 * ROW_TILES

    def gather(step, slot):
        def issue(r8, carry):
            for k in range(DMA_UNROLL):
                r = r8 * DMA_UNROLL + k
                p = pos_ref[step * tm + r]
                pltpu.make_async_copy(ys_ref.at[pl.ds(pl.multiple_of(p * R2, R2), R2)],
                                      buf.at[slot, pl.ds(pl.multiple_of(r * R2, R2), R2)],
                                      sem.at[slot]).start()
            return carry
        lax.fori_loop(0, tm // DMA_UNROLL, issue, 0)

    @pl.when(i == 0)
    def _():
        gather(0, 0)

    slot = i % 2

    @pl.when(i + 1 < pl.num_programs(0))
    def _():
        gather(i + 1, 1 - slot)

    pltpu.make_async_copy(ys_ref.at[pl.ds(0, tm * R2)], buf.at[slot], sem.at[slot]).wait()
    ylo = jnp.concatenate([buf[slot, pl.ds(k, tm, stride=R2), :] for k in range(ROW_TILES)], axis=1)
    yhi = jnp.concatenate([buf[slot, pl.ds(ROW_TILES + k, tm, stride=R2), :] for k in range(ROW_TILES)],
                          axis=1)
    y = w_ref[:, 0:1] * ylo + w_ref[:, 1:2] * yhi
    x2 = x1_ref[...] + mod_ref[5:6, :] * y
    r = lax.rsqrt(jnp.mean(x2 * x2, axis=-1, keepdims=True) + EPS)
    o_ref[...] = x2 * r * fg_ref[...]


def _combine(pos, ys, w, x1, mod, fg):
    N, D = x1.shape
    B = mod.shape[0]
    tm = COMBINE_ROWS
    per_b = (N // B) // tm
    return pl.pallas_call(
        _combine_kernel,
        grid_spec=pltpu.PrefetchScalarGridSpec(
            num_scalar_prefetch=1,
            grid=(N // tm,),
            in_specs=[pl.BlockSpec(memory_space=pl.ANY),
                      pl.BlockSpec((tm, 2), lambda i, p: (i, 0)),
                      pl.BlockSpec((tm, D), lambda i, p: (i, 0)),
                      pl.BlockSpec((None, 6, D), lambda i, p: (i // per_b, 0, 0)),
                      pl.BlockSpec((1, D), lambda i, p: (0, 0))],
            out_specs=pl.BlockSpec((tm, D), lambda i, p: (i, 0)),
            scratch_shapes=[pltpu.VMEM((2, tm * 2 * ROW_TILES, LANES), F32),
                            pltpu.SemaphoreType.DMA((2,))]),
        out_shape=jax.ShapeDtypeStruct((N, D), F32),
        compiler_params=_params("arbitrary"),
        name="moe_combine",
    )(pos, ys, w, x1, mod, fg)


def _arrange_w_in(w):
    gate0 = POOL_WIDTH + ATTN_WIDTH + 6 * KV_WIDTH
    gate1 = gate0 + N_BRANCH * N_HEADS
    pad = jnp.zeros((w.shape[0], LANES - N_BRANCH * N_HEADS), w.dtype)
    return jnp.concatenate([w[:, :gate1], pad, w[:, gate1:]], axis=1).astype(BF)


def _compress_weights(pos, w1, b1, w2, dup):
    second = jnp.asarray([1.0, 1.0 if dup else 0.0], F32)
    eye = jnp.eye(N_KV_GROUPS, dtype=F32)
    halves = CMP_BLOCK // CMP_STRIDE
    w1r = w1.reshape(halves, CMP_STRIDE, HEAD_DIM, CMP_HIDDEN)
    w1big = jnp.einsum('hidc,gk->igdkhc', w1r, eye).reshape(
        CMP_STRIDE * N_KV_GROUPS * HEAD_DIM, N_KV_GROUPS * halves * CMP_HIDDEN)
    w2big = jnp.einsum('cd,gk,r->gckrd', w2, eye, second).reshape(
        N_KV_GROUPS * CMP_HIDDEN, N_KV_GROUPS * 2 * HEAD_DIM)
    pos8 = jnp.broadcast_to(pos.reshape(1, CMP_BLOCK * HEAD_DIM), (8, CMP_BLOCK * HEAD_DIM))
    return (w1big.astype(BF), pos8.astype(BF), w1.astype(BF), b1.reshape(1, CMP_HIDDEN),
            w2big.astype(BF))


def _selection_tables(S):
    n_chunks = S // CMP_STRIDE
    n_cmp = n_chunks - CMP_BLOCK // CMP_STRIDE + 1
    n_blk = S // SEL_BLOCK
    s1 = np.arange(n_cmp)[:, None] * CMP_STRIDE
    s2 = np.arange(n_blk)[None, :] * SEL_BLOCK
    ovl = np.clip(np.minimum(s1 + CMP_BLOCK, s2 + SEL_BLOCK) - np.maximum(s1, s2), 0, None) / CMP_BLOCK
    ovt = np.zeros((n_blk, n_chunks), np.float32)
    ovt[:, :n_cmp] = ovl.T
    return jnp.asarray(ovt, BF)


def kernel(x, c, ada_w, ada_b, norm1_g, w_in, pool_w, pool_scale, cmp_pos, cmp_w1, cmp_b1, cmp_w2,
           w_up_pool, w_up_attn, w_out, norm2_g, router_g_w, router_g_b, router_e_w, router_e_b,
           exp_w_gate, exp_w_up, exp_w_down, final_g):
    B, S, D = x.shape
    N = B * S
    assert ada_w.shape[0] == 1, "the final norm is fused into the last layer's combine step"
    for l in range(ada_w.shape[0]):
        mod = _ada(c, ada_w[l], ada_b[l]).reshape(B, 6, D)
        (up, q, kc, vc, ks, vs, kw, vw, bg, gm) = _inproj(
            x, mod, norm1_g[l].reshape(1, D), _arrange_w_in(w_in[l]))
        n_chunks = S // CMP_STRIDE
        kcc = _compress(kc.reshape(B, n_chunks, CMP_STRIDE * KV_WIDTH),
                        *_compress_weights(cmp_pos[l, 0], cmp_w1[l, 0], cmp_b1[l, 0], cmp_w2[l, 0], False))
        vcc = _compress(vc.reshape(B, n_chunks, CMP_STRIDE * KV_WIDTH),
                        *_compress_weights(cmp_pos[l, 1], cmp_w1[l, 1], cmp_b1[l, 1], cmp_w2[l, 1], True))
        o = _attention(q, kcc, vcc, ks, vs, kw, vw, bg, _selection_tables(S))
        wr = jnp.zeros((8 + N_EXPERTS, D), F32)
        wr = wr.at[0:N_GROUPS].set(router_g_w[l].T).at[8:].set(router_e_w[l].T).astype(BF)
        br = jnp.zeros((8 + N_EXPERTS, 1), F32)
        br = br.at[0:N_GROUPS, 0].set(router_g_b[l]).at[8:, 0].set(router_e_b[l])
        x1, h2t, crk, rw, cnt = _mixer_out(
            up, o, gm, x, mod, pool_w[l].astype(BF), pool_scale[l].reshape(1, POOL_WIDTH),
            w_up_pool[l].astype(BF), w_up_attn[l].astype(BF), w_out[l].astype(BF),
            norm2_g[l].reshape(1, D), wr, br)
        n_tiles = N // EXPERT_TILE + N_CLASS
        n_tiles_pad = -(-n_tiles // LANES) * LANES
        pos, tmap = _plan(crk, cnt, n_tiles_pad)
        pos = pos.reshape(N)
        xs = _dispatch(pos, h2t, n_tiles * EXPERT_TILE)
        ys = _experts(tmap, xs, exp_w_gate[l].astype(BF), exp_w_up[l].astype(BF),
                      exp_w_down[l].astype(BF), n_tiles)
        w = jnp.stack([rw[:, 0, :].reshape(N), rw[:, 1, :].reshape(N)], axis=1)
        y = _combine(pos, ys, w, x1.reshape(N, D), mod, final_g.reshape(1, D))
        x = y.reshape(B, S, D)
    return x
```

```python
import functools

import numpy as np
import jax
import jax.numpy as jnp
from jax import lax
from jax.experimental import pallas as pl
from jax.experimental.pallas import tpu as pltpu

BF = jnp.bfloat16
F32 = jnp.float32
I32 = jnp.int32

D_MODEL = 1024
POOL_WIDTH = 512
POOL_WINDOWS = (2, 4, 8, 16)
POOL_GROUP = 128
POOL_HALO = 16
N_HEADS = 8
HEAD_DIM = 64
N_KV_GROUPS = 2
HEADS_PER_GROUP = 4
ATTN_WIDTH = 512
KV_WIDTH = 128
CMP_BLOCK = 32
CMP_STRIDE = 16
CMP_HIDDEN = 256
SEL_BLOCK = 64
N_SELECT = 8
WINDOW = 512
Q_CHUNK = 64
N_BRANCH = 3
N_GROUPS = 4
EXPERTS_PER_GROUP = 8
N_EXPERTS = 32
EXPERT_FF = 512
EPS = 1e-6
NEG = -1e30
FORCE_SCORE = 1e4
QK_SCALE = HEAD_DIM ** -0.5
LOG2E = 1.4426950408889634

PAIRS_PER_GROUP = EXPERTS_PER_GROUP * (EXPERTS_PER_GROUP - 1) // 2
N_CLASS = N_GROUPS * PAIRS_PER_GROUP
N_CLASS_PAD = 128
ROUTE_TILE = 512
EXPERT_TILE_LOG2 = 8
EXPERT_TILE = 1 << EXPERT_TILE_LOG2
DISPATCH_ROWS = 1024
COMBINE_ROWS = 256
DMA_UNROLL = 8

LANES = 128
ROW_TILES = D_MODEL // LANES
SEL_KEY_TILE = 512
ATTN_BATCH = 2
VMEM_LIMIT = 56 * 1024 * 1024

C_POOL = 0
C_Q = C_POOL + POOL_WIDTH
C_KC = C_Q + ATTN_WIDTH
C_VC = C_KC + KV_WIDTH
C_KS = C_VC + KV_WIDTH
C_VS = C_KS + KV_WIDTH
C_KW = C_VS + KV_WIDTH
C_VW = C_KW + KV_WIDTH
C_BG = C_VW + KV_WIDTH
C_MG = C_BG + LANES
C_END = C_MG + 2 * D_MODEL


def _dot(a, b):
    return jnp.dot(a, b, preferred_element_type=F32)


def _dot_nt(a, b):
    return lax.dot_general(a, b, (((1,), (1,)), ((), ())), preferred_element_type=F32)


def _params(*sem):
    return pltpu.CompilerParams(dimension_semantics=sem, vmem_limit_bytes=VMEM_LIMIT)


def _ada_kernel(c_ref, w_ref, b_ref, o_ref):
    o_ref[...] = _dot(c_ref[...].astype(BF), w_ref[...].astype(BF)) + b_ref[...]


def _ada(c, w, b):
    B, D = c.shape
    n = w.shape[1]
    tn = 1024
    return pl.pallas_call(
        _ada_kernel,
        grid=(n // tn,),
        in_specs=[pl.BlockSpec((B, D), lambda j: (0, 0)),
                  pl.BlockSpec((D, tn), lambda j: (0, j)),
                  pl.BlockSpec((1, tn), lambda j: (0, j))],
        out_specs=pl.BlockSpec((B, tn), lambda j: (0, j)),
        out_shape=jax.ShapeDtypeStruct((B, n), F32),
        compiler_params=_params("arbitrary"),
        name="ada_mod",
    )(c, w, b.reshape(1, n))


def _inproj_kernel(x_ref, mod_ref, g_ref, w_ref, blk_ref, up_ref, q_ref, kc_ref, vc_ref,
                   ks_ref, vs_ref, kw_ref, vw_ref, bg_ref, gm_ref):
    x = x_ref[...]
    r = lax.rsqrt(jnp.mean(x * x, axis=-1, keepdims=True) + EPS)
    h = x * r * g_ref[...] * (1.0 + mod_ref[1:2, :]) + mod_ref[0:1, :]
    hb = h.astype(BF)

    def proj(a, b):
        return _dot(hb, w_ref[:, a:b])

    def spread(v, fill):
        pad = jnp.full((v.shape[0], HEAD_DIM), fill, v.dtype)
        pieces = []
        for i in range(v.shape[1] // HEAD_DIM):
            pieces += [v[:, i * HEAD_DIM:(i + 1) * HEAD_DIM], pad]
        return jnp.concatenate(pieces, axis=1)

    up_ref[...] = proj(C_POOL, C_Q)
    q_ref[...] = spread(proj(C_Q, C_KC) * QK_SCALE, 0.0).astype(BF)
    kc_ref[...] = proj(C_KC, C_VC).astype(BF)
    vc_ref[...] = proj(C_VC, C_KS).astype(BF)
    ks_ref[...] = (spread(proj(C_KS, C_VS) * LOG2E, 0.0) + blk_ref[...]).astype(BF)
    kw_ref[...] = spread(proj(C_KW, C_VW) * LOG2E, 0.0).astype(BF)
    vs_ref[...] = spread(proj(C_VS, C_KW), 1.0).astype(BF)
    vw_ref[...] = spread(proj(C_VW, C_BG), 1.0).astype(BF)
    bg_ref[...] = proj(C_BG, C_MG)
    gm_ref[...] = jax.nn.sigmoid(proj(C_MG, C_END)).astype(BF)


def _inproj(x, mod, g, w):
    B, S, D = x.shape
    tm = 512
    blk = np.zeros((S, 2 * LANES), np.float32)
    for gg in range(N_KV_GROUPS):
        blk[np.arange(S), gg * LANES + HEAD_DIM + np.arange(S) // SEL_BLOCK] = 1.0
    blk = jnp.asarray(blk)
    widths = [(POOL_WIDTH, F32), (N_HEADS * LANES, BF), (KV_WIDTH, BF), (KV_WIDTH, BF),
              (2 * KV_WIDTH, BF), (2 * KV_WIDTH, BF), (2 * KV_WIDTH, BF), (2 * KV_WIDTH, BF),
              (LANES, F32), (2 * D_MODEL, BF)]
    row = lambda n: pl.BlockSpec((None, tm, n), lambda b, i: (b, i, 0))
    return pl.pallas_call(
        _inproj_kernel,
        grid=(B, S // tm),
        in_specs=[row(D),
                  pl.BlockSpec((None, 6, D), lambda b, i: (b, 0, 0)),
                  pl.BlockSpec((1, D), lambda b, i: (0, 0)),
                  pl.BlockSpec((D, C_END), lambda b, i: (0, 0)),
                  pl.BlockSpec((tm, 2 * LANES), lambda b, i: (i, 0))],
        out_specs=[row(n) for n, _ in widths],
        out_shape=[jax.ShapeDtypeStruct((B, S, n), dt) for n, dt in widths],
        compiler_params=_params("arbitrary", "arbitrary"),
        name="norm1_inproj",
    )(x, mod, g, w, blk)


def _gelu_tanh(x):
    return 0.5 * x * (1.0 + jnp.tanh(0.7978845608028654 * (x + 0.044715 * x * x * x)))


def _compress_kernel(x_ref, w1b_ref, pos_ref, w1_ref, b1_ref, w2b_ref, o_ref):
    y = _dot(x_ref[...], w1b_ref[...])
    posc = _dot(pos_ref[...], w1_ref[...])[0:1, :] + b1_ref[...]
    n = y.shape[0]
    acts = []
    for g in range(N_KV_GROUPS):
        first = y[:, g * 2 * CMP_HIDDEN: g * 2 * CMP_HIDDEN + CMP_HIDDEN]
        second = y[:, g * 2 * CMP_HIDDEN + CMP_HIDDEN: (g + 1) * 2 * CMP_HIDDEN]
        pre = first + pltpu.roll(second, n - 1, 0) + posc
        acts.append(_gelu_tanh(pre).astype(BF))
    act = jnp.concatenate(acts, axis=1)
    o_ref[...] = _dot(act, w2b_ref[...]).astype(BF)


def _compress(xk, w1big, pos8, w1, b1, w2big):
    B, n, width = xk.shape
    full = lambda a: pl.BlockSpec(a.shape, lambda b: (0,) * a.ndim)
    return pl.pallas_call(
        _compress_kernel,
        grid=(B,),
        in_specs=[pl.BlockSpec((None, n, width), lambda b: (b, 0, 0)),
                  full(w1big), full(pos8), full(w1), full(b1), full(w2big)],
        out_specs=pl.BlockSpec((None, n, 2 * KV_WIDTH), lambda b: (b, 0, 0)),
        out_shape=jax.ShapeDtypeStruct((B, n, 2 * KV_WIDTH), BF),
        compiler_params=_params("arbitrary"),
        name="compress",
    )(xk, w1big, pos8, w1, b1, w2big)


def _masked_exp(s, mask):
    sm = jnp.where(mask, s, NEG)
    m = jnp.max(sm, axis=-1, keepdims=True)
    p = jnp.where(mask, jnp.exp(sm - m), 0.0)
    return p, jnp.sum(p, axis=-1, keepdims=True)


def _safe_inv(l):
    return jnp.where(l > 0.0, 1.0 / jnp.where(l > 0.0, l, 1.0), 0.0)


def _softmax_tile(s, m_old):
    m_new = jnp.maximum(m_old, jnp.max(s, axis=-1, keepdims=True))
    return m_new, jnp.exp2(s - m_new)


def _attn_kernel(q_ref, kc_ref, vc_ref, ks_ref, vs_ref, kw_ref, vw_ref, bg_ref, ovt_ref, o_ref):
    ci = pl.program_id(1)
    q0 = ci * Q_CHUNK
    Q, H, G = Q_CHUNK, HEADS_PER_GROUP, N_KV_GROUPS
    R = H * Q
    n_blk = ovt_ref.shape[0]
    units = [(bb, g) for bb in range(q_ref.shape[0]) for g in range(G)]
    U = len(units)
    sig = [jax.nn.sigmoid(bg_ref[bb]) for bb in range(q_ref.shape[0])]
    t_q = q0 + lax.broadcasted_iota(I32, (Q, 1), 0)
    t_r = jnp.concatenate([t_q] * H, axis=0)

    def rows4(a):
        return jnp.concatenate([a] * H, axis=0)

    def q_rows(bb, g):
        return jnp.concatenate(
            [q_ref[bb, :, (g * H + h) * LANES:(g * H + h + 1) * LANES] for h in range(H)], axis=0)

    gcs = [slice(g * LANES, (g + 1) * LANES) for g in range(G)]
    qp = [q_rows(bb, g) for bb, g in units]

    w0 = pl.multiple_of(jnp.maximum(q0 - WINDOW, 0), Q_CHUNK)
    s3 = [_dot_nt(qp[u], kw_ref[bb, pl.ds(w0, WINDOW + Q_CHUNK), gcs[g]])
          for u, (bb, g) in enumerate(units)]
    s1 = [_dot_nt(qp[u], kc_ref[bb, :, gcs[g]]) for u, (bb, g) in enumerate(units)]

    n_idx = lax.broadcasted_iota(I32, s1[0].shape, 1)
    m1 = (n_idx * CMP_STRIDE + (CMP_BLOCK - 1)) <= t_r
    o1, inv1, psums = [], [], []
    for u, (bb, g) in enumerate(units):
        p1, l1 = _masked_exp(s1[u], m1)
        iv = _safe_inv(l1)
        o1.append(_dot(p1.astype(BF), vc_ref[bb, :, gcs[g]]))
        inv1.append(iv)
        p1n = p1 * iv
        psum = p1n[0:Q]
        for h in range(1, H):
            psum = psum + p1n[h * Q:(h + 1) * Q]
        psums.append(psum)
    psum = jnp.concatenate(psums, axis=0)
    hi = psum.astype(BF)
    lo = (psum - hi.astype(F32)).astype(BF)
    ps_t = _dot_nt(ovt_ref[...], hi) + _dot_nt(ovt_ref[...], lo)

    kpos3 = w0 + lax.broadcasted_iota(I32, (Q, WINDOW + Q_CHUNK), 1)
    bias3 = rows4(jnp.where((kpos3 <= t_q) & (kpos3 > t_q - WINDOW), 0.0, NEG))
    win = []
    for u, (bb, g) in enumerate(units):
        _, p3 = _softmax_tile(s3[u] + bias3, jnp.full((R, 1), NEG, F32))
        win.append(_dot(p3.astype(BF), vw_ref[bb, pl.ds(w0, WINDOW + Q_CHUNK), gcs[g]]))

    lo_half = lax.broadcasted_iota(I32, (Q, LANES), 1) < HEAD_DIM

    def normalised(o):
        den = jnp.where(lo_half, pltpu.roll(o, HEAD_DIM, 1), 1.0)
        return o / den

    early, gate_sel = [], []
    for u, (bb, g) in enumerate(units):
        parts, gates = [], []
        for h in range(H):
            rows = slice(h * Q, (h + 1) * Q)
            col = g * H + h
            g1 = sig[bb][:, col:col + 1]
            g3 = sig[bb][:, 2 * N_HEADS + col:2 * N_HEADS + col + 1]
            parts.append((g1 * inv1[u][rows]) * o1[u][rows] + g3 * normalised(win[u][rows]))
            gates.append(jnp.broadcast_to(sig[bb][:, N_HEADS + col:N_HEADS + col + 1], (Q, LANES)))
        early.append(parts)
        gate_sel.append(gates)

    j = lax.broadcasted_iota(I32, ps_t.shape, 0)
    forced = (j == 0) | (j == ci) | (j == ci - 1)
    score = jnp.where(forced, FORCE_SCORE, jnp.where(j <= ci, ps_t, NEG))
    rank = jnp.zeros(ps_t.shape, I32)
    for jp in range(n_blk):
        c = score[jp:jp + 1, :]
        beats = (c > score) | ((c == score) & (j > jp))
        rank = rank + beats.astype(I32)
    bias_t = jnp.where(rank < N_SELECT, 0.0, NEG)
    pad_t = jnp.concatenate([jnp.zeros((HEAD_DIM, U * Q), F32), bias_t,
                             jnp.zeros((LANES - HEAD_DIM - n_blk, U * Q), F32)], axis=0)
    sel_bias = pad_t.T.astype(BF)

    qa = [qp[u] + rows4(sel_bias[u * Q:(u + 1) * Q]) for u in range(U)]

    def sel_tile(k0, carry, causal):
        s = [_dot_nt(qa[u], ks_ref[bb, pl.ds(k0, SEL_KEY_TILE), gcs[g]])
             for u, (bb, g) in enumerate(units)]
        if causal:
            kpos = k0 + lax.broadcasted_iota(I32, (Q, SEL_KEY_TILE), 1)
            bias = rows4(jnp.where(kpos <= t_q, 0.0, NEG))
            s = [su + bias for su in s]
        out = []
        for u, (bb, g) in enumerate(units):
            m, acc = carry[u]
            m_new, p = _softmax_tile(s[u], m)
            acc = jnp.exp2(m - m_new) * acc + _dot(
                p.astype(BF), vs_ref[bb, pl.ds(k0, SEL_KEY_TILE), gcs[g]])
            out.append((m_new, acc))
        return tuple(out)

    def sweep(n_tiles):
        def run():
            carry = tuple((jnp.full((R, 1), NEG, F32), jnp.zeros((R, LANES), F32))
                          for _ in range(U))
            for kt in range(n_tiles - 1):
                carry = sel_tile(kt * SEL_KEY_TILE, carry, False)
            return sel_tile((n_tiles - 1) * SEL_KEY_TILE, carry, True)
        return run

    blocks_per_tile = SEL_KEY_TILE // SEL_BLOCK
    max_tiles = n_blk // blocks_per_tile
    carry = lax.switch(ci // blocks_per_tile, [sweep(n) for n in range(1, max_tiles + 1)])

    for u, (bb, g) in enumerate(units):
        _, o2 = carry[u]
        heads = []
        for h in range(H):
            rows = slice(h * Q, (h + 1) * Q)
            heads.append(early[u][h] + gate_sel[u][h] * normalised(o2[rows]))
        for k in range(H // 2):
            slab = jnp.where(lo_half, heads[2 * k], pltpu.roll(heads[2 * k + 1], HEAD_DIM, 1))
            c0 = (g * (H // 2) + k) * LANES
            o_ref[bb, :, c0:c0 + LANES] = slab.astype(BF)


def _attention(q, kc, vc, ks, vs, kw, vw, bg, ovt):
    B, S, _ = q.shape
    nq = S // Q_CHUNK
    nb = ATTN_BATCH if B % ATTN_BATCH == 0 else 1
    per_b = lambda a: pl.BlockSpec((nb,) + a.shape[1:], lambda b, i: (b, 0, 0))
    full = lambda a: pl.BlockSpec(a.shape, lambda b, i: (0,) * a.ndim)
    return pl.pallas_call(
        _attn_kernel,
        grid=(B // nb, nq),
        in_specs=[pl.BlockSpec((nb, Q_CHUNK, N_HEADS * LANES), lambda b, i: (b, i, 0)),
                  per_b(kc), per_b(vc), per_b(ks), per_b(vs), per_b(kw), per_b(vw),
                  pl.BlockSpec((nb, Q_CHUNK, LANES), lambda b, i: (b, i, 0)),
                  full(ovt)],
        out_specs=pl.BlockSpec((nb, Q_CHUNK, ATTN_WIDTH), lambda b, i: (b, i, 0)),
        out_shape=jax.ShapeDtypeStruct((B, S, ATTN_WIDTH), BF),
        compiler_params=_params("arbitrary", "arbitrary"),
        name="nsa_attention",
    )(q, kc, vc, ks, vs, kw, vw, bg, ovt)


def _mixer_out_kernel(upc_ref, upp_ref, o_ref, gm_ref, x_ref, mod_ref, pw_ref, psc_ref,
                      wup_ref, wua_ref, wo_ref, g2_ref, wr_ref, br_ref, tri_ref,
                      x1_ref, h2t_ref, crk_ref, rw_ref, cnt_ref, cnt_scr):
    i = pl.program_id(1)

    @pl.when((pl.program_id(0) == 0) & (i == 0))
    def _():
        cnt_scr[...] = jnp.zeros_like(cnt_scr)

    tm = upc_ref.shape[0]
    prev = upp_ref[...] * (i > 0).astype(F32)
    ext = jnp.concatenate([prev, upc_ref[...]], axis=0)
    t = i * tm + lax.broadcasted_iota(I32, (tm, 1), 0)
    ys = []
    for gi, w in enumerate(POOL_WINDOWS):
        u = ext[:, gi * POOL_GROUP:(gi + 1) * POOL_GROUP]
        acc = u
        shift = 1
        while shift < w:
            acc = acc + pltpu.roll(acc, shift, 0)
            shift *= 2
        inv_cnt = 1.0 / jnp.minimum(t + 1, w).astype(F32)
        p = acc[POOL_HALO:] * inv_cnt - u[POOL_HALO:]
        ys.append(_dot(p.astype(BF), pw_ref[gi]))
    y = jnp.concatenate(ys, axis=1) * psc_ref[...]
    y_pool = _dot(y.astype(BF), wup_ref[...])
    y_attn = _dot(o_ref[...], wua_ref[...])
    gm = gm_ref[...].astype(F32)
    mix = gm[:, :D_MODEL] * y_pool + gm[:, D_MODEL:] * y_attn
    x1 = x_ref[...] + mod_ref[2:3, :] * _dot(mix.astype(BF), wo_ref[...])
    x1_ref[...] = x1
    r = lax.rsqrt(jnp.mean(x1 * x1, axis=-1, keepdims=True) + EPS)
    h2f = x1 * r * g2_ref[...] * (1.0 + mod_ref[4:5, :]) + mod_ref[3:4, :]
    h2 = h2f.astype(BF)
    for k in range(ROW_TILES):
        h2t_ref[pl.ds(k, tm, stride=ROW_TILES), :] = h2f[:, k * LANES:(k + 1) * LANES]

    lt = _dot_nt(wr_ref[...], h2) + br_ref[...]
    lg = lt[0:N_GROUPS]
    gmax = jnp.max(lg, axis=0, keepdims=True)
    gi_ = lax.broadcasted_iota(I32, lg.shape, 0)
    gidx = jnp.min(jnp.where(lg == gmax, gi_, N_GROUPS), axis=0, keepdims=True)
    gp = 1.0 / jnp.sum(jnp.exp(lg - gmax), axis=0, keepdims=True)
    E = EXPERTS_PER_GROUP
    le = jnp.zeros((E, tm), F32)
    for gg in range(N_GROUPS):
        le = jnp.where(gidx == gg, lt[8 + gg * E:8 + (gg + 1) * E], le)
    ei = lax.broadcasted_iota(I32, le.shape, 0)
    v1 = jnp.max(le, axis=0, keepdims=True)
    i1 = jnp.min(jnp.where(le == v1, ei, E), axis=0, keepdims=True)
    rest = jnp.where(ei == i1, -jnp.inf, le)
    v2 = jnp.max(rest, axis=0, keepdims=True)
    i2 = jnp.min(jnp.where(rest == v2, ei, E), axis=0, keepdims=True)
    e = jnp.exp(v2 - v1)
    wa = gp / (1.0 + e)
    wb = gp * e / (1.0 + e)
    lo = jnp.minimum(i1, i2)
    hi = jnp.maximum(i1, i2)
    pair = lax.shift_right_logical(lo * (2 * E - 1 - lo), 1) + hi - lo - 1
    cls = gidx * PAIRS_PER_GROUP + pair
    first_lo = i1 < i2
    w_lo = jnp.where(first_lo, wa, wb)
    w_hi = jnp.where(first_lo, wb, wa)
    oh = lax.broadcasted_iota(I32, (N_CLASS_PAD, tm), 0) == cls
    before = _dot(oh.astype(BF), tri_ref[...]) + cnt_scr[:, 0:1]
    rank = jnp.sum(jnp.where(oh, before, 0.0), axis=0, keepdims=True).astype(I32)
    cnt_scr[...] = cnt_scr[...] + jnp.sum(oh.astype(F32), axis=1, keepdims=True)
    cnt_ref[...] = cnt_scr[...]
    row = lax.broadcasted_iota(I32, (8, tm), 0)
    crk_ref[...] = jnp.where(row == 0, cls, jnp.where(row == 1, rank, 0))
    rw_ref[...] = jnp.where(row == 0, w_lo, jnp.where(row == 1, w_hi, 0.0))


def _mixer_out(up, o, gm, x, mod, pw, psc, wup, wua, wo, g2, wr, br):
    B, S, D = x.shape
    tm = ROUTE_TILE
    nt = S // tm
    tri = jnp.asarray(np.triu(np.ones((tm, tm), np.float32), k=1), BF)
    row = lambda n: pl.BlockSpec((None, tm, n), lambda b, i: (b, i, 0))
    full = lambda a: pl.BlockSpec(a.shape, lambda b, i: (0,) * a.ndim)
    per = tm // POOL_HALO
    return pl.pallas_call(
        _mixer_out_kernel,
        grid=(B, nt),
        in_specs=[row(POOL_WIDTH),
                  pl.BlockSpec((None, POOL_HALO, POOL_WIDTH),
                               lambda b, i: (b, jnp.maximum(i * per - 1, 0), 0)),
                  row(ATTN_WIDTH), row(2 * D_MODEL), row(D),
                  pl.BlockSpec((None, 6, D), lambda b, i: (b, 0, 0)),
                  full(pw), full(psc), full(wup), full(wua), full(wo), full(g2),
                  full(wr), full(br), full(tri)],
        out_specs=[row(D),
                   pl.BlockSpec((tm * ROW_TILES, LANES), lambda b, i: (b * nt + i, 0)),
                   pl.BlockSpec((None, 8, tm), lambda b, i: (b * nt + i, 0, 0)),
                   pl.BlockSpec((None, 8, tm), lambda b, i: (b * nt + i, 0, 0)),
                   pl.BlockSpec((N_CLASS_PAD, LANES), lambda b, i: (0, 0))],
        out_shape=[jax.ShapeDtypeStruct((B, S, D), F32),
                   jax.ShapeDtypeStruct((B * S * ROW_TILES, LANES), F32),
                   jax.ShapeDtypeStruct((B * nt, 8, tm), I32),
                   jax.ShapeDtypeStruct((B * nt, 8, tm), F32),
                   jax.ShapeDtypeStruct((N_CLASS_PAD, LANES), F32)],
        scratch_shapes=[pltpu.VMEM((N_CLASS_PAD, LANES), F32)],
        compiler_params=_params("arbitrary", "arbitrary"),
        name="mixer_out_router",
    )(up, up, o, gm, x, mod, pw, psc, wup, wua, wo, g2, wr, br, tri)


def _plan_kernel(crk_ref, cnt_ref, etab_ref, pos_ref, tmap_ref):
    C = N_CLASS_PAD
    cnt = cnt_ref[:, 0:1].astype(I32)
    ntile = lax.shift_right_logical(cnt + (EXPERT_TILE - 1), EXPERT_TILE_LOG2)
    ntile_f = ntile.astype(F32)
    r = lax.broadcasted_iota(I32, (C, C), 0)
    c = lax.broadcasted_iota(I32, (C, C), 1)
    lower = (c < r).astype(BF)
    first = _dot(lower, jnp.broadcast_to(ntile_f, (C, LANES)).astype(BF))[:, 0:1]
    last = first + ntile_f
    total = jnp.sum(ntile_f, axis=0, keepdims=True)
    off = (first * EXPERT_TILE).astype(I32)

    def body(i, carry):
        cls = crk_ref[i, 0:1, :]
        rank = crk_ref[i, 1:2, :]
        oh = lax.broadcasted_iota(I32, (C, cls.shape[1]), 0) == cls
        pos_ref[pl.ds(i, 1), :] = jnp.sum(jnp.where(oh, off, 0), axis=0, keepdims=True) + rank
        return carry

    lax.fori_loop(0, crk_ref.shape[0], body, 0)

    nj = tmap_ref.shape[1]
    j = lax.broadcasted_iota(I32, (1, nj), 1).astype(F32)
    jj = jnp.minimum(j, total - 1.0)
    tcls = jnp.sum((last <= jj).astype(I32), axis=0, keepdims=True)
    oh2 = lax.broadcasted_iota(I32, (C, nj), 0) == tcls
    elo = jnp.sum(jnp.where(oh2, etab_ref[:, 0:1], 0), axis=0, keepdims=True)
    ehi = jnp.sum(jnp.where(oh2, etab_ref[:, 1:2], 0), axis=0, keepdims=True)
    row = lax.broadcasted_iota(I32, (8, nj), 0)
    tmap_ref[...] = jnp.where(
        row == 0, elo, jnp.where(row == 1, ehi, jnp.where(
            row == 2, (j < total).astype(I32), jnp.where(row == 3, jj.astype(I32), 0))))


def _plan(crk, cnt, n_tiles_pad):
    nt, _, tm = crk.shape
    etab = np.zeros((N_CLASS_PAD, LANES), np.int32)
    cid = 0
    for g in range(N_GROUPS):
        for lo in range(EXPERTS_PER_GROUP):
            for hi in range(lo + 1, EXPERTS_PER_GROUP):
                etab[cid, 0] = g * EXPERTS_PER_GROUP + lo
                etab[cid, 1] = g * EXPERTS_PER_GROUP + hi
                cid += 1
    etab = jnp.asarray(etab)
    full = lambda a: pl.BlockSpec(a.shape, lambda i: (0,) * a.ndim)
    return pl.pallas_call(
        _plan_kernel,
        grid=(1,),
        in_specs=[full(crk), full(cnt), full(etab)],
        out_specs=[pl.BlockSpec((nt, tm), lambda i: (0, 0)),
                   pl.BlockSpec((8, n_tiles_pad), lambda i: (0, 0))],
        out_shape=[jax.ShapeDtypeStruct((nt, tm), I32),
                   jax.ShapeDtypeStruct((8, n_tiles_pad), I32)],
        compiler_params=_params("arbitrary"),
        name="moe_plan",
    )(crk, cnt, etab)


def _dispatch_kernel(pos_ref, h_ref, xs_in_ref, xs_ref, sem):
    del xs_in_ref
    i = pl.program_id(0)
    rows = h_ref.shape[0] // ROW_TILES

    def issue(r8, carry):
        for k in range(DMA_UNROLL):
            r = r8 * DMA_UNROLL + k
            p = pos_ref[i * rows + r]
            pltpu.make_async_copy(
                h_ref.at[pl.ds(pl.multiple_of(r * ROW_TILES, ROW_TILES), ROW_TILES)],
                xs_ref.at[pl.ds(pl.multiple_of(p * ROW_TILES, ROW_TILES), ROW_TILES)],
                sem).start()
        return carry

    lax.fori_loop(0, rows // DMA_UNROLL, issue, 0)
    pltpu.make_async_copy(h_ref, xs_ref.at[pl.ds(0, rows * ROW_TILES)], sem).wait()


def _dispatch(pos, h2t, n_sorted_rows):
    rows = DISPATCH_ROWS
    n = h2t.shape[0] // ROW_TILES
    zeros = jnp.zeros((n_sorted_rows * ROW_TILES, LANES), F32)
    return pl.pallas_call(
        _dispatch_kernel,
        grid_spec=pltpu.PrefetchScalarGridSpec(
            num_scalar_prefetch=1,
            grid=(n // rows,),
            in_specs=[pl.BlockSpec((rows * ROW_TILES, LANES), lambda i, p: (i, 0)),
                      pl.BlockSpec(memory_space=pl.ANY)],
            out_specs=pl.BlockSpec(memory_space=pl.ANY),
            scratch_shapes=[pltpu.SemaphoreType.DMA]),
        out_shape=jax.ShapeDtypeStruct(zeros.shape, F32),
        input_output_aliases={2: 0},
        compiler_params=_params("arbitrary"),
        name="moe_dispatch",
    )(pos, h2t, zeros)


def _expert_kernel(elo_ref, ehi_ref, valid_ref, blk_ref, x_ref, wg0, wu0, wd0, wg1, wu1, wd1, o_ref):
    del elo_ref, ehi_ref, blk_ref
    T = EXPERT_TILE
    valid = valid_ref[pl.program_id(0)] > 0

    @pl.when(jnp.logical_not(valid))
    def _():
        o_ref[...] = jnp.zeros_like(o_ref)

    @pl.when(valid)
    def _():
        x = jnp.concatenate([x_ref[pl.ds(k, T, stride=ROW_TILES), :] for k in range(ROW_TILES)],
                            axis=1).astype(BF)
        for half, (wg, wu, wd) in enumerate(((wg0, wu0, wd0), (wg1, wu1, wd1))):
            a = _dot(x, wg[...])
            b = _dot(x, wu[...])
            he = (a * jax.nn.sigmoid(a)) * b
            y = _dot(he.astype(BF), wd[...])
            for k in range(ROW_TILES):
                o_ref[pl.ds(half * ROW_TILES + k, T, stride=2 * ROW_TILES), :] = (
                    y[:, k * LANES:(k + 1) * LANES])


def _experts(tmap, xs, wg, wu, wd, n_tiles):
    T = EXPERT_TILE
    D, F = D_MODEL, EXPERT_FF
    lo = lambda shape: pl.BlockSpec((None,) + shape, lambda j, elo, ehi, v, blk: (elo[j], 0, 0))
    hi = lambda shape: pl.BlockSpec((None,) + shape, lambda j, elo, ehi, v, blk: (ehi[j], 0, 0))
    return pl.pallas_call(
        _expert_kernel,
        grid_spec=pltpu.PrefetchScalarGridSpec(
            num_scalar_prefetch=4,
            grid=(n_tiles,),
            in_specs=[pl.BlockSpec((T * ROW_TILES, LANES), lambda j, elo, ehi, v, blk: (blk[j], 0)),
                      lo((D, F)), lo((D, F)), lo((F, D)), hi((D, F)), hi((D, F)), hi((F, D))],
            out_specs=pl.BlockSpec((T * 2 * ROW_TILES, LANES), lambda j, elo, ehi, v, blk: (j, 0))),
        out_shape=jax.ShapeDtypeStruct((n_tiles * T * 2 * ROW_TILES, LANES), F32),
        compiler_params=_params("arbitrary"),
        name="moe_experts",
    )(tmap[0], tmap[1], tmap[2], tmap[3], xs, wg, wu, wd, wg, wu, wd)


def _combine_kernel(pos_ref, ys_ref, w_ref, x1_ref, mod_ref, fg_ref, o_ref, buf, sem):
    i = pl.program_id(0)
    tm = x1_ref.shape[0]
    R2 = 2 * ROW_TILES

    def gather(step, slot):
        def issue(r8, carry):
            for k in range(DMA_UNROLL):
                r = r8 * DMA_UNROLL + k
                p = pos_ref[step * tm + r]
                pltpu.make_async_copy(ys_ref.at[pl.ds(pl.multiple_of(p * R2, R2), R2)],
                                      buf.at[slot, pl.ds(pl.multiple_of(r * R2, R2), R2)],
                                      sem.at[slot]).start()
            return carry
        lax.fori_loop(0, tm // DMA_UNROLL, issue, 0)

    @pl.when(i == 0)
    def _():
        gather(0, 0)

    slot = i % 2

    @pl.when(i + 1 < pl.num_programs(0))
    def _():
        gather(i + 1, 1 - slot)

    pltpu.make_async_copy(ys_ref.at[pl.ds(0, tm * R2)], buf.at[slot], sem.at[slot]).wait()
    ylo = jnp.concatenate([buf[slot, pl.ds(k, tm, stride=R2), :] for k in range(ROW_TILES)], axis=1)
    yhi = jnp.concatenate([buf[slot, pl.ds(ROW_TILES + k, tm, stride=R2), :] for k in range(ROW_TILES)],
                          axis=1)
    y = w_ref[:, 0:1] * ylo + w_ref[:, 1:2] * yhi
    x2 = x1_ref[...] + mod_ref[5:6, :] * y
    r = lax.rsqrt(jnp.mean(x2 * x2, axis=-1, keepdims=True) + EPS)
    o_ref[...] = x2 * r * fg_ref[...]


def _combine(pos, ys, w, x1, mod, fg):
    N, D = x1.shape
    B = mod.shape[0]
    tm = COMBINE_ROWS
    per_b = (N // B) // tm
    return pl.pallas_call(
        _combine_kernel,
        grid_spec=pltpu.PrefetchScalarGridSpec(
            num_scalar_prefetch=1,
            grid=(N // tm,),
            in_specs=[pl.BlockSpec(memory_space=pl.ANY),
                      pl.BlockSpec((tm, 2), lambda i, p: (i, 0)),
                      pl.BlockSpec((tm, D), lambda i, p: (i, 0)),
                      pl.BlockSpec((None, 6, D), lambda i, p: (i // per_b, 0, 0)),
                      pl.BlockSpec((1, D), lambda i, p: (0, 0))],
            out_specs=pl.BlockSpec((tm, D), lambda i, p: (i, 0)),
            scratch_shapes=[pltpu.VMEM((2, tm * 2 * ROW_TILES, LANES), F32),
                            pltpu.SemaphoreType.DMA((2,))]),
        out_shape=jax.ShapeDtypeStruct((N, D), F32),
        compiler_params=_params("arbitrary"),
        name="moe_combine",
    )(pos, ys, w, x1, mod, fg)


def _arrange_w_in(w):
    gate0 = POOL_WIDTH + ATTN_WIDTH + 6 * KV_WIDTH
    gate1 = gate0 + N_BRANCH * N_HEADS
    pad = jnp.zeros((w.shape[0], LANES - N_BRANCH * N_HEADS), w.dtype)
    return jnp.concatenate([w[:, :gate1], pad, w[:, gate1:]], axis=1).astype(BF)


def _compress_weights(pos, w1, b1, w2, dup):
    second = jnp.asarray([1.0, 1.0 if dup else 0.0], F32)
    eye = jnp.eye(N_KV_GROUPS, dtype=F32)
    halves = CMP_BLOCK // CMP_STRIDE
    w1r = w1.reshape(halves, CMP_STRIDE, HEAD_DIM, CMP_HIDDEN)
    w1big = jnp.einsum('hidc,gk->igdkhc', w1r, eye).reshape(
        CMP_STRIDE * N_KV_GROUPS * HEAD_DIM, N_KV_GROUPS * halves * CMP_HIDDEN)
    w2big = jnp.einsum('cd,gk,r->gckrd', w2, eye, second).reshape(
        N_KV_GROUPS * CMP_HIDDEN, N_KV_GROUPS * 2 * HEAD_DIM)
    pos8 = jnp.broadcast_to(pos.reshape(1, CMP_BLOCK * HEAD_DIM), (8, CMP_BLOCK * HEAD_DIM))
    return (w1big.astype(BF), pos8.astype(BF), w1.astype(BF), b1.reshape(1, CMP_HIDDEN),
            w2big.astype(BF))


def _selection_tables(S):
    n_chunks = S // CMP_STRIDE
    n_cmp = n_chunks - CMP_BLOCK // CMP_STRIDE + 1
    n_blk = S // SEL_BLOCK
    s1 = np.arange(n_cmp)[:, None] * CMP_STRIDE
    s2 = np.arange(n_blk)[None, :] * SEL_BLOCK
    ovl = np.clip(np.minimum(s1 + CMP_BLOCK, s2 + SEL_BLOCK) - np.maximum(s1, s2), 0, None) / CMP_BLOCK
    ovt = np.zeros((n_blk, n_chunks), np.float32)
    ovt[:, :n_cmp] = ovl.T
    return jnp.asarray(ovt, BF)


def kernel(x, c, ada_w, ada_b, norm1_g, w_in, pool_w, pool_scale, cmp_pos, cmp_w1, cmp_b1, cmp_w2,
           w_up_pool, w_up_attn, w_out, norm2_g, router_g_w, router_g_b, router_e_w, router_e_b,
           exp_w_gate, exp_w_up, exp_w_down, final_g):
    B, S, D = x.shape
    N = B * S
    assert ada_w.shape[0] == 1, "the final norm is fused into the last layer's combine step"
    for l in range(ada_w.shape[0]):
        mod = _ada(c, ada_w[l], ada_b[l]).reshape(B, 6, D)
        (up, q, kc, vc, ks, vs, kw, vw, bg, gm) = _inproj(
            x, mod, norm1_g[l].reshape(1, D), _arrange_w_in(w_in[l]))
        n_chunks = S // CMP_STRIDE
        kcc = _compress(kc.reshape(B, n_chunks, CMP_STRIDE * KV_WIDTH),
                        *_compress_weights(cmp_pos[l, 0], cmp_w1[l, 0], cmp_b1[l, 0], cmp_w2[l, 0], False))
        vcc = _compress(vc.reshape(B, n_chunks, CMP_STRIDE * KV_WIDTH),
                        *_compress_weights(cmp_pos[l, 1], cmp_w1[l, 1], cmp_b1[l, 1], cmp_w2[l, 1], True))
        o = _attention(q, kcc, vcc, ks, vs, kw, vw, bg, _selection_tables(S))
        wr = jnp.zeros((8 + N_EXPERTS, D), F32)
        wr = wr.at[0:N_GROUPS].set(router_g_w[l].T).at[8:].set(router_e_w[l].T).astype(BF)
        br = jnp.zeros((8 + N_EXPERTS, 1), F32)
        br = br.at[0:N_GROUPS, 0].set(router_g_b[l]).at[8:, 0].set(router_e_b[l])
        x1, h2t, crk, rw, cnt = _mixer_out(
            up, o, gm, x, mod, pool_w[l].astype(BF), pool_scale[l].reshape(1, POOL_WIDTH),
            w_up_pool[l].astype(BF), w_up_attn[l].astype(BF), w_out[l].astype(BF),
            norm2_g[l].reshape(1, D), wr, br)
        n_tiles = N // EXPERT_TILE + N_CLASS
        n_tiles_pad = -(-n_tiles // LANES) * LANES
        pos, tmap = _plan(crk, cnt, n_tiles_pad)
        pos = pos.reshape(N)
        xs = _dispatch(pos, h2t, n_tiles * EXPERT_TILE)
        ys = _experts(tmap, xs, exp_w_gate[l].astype(BF), exp_w_up[l].astype(BF),
                      exp_w_down[l].astype(BF), n_tiles)
        w = jnp.stack([rw[:, 0, :].reshape(N), rw[:, 1, :].reshape(N)], axis=1)
        y = _combine(pos, ys, w, x1.reshape(N, D), mod, final_g.reshape(1, D))
        x = y.reshape(B, S, D)
    return x
```

```python
import functools

import numpy as np
import jax
import jax.numpy as jnp
from jax import lax
from jax.experimental import pallas as pl
from jax.experimental.pallas import tpu as pltpu

BF = jnp.bfloat16
F32 = jnp.float32
I32 = jnp.int32

D_MODEL = 1024
POOL_WIDTH = 512
POOL_WINDOWS = (2, 4, 8, 16)
POOL_GROUP = 128
POOL_HALO = 16
N_HEADS = 8
HEAD_DIM = 64
N_KV_GROUPS = 2
HEADS_PER_GROUP = 4
ATTN_WIDTH = 512
KV_WIDTH = 128
CMP_BLOCK = 32
CMP_STRIDE = 16
CMP_HIDDEN = 256
SEL_BLOCK = 64
N_SELECT = 8
WINDOW = 512
Q_CHUNK = 64
N_BRANCH = 3
N_GROUPS = 4
EXPERTS_PER_GROUP = 8
N_EXPERTS = 32
EXPERT_FF = 512
EPS = 1e-6
NEG = -1e30
FORCE_SCORE = 1e4
QK_SCALE = HEAD_DIM ** -0.5
LOG2E = 1.4426950408889634

PAIRS_PER_GROUP = EXPERTS_PER_GROUP * (EXPERTS_PER_GROUP - 1) // 2
N_CLASS = N_GROUPS * PAIRS_PER_GROUP
N_CLASS_PAD = 128
ROUTE_TILE = 512
EXPERT_TILE_LOG2 = 8
EXPERT_TILE = 1 << EXPERT_TILE_LOG2
DISPATCH_ROWS = 1024
COMBINE_ROWS = 256
DMA_UNROLL = 8

LANES = 128
ROW_TILES = D_MODEL // LANES
SEL_KEY_TILE = 512
ATTN_BATCH = 2
DEN_ROWS = 16
VMEM_LIMIT = 56 * 1024 * 1024

C_POOL = 0
C_Q = C_POOL + POOL_WIDTH
C_KC = C_Q + ATTN_WIDTH
C_VC = C_KC + KV_WIDTH
C_KS = C_VC + KV_WIDTH
C_KW = C_KS + KV_WIDTH
C_BG = C_KW + KV_WIDTH
C_MG = C_BG + LANES
C_END = C_MG + 2 * D_MODEL


def _dot(a, b):
    return jnp.dot(a, b, preferred_element_type=F32)


def _dot_nt(a, b):
    return lax.dot_general(a, b, (((1,), (1,)), ((), ())), preferred_element_type=F32)


def _params(*sem):
    return pltpu.CompilerParams(dimension_semantics=sem, vmem_limit_bytes=VMEM_LIMIT)


def _ada_kernel(c_ref, w_ref, b_ref, o_ref):
    o_ref[...] = _dot(c_ref[...].astype(BF), w_ref[...].astype(BF)) + b_ref[...]


def _ada(c, w, b):
    B, D = c.shape
    n = w.shape[1]
    tn = 1024
    return pl.pallas_call(
        _ada_kernel,
        grid=(n // tn,),
        in_specs=[pl.BlockSpec((B, D), lambda j: (0, 0)),
                  pl.BlockSpec((D, tn), lambda j: (0, j)),
                  pl.BlockSpec((1, tn), lambda j: (0, j))],
        out_specs=pl.BlockSpec((B, tn), lambda j: (0, j)),
        out_shape=jax.ShapeDtypeStruct((B, n), F32),
        compiler_params=_params("arbitrary"),
        name="ada_mod",
    )(c, w, b.reshape(1, n))


def _inproj_kernel(x_ref, mod_ref, g_ref, w_ref, wvt_ref, blk_ref, up_ref, q_ref, kc_ref, vc_ref,
                   ks_ref, kw_ref, bg_ref, gm_ref, vst_ref, vwt_ref):
    x = x_ref[...]
    r = lax.rsqrt(jnp.mean(x * x, axis=-1, keepdims=True) + EPS)
    h = x * r * g_ref[...] * (1.0 + mod_ref[1:2, :]) + mod_ref[0:1, :]
    hb = h.astype(BF)

    def proj(a, b):
        return _dot(hb, w_ref[:, a:b])

    def spread(v, fill):
        pad = jnp.full((v.shape[0], HEAD_DIM), fill, v.dtype)
        pieces = []
        for i in range(v.shape[1] // HEAD_DIM):
            pieces += [v[:, i * HEAD_DIM:(i + 1) * HEAD_DIM], pad]
        return jnp.concatenate(pieces, axis=1)

    up_ref[...] = proj(C_POOL, C_Q)
    q_ref[...] = spread(proj(C_Q, C_KC) * QK_SCALE, 0.0).astype(BF)
    kc_ref[...] = proj(C_KC, C_VC).astype(BF)
    vc_ref[...] = proj(C_VC, C_KS).astype(BF)
    ks_ref[...] = (spread(proj(C_KS, C_KW) * LOG2E, 0.0) + blk_ref[...]).astype(BF)
    kw_ref[...] = spread(proj(C_KW, C_BG) * LOG2E, 0.0).astype(BF)
    vt = _dot_nt(wvt_ref[...], hb)
    vst_ref[...] = vt[:KV_WIDTH].astype(BF)
    vwt_ref[...] = vt[KV_WIDTH:].astype(BF)
    bg_ref[...] = proj(C_BG, C_MG)
    gm_ref[...] = jax.nn.sigmoid(proj(C_MG, C_END)).astype(BF)


def _inproj(x, mod, g, w, wvt):
    B, S, D = x.shape
    tm = 512
    blk = np.zeros((S, 2 * LANES), np.float32)
    for gg in range(N_KV_GROUPS):
        blk[np.arange(S), gg * LANES + HEAD_DIM + np.arange(S) // SEL_BLOCK] = 1.0
    blk = jnp.asarray(blk)
    widths = [(POOL_WIDTH, F32), (N_HEADS * LANES, BF), (KV_WIDTH, BF), (KV_WIDTH, BF),
              (2 * KV_WIDTH, BF), (2 * KV_WIDTH, BF), (LANES, F32), (2 * D_MODEL, BF)]
    row = lambda n: pl.BlockSpec((None, tm, n), lambda b, i: (b, i, 0))
    col = pl.BlockSpec((None, KV_WIDTH, tm), lambda b, i: (b, 0, i))
    return pl.pallas_call(
        _inproj_kernel,
        grid=(B, S // tm),
        in_specs=[row(D),
                  pl.BlockSpec((None, 6, D), lambda b, i: (b, 0, 0)),
                  pl.BlockSpec((1, D), lambda b, i: (0, 0)),
                  pl.BlockSpec((D, C_END), lambda b, i: (0, 0)),
                  pl.BlockSpec((2 * KV_WIDTH, D), lambda b, i: (0, 0)),
                  pl.BlockSpec((tm, 2 * LANES), lambda b, i: (i, 0))],
        out_specs=[row(n) for n, _ in widths] + [col, col],
        out_shape=[jax.ShapeDtypeStruct((B, S, n), dt) for n, dt in widths]
        + [jax.ShapeDtypeStruct((B, KV_WIDTH, S), BF)] * 2,
        compiler_params=_params("arbitrary", "arbitrary"),
        name="norm1_inproj",
    )(x, mod, g, w, wvt, blk)


def _gelu_tanh(x):
    return 0.5 * x * (1.0 + jnp.tanh(0.7978845608028654 * (x + 0.044715 * x * x * x)))


def _compress_kernel(transposed, x_ref, w1b_ref, pos_ref, w1_ref, b1_ref, w2b_ref, o_ref):
    y = _dot(x_ref[...], w1b_ref[...])
    posc = _dot(pos_ref[...], w1_ref[...])[0:1, :] + b1_ref[...]
    n = y.shape[0]
    acts = []
    for g in range(N_KV_GROUPS):
        first = y[:, g * 2 * CMP_HIDDEN: g * 2 * CMP_HIDDEN + CMP_HIDDEN]
        second = y[:, g * 2 * CMP_HIDDEN + CMP_HIDDEN: (g + 1) * 2 * CMP_HIDDEN]
        pre = first + pltpu.roll(second, n - 1, 0) + posc
        acts.append(_gelu_tanh(pre).astype(BF))
    act = jnp.concatenate(acts, axis=1)
    if transposed:
        o_ref[...] = _dot_nt(w2b_ref[...], act).astype(BF)
    else:
        o_ref[...] = _dot(act, w2b_ref[...]).astype(BF)


def _compress(xk, w1big, pos8, w1, b1, w2big, transposed):
    B, n, width = xk.shape
    full = lambda a: pl.BlockSpec(a.shape, lambda b: (0,) * a.ndim)
    out = (KV_WIDTH, n) if transposed else (n, 2 * KV_WIDTH)
    return pl.pallas_call(
        functools.partial(_compress_kernel, transposed),
        grid=(B,),
        in_specs=[pl.BlockSpec((None, n, width), lambda b: (b, 0, 0)),
                  full(w1big), full(pos8), full(w1), full(b1), full(w2big)],
        out_specs=pl.BlockSpec((None,) + out, lambda b: (b, 0, 0)),
        out_shape=jax.ShapeDtypeStruct((B,) + out, BF),
        compiler_params=_params("arbitrary"),
        name="compress",
    )(xk, w1big, pos8, w1, b1, w2big)


def _masked_exp(s, mask):
    sm = jnp.where(mask, s, NEG)
    m = jnp.max(sm, axis=-1, keepdims=True)
    p = jnp.where(mask, jnp.exp(sm - m), 0.0)
    return p, jnp.sum(p, axis=-1, keepdims=True)


def _safe_inv(l):
    return jnp.where(l > 0.0, 1.0 / jnp.where(l > 0.0, l, 1.0), 0.0)


def _softmax_tile(s, m_old):
    m_new = jnp.maximum(m_old, jnp.max(s, axis=-1, keepdims=True))
    return m_new, jnp.exp2(s - m_new)


def _attn_kernel(q_ref, kc_ref, vct_ref, ks_ref, vst_ref, kw_ref, vwt_ref, bg_ref, ovt_ref, o_ref):
    ci = pl.program_id(1)
    q0 = ci * Q_CHUNK
    Q, H, G = Q_CHUNK, HEADS_PER_GROUP, N_KV_GROUPS
    R = H * Q
    n_blk = ovt_ref.shape[0]
    units = [(bb, g) for bb in range(q_ref.shape[0]) for g in range(G)]
    U = len(units)
    sig = [jax.nn.sigmoid(bg_ref[bb]) for bb in range(q_ref.shape[0])]
    t_q = q0 + lax.broadcasted_iota(I32, (Q, 1), 0)
    t_r = jnp.concatenate([t_q] * H, axis=0)

    def rows4(a):
        return jnp.concatenate([a] * H, axis=0)

    def q_rows(bb, g):
        return jnp.concatenate(
            [q_ref[bb, :, (g * H + h) * LANES:(g * H + h + 1) * LANES] for h in range(H)], axis=0)

    gcs = [slice(g * LANES, (g + 1) * LANES) for g in range(G)]
    grs = [slice(g * HEAD_DIM, (g + 1) * HEAD_DIM) for g in range(G)]
    qp = [q_rows(bb, g) for bb, g in units]

    def pv_t(vt, p):
        vt1 = jnp.concatenate([vt, jnp.ones((DEN_ROWS, vt.shape[1]), BF)], axis=0)
        return _dot_nt(vt1, p.astype(BF))

    w0 = pl.multiple_of((jnp.maximum(q0 - WINDOW, 0) // LANES) * LANES, LANES)
    wkeys = WINDOW + 2 * Q_CHUNK
    s3 = [_dot_nt(qp[u], kw_ref[bb, pl.ds(w0, wkeys), gcs[g]]) for u, (bb, g) in enumerate(units)]
    s1 = [_dot_nt(qp[u], kc_ref[bb, :, gcs[g]]) for u, (bb, g) in enumerate(units)]

    n_idx = lax.broadcasted_iota(I32, s1[0].shape, 1)
    m1 = (n_idx * CMP_STRIDE + (CMP_BLOCK - 1)) <= t_r
    o1, psums = [], []
    for u, (bb, g) in enumerate(units):
        p1, l1 = _masked_exp(s1[u], m1)
        iv = _safe_inv(l1)
        o1.append(pv_t(vct_ref[bb, grs[g], :], p1))
        p1n = p1 * iv
        psum = p1n[0:Q]
        for h in range(1, H):
            psum = psum + p1n[h * Q:(h + 1) * Q]
        psums.append(psum)
    psum = jnp.concatenate(psums, axis=0)
    hi = psum.astype(BF)
    lo = (psum - hi.astype(F32)).astype(BF)
    ps_t = _dot_nt(ovt_ref[...], hi) + _dot_nt(ovt_ref[...], lo)

    kpos3 = w0 + lax.broadcasted_iota(I32, (Q, wkeys), 1)
    bias3 = rows4(jnp.where((kpos3 <= t_q) & (kpos3 > t_q - WINDOW), 0.0, NEG))
    win = []
    for u, (bb, g) in enumerate(units):
        _, p3 = _softmax_tile(s3[u] + bias3, jnp.full((R, 1), NEG, F32))
        win.append(pv_t(vwt_ref[bb, grs[g], pl.ds(w0, wkeys)], p3))

    sig_t = [jnp.concatenate([s, s], axis=0).T for s in sig]
    lane_lo = lax.broadcasted_iota(I32, (1, LANES), 1) < HEAD_DIM

    def gate_row(bb, g, branch):
        rows = [sig_t[bb][branch * N_HEADS + g * H + h:branch * N_HEADS + g * H + h + 1, :]
                for h in range(H)]
        return jnp.concatenate([jnp.where(lane_lo, rows[2 * k], rows[2 * k + 1])
                                for k in range(H // 2)], axis=1)

    def normalised(ot, may_be_empty):
        den = ot[HEAD_DIM:HEAD_DIM + 1, :]
        return ot[:HEAD_DIM] * (_safe_inv(den) if may_be_empty else 1.0 / den)

    early = [gate_row(bb, g, 0) * normalised(o1[u], True)
             + gate_row(bb, g, 2) * normalised(win[u], False)
             for u, (bb, g) in enumerate(units)]
    gate_sel = [gate_row(bb, g, 1) for bb, g in units]

    j = lax.broadcasted_iota(I32, ps_t.shape, 0)
    forced = (j == 0) | (j == ci) | (j == ci - 1)
    score = jnp.where(forced, FORCE_SCORE, jnp.where(j <= ci, ps_t, NEG))
    rank = jnp.zeros(ps_t.shape, I32)
    for jp in range(n_blk):
        c = score[jp:jp + 1, :]
        beats = (c > score) | ((c == score) & (j > jp))
        rank = rank + beats.astype(I32)
    bias_t = jnp.where(rank < N_SELECT, 0.0, NEG)
    pad_t = jnp.concatenate([jnp.zeros((HEAD_DIM, U * Q), F32), bias_t,
                             jnp.zeros((LANES - HEAD_DIM - n_blk, U * Q), F32)], axis=0)
    sel_bias = pad_t.T.astype(BF)

    qa = [qp[u] + rows4(sel_bias[u * Q:(u + 1) * Q]) for u in range(U)]

    def sweep(n_tiles):
        past = (n_tiles - 1) * SEL_KEY_TILE
        keys = n_tiles * SEL_KEY_TILE

        def run():
            kpos = past + lax.broadcasted_iota(I32, (Q, SEL_KEY_TILE), 1)
            bias = rows4(jnp.where(kpos <= t_q, 0.0, NEG))
            s = [_dot_nt(qa[u], ks_ref[bb, 0:keys, gcs[g]]) for u, (bb, g) in enumerate(units)]
            out = []
            for u, (bb, g) in enumerate(units):
                s_last = s[u][:, past:] + bias
                m = jnp.max(s_last, axis=-1, keepdims=True)
                if past:
                    m = jnp.maximum(m, jnp.max(s[u][:, :past], axis=-1, keepdims=True))
                acc = pv_t(vst_ref[bb, grs[g], past:keys], jnp.exp2(s_last - m))
                if past:
                    acc = acc + pv_t(vst_ref[bb, grs[g], 0:past], jnp.exp2(s[u][:, :past] - m))
                out.append(acc)
            return tuple(out)
        return run

    blocks_per_tile = SEL_KEY_TILE // SEL_BLOCK
    max_tiles = n_blk // blocks_per_tile
    sel = lax.switch(ci // blocks_per_tile, [sweep(n) for n in range(1, max_tiles + 1)])

    for u, (bb, g) in enumerate(units):
        out = (early[u] + gate_sel[u] * normalised(sel[u], False)).T
        for k in range(H // 2):
            slab = jnp.concatenate([out[(2 * k) * Q:(2 * k + 1) * Q],
                                    out[(2 * k + 1) * Q:(2 * k + 2) * Q]], axis=1)
            c0 = (g * (H // 2) + k) * LANES
            o_ref[bb, :, c0:c0 + LANES] = slab.astype(BF)


def _attention(q, kc, vc, ks, vs, kw, vw, bg, ovt):
    B, S, _ = q.shape
    nq = S // Q_CHUNK
    nb = ATTN_BATCH if B % ATTN_BATCH == 0 else 1
    per_b = lambda a: pl.BlockSpec((nb,) + a.shape[1:], lambda b, i: (b, 0, 0))
    full = lambda a: pl.BlockSpec(a.shape, lambda b, i: (0,) * a.ndim)
    return pl.pallas_call(
        _attn_kernel,
        grid=(B // nb, nq),
        in_specs=[pl.BlockSpec((nb, Q_CHUNK, N_HEADS * LANES), lambda b, i: (b, i, 0)),
                  per_b(kc), per_b(vc), per_b(ks), per_b(vs), per_b(kw), per_b(vw),
                  pl.BlockSpec((nb, Q_CHUNK, LANES), lambda b, i: (b, i, 0)),
                  full(ovt)],
        out_specs=pl.BlockSpec((nb, Q_CHUNK, ATTN_WIDTH), lambda b, i: (b, i, 0)),
        out_shape=jax.ShapeDtypeStruct((B, S, ATTN_WIDTH), BF),
        compiler_params=_params("arbitrary", "arbitrary"),
        name="nsa_attention",
    )(q, kc, vc, ks, vs, kw, vw, bg, ovt)


def _mixer_out_kernel(upc_ref, upp_ref, o_ref, gm_ref, x_ref, mod_ref, pw_ref, psc_ref,
                      wup_ref, wua_ref, wo_ref, g2_ref, wr_ref, br_ref, tri_ref,
                      x1_ref, h2t_ref, crk_ref, rw_ref, cnt_ref, cnt_scr):
    i = pl.program_id(1)

    @pl.when((pl.program_id(0) == 0) & (i == 0))
    def _():
        cnt_scr[...] = jnp.zeros_like(cnt_scr)

    tm = upc_ref.shape[0]
    prev = upp_ref[...] * (i > 0).astype(F32)
    ext = jnp.concatenate([prev, upc_ref[...]], axis=0)
    t = i * tm + lax.broadcasted_iota(I32, (tm, 1), 0)
    ys = []
    for gi, w in enumerate(POOL_WINDOWS):
        u = ext[:, gi * POOL_GROUP:(gi + 1) * POOL_GROUP]
        acc = u
        shift = 1
        while shift < w:
            acc = acc + pltpu.roll(acc, shift, 0)
            shift *= 2
        inv_cnt = 1.0 / jnp.minimum(t + 1, w).astype(F32)
        p = acc[POOL_HALO:] * inv_cnt - u[POOL_HALO:]
        ys.append(_dot(p.astype(BF), pw_ref[gi]))
    y = jnp.concatenate(ys, axis=1) * psc_ref[...]
    y_pool = _dot(y.astype(BF), wup_ref[...])
    y_attn = _dot(o_ref[...], wua_ref[...])
    gm = gm_ref[...].astype(F32)
    mix = gm[:, :D_MODEL] * y_pool + gm[:, D_MODEL:] * y_attn
    x1 = x_ref[...] + mod_ref[2:3, :] * _dot(mix.astype(BF), wo_ref[...])
    x1_ref[...] = x1
    r = lax.rsqrt(jnp.mean(x1 * x1, axis=-1, keepdims=True) + EPS)
    h2f = x1 * r * g2_ref[...] * (1.0 + mod_ref[4:5, :]) + mod_ref[3:4, :]
    h2 = h2f.astype(BF)
    for k in range(ROW_TILES):
        h2t_ref[pl.ds(k, tm, stride=ROW_TILES), :] = h2f[:, k * LANES:(k + 1) * LANES]

    lt = _dot_nt(wr_ref[...], h2) + br_ref[...]
    lg = lt[0:N_GROUPS]
    gmax = jnp.max(lg, axis=0, keepdims=True)
    gi_ = lax.broadcasted_iota(I32, lg.shape, 0)
    gidx = jnp.min(jnp.where(lg == gmax, gi_, N_GROUPS), axis=0, keepdims=True)
    gp = 1.0 / jnp.sum(jnp.exp(lg - gmax), axis=0, keepdims=True)
    E = EXPERTS_PER_GROUP
    le = jnp.zeros((E, tm), F32)
    for gg in range(N_GROUPS):
        le = jnp.where(gidx == gg, lt[8 + gg * E:8 + (gg + 1) * E], le)
    ei = lax.broadcasted_iota(I32, le.shape, 0)
    v1 = jnp.max(le, axis=0, keepdims=True)
    i1 = jnp.min(jnp.where(le == v1, ei, E), axis=0, keepdims=True)
    rest = jnp.where(ei == i1, -jnp.inf, le)
    v2 = jnp.max(rest, axis=0, keepdims=True)
    i2 = jnp.min(jnp.where(rest == v2, ei, E), axis=0, keepdims=True)
    e = jnp.exp(v2 - v1)
    wa = gp / (1.0 + e)
    wb = gp * e / (1.0 + e)
    lo = jnp.minimum(i1, i2)
    hi = jnp.maximum(i1, i2)
    pair = lax.shift_right_logical(lo * (2 * E - 1 - lo), 1) + hi - lo - 1
    cls = gidx * PAIRS_PER_GROUP + pair
    first_lo = i1 < i2
    w_lo = jnp.where(first_lo, wa, wb)
    w_hi = jnp.where(first_lo, wb, wa)
    oh = lax.broadcasted_iota(I32, (N_CLASS_PAD, tm), 0) == cls
    before = _dot(oh.astype(BF), tri_ref[...]) + cnt_scr[:, 0:1]
    rank = jnp.sum(jnp.where(oh, before, 0.0), axis=0, keepdims=True).astype(I32)
    cnt_scr[...] = cnt_scr[...] + jnp.sum(oh.astype(F32), axis=1, keepdims=True)
    cnt_ref[...] = cnt_scr[...]
    row = lax.broadcasted_iota(I32, (8, tm), 0)
    crk_ref[...] = jnp.where(row == 0, cls, jnp.where(row == 1, rank, 0))
    rw_ref[...] = jnp.where(row == 0, w_lo, jnp.where(row == 1, w_hi, 0.0))


def _mixer_out(up, o, gm, x, mod, pw, psc, wup, wua, wo, g2, wr, br):
    B, S, D = x.shape
    tm = ROUTE_TILE
    nt = S // tm
    tri = jnp.asarray(np.triu(np.ones((tm, tm), np.float32), k=1), BF)
    row = lambda n: pl.BlockSpec((None, tm, n), lambda b, i: (b, i, 0))
    full = lambda a: pl.BlockSpec(a.shape, lambda b, i: (0,) * a.ndim)
    per = tm // POOL_HALO
    return pl.pallas_call(
        _mixer_out_kernel,
        grid=(B, nt),
        in_specs=[row(POOL_WIDTH),
                  pl.BlockSpec((None, POOL_HALO, POOL_WIDTH),
                               lambda b, i: (b, jnp.maximum(i * per - 1, 0), 0)),
                  row(ATTN_WIDTH), row(2 * D_MODEL), row(D),
                  pl.BlockSpec((None, 6, D), lambda b, i: (b, 0, 0)),
                  full(pw), full(psc), full(wup), full(wua), full(wo), full(g2),
                  full(wr), full(br), full(tri)],
        out_specs=[row(D),
                   pl.BlockSpec((tm * ROW_TILES, LANES), lambda b, i: (b * nt + i, 0)),
                   pl.BlockSpec((None, 8, tm), lambda b, i: (b * nt + i, 0, 0)),
                   pl.BlockSpec((None, 8, tm), lambda b, i: (b * nt + i, 0, 0)),
                   pl.BlockSpec((N_CLASS_PAD, LANES), lambda b, i: (0, 0))],
        out_shape=[jax.ShapeDtypeStruct((B, S, D), F32),
                   jax.ShapeDtypeStruct((B * S * ROW_TILES, LANES), F32),
                   jax.ShapeDtypeStruct((B * nt, 8, tm), I32),
                   jax.ShapeDtypeStruct((B * nt, 8, tm), F32),
                   jax.ShapeDtypeStruct((N_CLASS_PAD, LANES), F32)],
        scratch_shapes=[pltpu.VMEM((N_CLASS_PAD, LANES), F32)],
        compiler_params=_params("arbitrary", "arbitrary"),
        name="mixer_out_router",
    )(up, up, o, gm, x, mod, pw, psc, wup, wua, wo, g2, wr, br, tri)


def _plan_kernel(crk_ref, cnt_ref, etab_ref, pos_ref, tmap_ref):
    C = N_CLASS_PAD
    cnt = cnt_ref[:, 0:1].astype(I32)
    ntile = lax.shift_right_logical(cnt + (EXPERT_TILE - 1), EXPERT_TILE_LOG2)
    ntile_f = ntile.astype(F32)
    r = lax.broadcasted_iota(I32, (C, C), 0)
    c = lax.broadcasted_iota(I32, (C, C), 1)
    lower = (c < r).astype(BF)
    first = _dot(lower, jnp.broadcast_to(ntile_f, (C, LANES)).astype(BF))[:, 0:1]
    last = first + ntile_f
    total = jnp.sum(ntile_f, axis=0, keepdims=True)
    off = (first * EXPERT_TILE).astype(I32)

    def body(i, carry):
        cls = crk_ref[i, 0:1, :]
        rank = crk_ref[i, 1:2, :]
        oh = lax.broadcasted_iota(I32, (C, cls.shape[1]), 0) == cls
        pos_ref[pl.ds(i, 1), :] = jnp.sum(jnp.where(oh, off, 0), axis=0, keepdims=True) + rank
        return carry

    lax.fori_loop(0, crk_ref.shape[0], body, 0)

    nj = tmap_ref.shape[1]
    j = lax.broadcasted_iota(I32, (1, nj), 1).astype(F32)
    jj = jnp.minimum(j, total - 1.0)
    tcls = jnp.sum((last <= jj).astype(I32), axis=0, keepdims=True)
    oh2 = lax.broadcasted_iota(I32, (C, nj), 0) == tcls
    elo = jnp.sum(jnp.where(oh2, etab_ref[:, 0:1], 0), axis=0, keepdims=True)
    ehi = jnp.sum(jnp.where(oh2, etab_ref[:, 1:2], 0), axis=0, keepdims=True)
    row = lax.broadcasted_iota(I32, (8, nj), 0)
    tmap_ref[...] = jnp.where(
        row == 0, elo, jnp.where(row == 1, ehi, jnp.where(
            row == 2, (j < total).astype(I32), jnp.where(row == 3, jj.astype(I32), 0))))


def _plan(crk, cnt, n_tiles_pad):
    nt, _, tm = crk.shape
    etab = np.zeros((N_CLASS_PAD, LANES), np.int32)
    cid = 0
    for g in range(N_GROUPS):
        for lo in range(EXPERTS_PER_GROUP):
            for hi in range(lo + 1, EXPERTS_PER_GROUP):
                etab[cid, 0] = g * EXPERTS_PER_GROUP + lo
                etab[cid, 1] = g * EXPERTS_PER_GROUP + hi
                cid += 1
    etab = jnp.asarray(etab)
    full = lambda a: pl.BlockSpec(a.shape, lambda i: (0,) * a.ndim)
    return pl.pallas_call(
        _plan_kernel,
        grid=(1,),
        in_specs=[full(crk), full(cnt), full(etab)],
        out_specs=[pl.BlockSpec((nt, tm), lambda i: (0, 0)),
                   pl.BlockSpec((8, n_tiles_pad), lambda i: (0, 0))],
        out_shape=[jax.ShapeDtypeStruct((nt, tm), I32),
                   jax.ShapeDtypeStruct((8, n_tiles_pad), I32)],
        compiler_params=_params("arbitrary"),
        name="moe_plan",
    )(crk, cnt, etab)


def _dispatch_kernel(pos_ref, h_ref, xs_in_ref, xs_ref, sem):
    del xs_in_ref
    i = pl.program_id(0)
    rows = h_ref.shape[0] // ROW_TILES

    def issue(r8, carry):
        for k in range(DMA_UNROLL):
            r = r8 * DMA_UNROLL + k
            p = pos_ref[i * rows + r]
            pltpu.make_async_copy(
                h_ref.at[pl.ds(pl.multiple_of(r * ROW_TILES, ROW_TILES), ROW_TILES)],
                xs_ref.at[pl.ds(pl.multiple_of(p * ROW_TILES, ROW_TILES), ROW_TILES)],
                sem).start()
        return carry

    lax.fori_loop(0, rows // DMA_UNROLL, issue, 0)
    pltpu.make_async_copy(h_ref, xs_ref.at[pl.ds(0, rows * ROW_TILES)], sem).wait()


def _dispatch(pos, h2t, n_sorted_rows):
    rows = DISPATCH_ROWS
    n = h2t.shape[0] // ROW_TILES
    zeros = jnp.zeros((n_sorted_rows * ROW_TILES, LANES), F32)
    return pl.pallas_call(
        _dispatch_kernel,
        grid_spec=pltpu.PrefetchScalarGridSpec(
            num_scalar_prefetch=1,
            grid=(n // rows,),
            in_specs=[pl.BlockSpec((rows * ROW_TILES, LANES), lambda i, p: (i, 0)),
                      pl.BlockSpec(memory_space=pl.ANY)],
            out_specs=pl.BlockSpec(memory_space=pl.ANY),
            scratch_shapes=[pltpu.SemaphoreType.DMA]),
        out_shape=jax.ShapeDtypeStruct(zeros.shape, F32),
        input_output_aliases={2: 0},
        compiler_params=_params("arbitrary"),
        name="moe_dispatch",
    )(pos, h2t, zeros)


def _expert_kernel(elo_ref, ehi_ref, valid_ref, blk_ref, x_ref, wg0, wu0, wd0, wg1, wu1, wd1, o_ref):
    del elo_ref, ehi_ref, blk_ref
    T = EXPERT_TILE
    valid = valid_ref[pl.program_id(0)] > 0

    @pl.when(jnp.logical_not(valid))
    def _():
        o_ref[...] = jnp.zeros_like(o_ref)

    @pl.when(valid)
    def _():
        x = jnp.concatenate([x_ref[pl.ds(k, T, stride=ROW_TILES), :] for k in range(ROW_TILES)],
                            axis=1).astype(BF)
        for half, (wg, wu, wd) in enumerate(((wg0, wu0, wd0), (wg1, wu1, wd1))):
            a = _dot(x, wg[...])
            b = _dot(x, wu[...])
            he = (a * jax.nn.sigmoid(a)) * b
            y = _dot(he.astype(BF), wd[...])
            for k in range(ROW_TILES):
                o_ref[pl.ds(half * ROW_TILES + k, T, stride=2 * ROW_TILES), :] = (
                    y[:, k * LANES:(k + 1) * LANES])


def _experts(tmap, xs, wg, wu, wd, n_tiles):
    T = EXPERT_TILE
    D, F = D_MODEL, EXPERT_FF
    lo = lambda shape: pl.BlockSpec((None,) + shape, lambda j, elo, ehi, v, blk: (elo[j], 0, 0))
    hi = lambda shape: pl.BlockSpec((None,) + shape, lambda j, elo, ehi, v, blk: (ehi[j], 0, 0))
    return pl.pallas_call(
        _expert_kernel,
        grid_spec=pltpu.PrefetchScalarGridSpec(
            num_scalar_prefetch=4,
            grid=(n_tiles,),
            in_specs=[pl.BlockSpec((T * ROW_TILES, LANES), lambda j, elo, ehi, v, blk: (blk[j], 0)),
                      lo((D, F)), lo((D, F)), lo((F, D)), hi((D, F)), hi((D, F)), hi((F, D))],
            out_specs=pl.BlockSpec((T * 2 * ROW_TILES, LANES), lambda j, elo, ehi, v, blk: (j, 0))),
        out_shape=jax.ShapeDtypeStruct((n_tiles * T * 2 * ROW_TILES, LANES), F32),
        compiler_params=_params("arbitrary"),
        name="moe_experts",
    )(tmap[0], tmap[1], tmap[2], tmap[3], xs, wg, wu, wd, wg, wu, wd)


def _combine_kernel(pos_ref, ys_ref, w_ref, x1_ref, mod_ref, fg_ref, o_ref, buf, sem):
    i = pl.program_id(0)
    tm = x1_ref.shape[0]
    R2 = 2 * ROW_TILES

    def gather(step, slot):
        def issue(r8, carry):
            for k in range(DMA_UNROLL):
                r = r8 * DMA_UNROLL + k
                p = pos_ref[step * tm + r]
                pltpu.make_async_copy(ys_ref.at[pl.ds(pl.multiple_of(p * R2, R2), R2)],
                                      buf.at[slot, pl.ds(pl.multiple_of(r * R2, R2), R2)],
                                      sem.at[slot]).start()
            return carry
        lax.fori_loop(0, tm // DMA_UNROLL, issue, 0)

    @pl.when(i == 0)
    def _():
        gather(0, 0)

    slot = i % 2

    @pl.when(i + 1 < pl.num_programs(0))
    def _():
        gather(i + 1, 1 - slot)

    pltpu.make_async_copy(ys_ref.at[pl.ds(0, tm * R2)], buf.at[slot], sem.at[slot]).wait()
    ylo = jnp.concatenate([buf[slot, pl.ds(k, tm, stride=R2), :] for k in range(ROW_TILES)], axis=1)
    yhi = jnp.concatenate([buf[slot, pl.ds(ROW_TILES + k, tm, stride=R2), :] for k in range(ROW_TILES)],
                          axis=1)
    y = w_ref[:, 0:1] * ylo + w_ref[:, 1:2] * yhi
    x2 = x1_ref[...] + mod_ref[5:6, :] * y
    r = lax.rsqrt(jnp.mean(x2 * x2, axis=-1, keepdims=True) + EPS)
    o_ref[...] = x2 * r * fg_ref[...]


def _combine(pos, ys, w, x1, mod, fg):
    N, D = x1.shape
    B = mod.shape[0]
    tm = COMBINE_ROWS
    per_b = (N // B) // tm
    return pl.pallas_call(
        _combine_kernel,
        grid_spec=pltpu.PrefetchScalarGridSpec(
            num_scalar_prefetch=1,
            grid=(N // tm,),
            in_specs=[pl.BlockSpec(memory_space=pl.ANY),
                      pl.BlockSpec((tm, 2), lambda i, p: (i, 0)),
                      pl.BlockSpec((tm, D), lambda i, p: (i, 0)),
                      pl.BlockSpec((None, 6, D), lambda i, p: (i // per_b, 0, 0)),
                      pl.BlockSpec((1, D), lambda i, p: (0, 0))],
            out_specs=pl.BlockSpec((tm, D), lambda i, p: (i, 0)),
            scratch_shapes=[pltpu.VMEM((2, tm * 2 * ROW_TILES, LANES), F32),
                            pltpu.SemaphoreType.DMA((2,))]),
        out_shape=jax.ShapeDtypeStruct((N, D), F32),
        compiler_params=_params("arbitrary"),
        name="moe_combine",
    )(pos, ys, w, x1, mod, fg)


def _arrange_w_in(w):
    cuts = np.cumsum([0, POOL_WIDTH, ATTN_WIDTH] + [KV_WIDTH] * 6 + [N_BRANCH * N_HEADS, 2 * D_MODEL])
    pool, q, kc, vc, ks, vs, kw, vw, bg, mg = [
        w[:, int(cuts[i]):int(cuts[i + 1])] for i in range(len(cuts) - 1)]
    bg = jnp.pad(bg, ((0, 0), (0, LANES - bg.shape[1])))
    main = jnp.concatenate([pool, q, kc, vc, ks, kw, bg, mg], axis=1).astype(BF)
    return main, jnp.concatenate([vs, vw], axis=1).T.astype(BF)


def _compress_weights(pos, w1, b1, w2, transposed):
    eye = jnp.eye(N_KV_GROUPS, dtype=F32)
    halves = CMP_BLOCK // CMP_STRIDE
    w1r = w1.reshape(halves, CMP_STRIDE, HEAD_DIM, CMP_HIDDEN)
    w1big = jnp.einsum('hidc,gk->igdkhc', w1r, eye).reshape(
        CMP_STRIDE * N_KV_GROUPS * HEAD_DIM, N_KV_GROUPS * halves * CMP_HIDDEN)
    if transposed:
        w2big = jnp.einsum('cd,gk->kdgc', w2, eye).reshape(
            N_KV_GROUPS * HEAD_DIM, N_KV_GROUPS * CMP_HIDDEN)
    else:
        w2big = jnp.einsum('cd,gk,r->gckrd', w2, eye, jnp.asarray([1.0, 0.0], F32)).reshape(
            N_KV_GROUPS * CMP_HIDDEN, N_KV_GROUPS * 2 * HEAD_DIM)
    pos8 = jnp.broadcast_to(pos.reshape(1, CMP_BLOCK * HEAD_DIM), (8, CMP_BLOCK * HEAD_DIM))
    return (w1big.astype(BF), pos8.astype(BF), w1.astype(BF), b1.reshape(1, CMP_HIDDEN),
            w2big.astype(BF))


def _selection_tables(S):
    n_chunks = S // CMP_STRIDE
    n_cmp = n_chunks - CMP_BLOCK // CMP_STRIDE + 1
    n_blk = S // SEL_BLOCK
    s1 = np.arange(n_cmp)[:, None] * CMP_STRIDE
    s2 = np.arange(n_blk)[None, :] * SEL_BLOCK
    ovl = np.clip(np.minimum(s1 + CMP_BLOCK, s2 + SEL_BLOCK) - np.maximum(s1, s2), 0, None) / CMP_BLOCK
    ovt = np.zeros((n_blk, n_chunks), np.float32)
    ovt[:, :n_cmp] = ovl.T
    return jnp.asarray(ovt, BF)


def kernel(x, c, ada_w, ada_b, norm1_g, w_in, pool_w, pool_scale, cmp_pos, cmp_w1, cmp_b1, cmp_w2,
           w_up_pool, w_up_attn, w_out, norm2_g, router_g_w, router_g_b, router_e_w, router_e_b,
           exp_w_gate, exp_w_up, exp_w_down, final_g):
    B, S, D = x.shape
    N = B * S
    assert ada_w.shape[0] == 1, "the final norm is fused into the last layer's combine step"
    for l in range(ada_w.shape[0]):
        mod = _ada(c, ada_w[l], ada_b[l]).reshape(B, 6, D)
        (up, q, kc, vc, ks, kw, bg, gm, vst, vwt) = _inproj(
            x, mod, norm1_g[l].reshape(1, D), *_arrange_w_in(w_in[l]))
        n_chunks = S // CMP_STRIDE
        kcc = _compress(kc.reshape(B, n_chunks, CMP_STRIDE * KV_WIDTH),
                        *_compress_weights(cmp_pos[l, 0], cmp_w1[l, 0], cmp_b1[l, 0], cmp_w2[l, 0], False),
                        False)
        vct = _compress(vc.reshape(B, n_chunks, CMP_STRIDE * KV_WIDTH),
                        *_compress_weights(cmp_pos[l, 1], cmp_w1[l, 1], cmp_b1[l, 1], cmp_w2[l, 1], True),
                        True)
        o = _attention(q, kcc, vct, ks, vst, kw, vwt, bg, _selection_tables(S))
        wr = jnp.zeros((8 + N_EXPERTS, D), F32)
        wr = wr.at[0:N_GROUPS].set(router_g_w[l].T).at[8:].set(router_e_w[l].T).astype(BF)
        br = jnp.zeros((8 + N_EXPERTS, 1), F32)
        br = br.at[0:N_GROUPS, 0].set(router_g_b[l]).at[8:, 0].set(router_e_b[l])
        x1, h2t, crk, rw, cnt = _mixer_out(
            up, o, gm, x, mod, pool_w[l].astype(BF), pool_scale[l].reshape(1, POOL_WIDTH),
            w_up_pool[l].astype(BF), w_up_attn[l].astype(BF), w_out[l].astype(BF),
            norm2_g[l].reshape(1, D), wr, br)
        n_tiles = N // EXPERT_TILE + N_CLASS
        n_tiles_pad = -(-n_tiles // LANES) * LANES
        pos, tmap = _plan(crk, cnt, n_tiles_pad)
        pos = pos.reshape(N)
        xs = _dispatch(pos, h2t, n_tiles * EXPERT_TILE)
        ys = _experts(tmap, xs, exp_w_gate[l].astype(BF), exp_w_up[l].astype(BF),
                      exp_w_down[l].astype(BF), n_tiles)
        w = jnp.stack([rw[:, 0, :].reshape(N), rw[:, 1, :].reshape(N)], axis=1)
        y = _combine(pos, ys, w, x1.reshape(N, D), mod, final_g.reshape(1, D))
        x = y.reshape(B, S, D)
    return x
```

```python
import functools

import numpy as np
import jax
import jax.numpy as jnp
from jax import lax
from jax.experimental import pallas as pl
from jax.experimental.pallas import tpu as pltpu

BF = jnp.bfloat16
F32 = jnp.float32
I32 = jnp.int32

D_MODEL = 1024
POOL_WIDTH = 512
POOL_WINDOWS = (2, 4, 8, 16)
POOL_GROUP = 128
POOL_HALO = 16
N_HEADS = 8
HEAD_DIM = 64
N_KV_GROUPS = 2
HEADS_PER_GROUP = 4
ATTN_WIDTH = 512
KV_WIDTH = 128
CMP_BLOCK = 32
CMP_STRIDE = 16
CMP_HIDDEN = 256
SEL_BLOCK = 64
N_SELECT = 8
WINDOW = 512
Q_CHUNK = 64
N_BRANCH = 3
N_GROUPS = 4
EXPERTS_PER_GROUP = 8
N_EXPERTS = 32
EXPERT_FF = 512
EPS = 1e-6
NEG = -1e30
FORCE_SCORE = 1e4
QK_SCALE = HEAD_DIM ** -0.5
LOG2E = 1.4426950408889634

PAIRS_PER_GROUP = EXPERTS_PER_GROUP * (EXPERTS_PER_GROUP - 1) // 2
N_CLASS = N_GROUPS * PAIRS_PER_GROUP
N_CLASS_PAD = 128
ROUTE_TILE = 512
EXPERT_TILE_LOG2 = 8
EXPERT_TILE = 1 << EXPERT_TILE_LOG2
DISPATCH_ROWS = 2048
COMBINE_ROWS = 256
DMA_UNROLL = 8

LANES = 128
ROW_TILES = D_MODEL // LANES
SEL_KEY_TILE = 512
ATTN_BATCH = 2
DEN_ROWS = 16
VMEM_LIMIT = 56 * 1024 * 1024

C_POOL = 0
C_Q = C_POOL + POOL_WIDTH
C_KC = C_Q + ATTN_WIDTH
C_VC = C_KC + KV_WIDTH
C_KS = C_VC + KV_WIDTH
C_KW = C_KS + KV_WIDTH
C_BG = C_KW + KV_WIDTH
C_MG = C_BG + LANES
C_END = C_MG + 2 * D_MODEL


def _dot(a, b):
    return jnp.dot(a, b, preferred_element_type=F32)


def _dot_nt(a, b):
    return lax.dot_general(a, b, (((1,), (1,)), ((), ())), preferred_element_type=F32)


def _params(*sem):
    return pltpu.CompilerParams(dimension_semantics=sem, vmem_limit_bytes=VMEM_LIMIT)


def _ada_kernel(c_ref, w_ref, b_ref, o_ref):
    o_ref[...] = _dot(c_ref[...].astype(BF), w_ref[...].astype(BF)) + b_ref[...]


def _ada(c, w, b):
    B, D = c.shape
    n = w.shape[1]
    tn = 1024
    return pl.pallas_call(
        _ada_kernel,
        grid=(n // tn,),
        in_specs=[pl.BlockSpec((B, D), lambda j: (0, 0)),
                  pl.BlockSpec((D, tn), lambda j: (0, j)),
                  pl.BlockSpec((1, tn), lambda j: (0, j))],
        out_specs=pl.BlockSpec((B, tn), lambda j: (0, j)),
        out_shape=jax.ShapeDtypeStruct((B, n), F32),
        compiler_params=_params("arbitrary"),
        name="ada_mod",
    )(c, w, b.reshape(1, n))


def _inproj_kernel(x_ref, mod_ref, g_ref, w_ref, wvt_ref, blk_ref, up_ref, q_ref, kc_ref, vc_ref,
                   ks_ref, kw_ref, bg_ref, gm_ref, vst_ref, vwt_ref):
    x = x_ref[...]
    r = lax.rsqrt(jnp.mean(x * x, axis=-1, keepdims=True) + EPS)
    h = x * r * g_ref[...] * (1.0 + mod_ref[1:2, :]) + mod_ref[0:1, :]
    hb = h.astype(BF)

    def proj(a, b):
        return _dot(hb, w_ref[:, a:b])

    def spread(v, fill):
        pad = jnp.full((v.shape[0], HEAD_DIM), fill, v.dtype)
        pieces = []
        for i in range(v.shape[1] // HEAD_DIM):
            pieces += [v[:, i * HEAD_DIM:(i + 1) * HEAD_DIM], pad]
        return jnp.concatenate(pieces, axis=1)

    up_ref[...] = proj(C_POOL, C_Q)
    q_ref[...] = spread(proj(C_Q, C_KC) * QK_SCALE, 0.0).astype(BF)
    kc_ref[...] = proj(C_KC, C_VC).astype(BF)
    vc_ref[...] = proj(C_VC, C_KS).astype(BF)
    ks_ref[...] = (spread(proj(C_KS, C_KW) * LOG2E, 0.0) + blk_ref[...]).astype(BF)
    kw_ref[...] = spread(proj(C_KW, C_BG) * LOG2E, 0.0).astype(BF)
    vt = _dot_nt(wvt_ref[...], hb)
    vst_ref[...] = vt[:KV_WIDTH].astype(BF)
    vwt_ref[...] = vt[KV_WIDTH:].astype(BF)
    bg_ref[...] = proj(C_BG, C_MG)
    gm_ref[...] = jax.nn.sigmoid(proj(C_MG, C_END)).astype(BF)


def _inproj(x, mod, g, w, wvt):
    B, S, D = x.shape
    tm = 512
    blk = np.zeros((S, 2 * LANES), np.float32)
    for gg in range(N_KV_GROUPS):
        blk[np.arange(S), gg * LANES + HEAD_DIM + np.arange(S) // SEL_BLOCK] = 1.0
    blk = jnp.asarray(blk)
    widths = [(POOL_WIDTH, F32), (N_HEADS * LANES, BF), (KV_WIDTH, BF), (KV_WIDTH, BF),
              (2 * KV_WIDTH, BF), (2 * KV_WIDTH, BF), (LANES, F32), (2 * D_MODEL, BF)]
    row = lambda n: pl.BlockSpec((None, tm, n), lambda b, i: (b, i, 0))
    col = pl.BlockSpec((None, KV_WIDTH, tm), lambda b, i: (b, 0, i))
    return pl.pallas_call(
        _inproj_kernel,
        grid=(B, S // tm),
        in_specs=[row(D),
                  pl.BlockSpec((None, 6, D), lambda b, i: (b, 0, 0)),
                  pl.BlockSpec((1, D), lambda b, i: (0, 0)),
                  pl.BlockSpec((D, C_END), lambda b, i: (0, 0)),
                  pl.BlockSpec((2 * KV_WIDTH, D), lambda b, i: (0, 0)),
                  pl.BlockSpec((tm, 2 * LANES), lambda b, i: (i, 0))],
        out_specs=[row(n) for n, _ in widths] + [col, col],
        out_shape=[jax.ShapeDtypeStruct((B, S, n), dt) for n, dt in widths]
        + [jax.ShapeDtypeStruct((B, KV_WIDTH, S), BF)] * 2,
        compiler_params=_params("arbitrary", "arbitrary"),
        name="norm1_inproj",
    )(x, mod, g, w, wvt, blk)


def _gelu_tanh(x):
    return 0.5 * x * (1.0 + jnp.tanh(0.7978845608028654 * (x + 0.044715 * x * x * x)))


def _compress_kernel(transposed, x_ref, w1b_ref, pos_ref, w1_ref, b1_ref, w2b_ref, o_ref):
    y = _dot(x_ref[...], w1b_ref[...])
    posc = _dot(pos_ref[...], w1_ref[...])[0:1, :] + b1_ref[...]
    n = y.shape[0]
    acts = []
    for g in range(N_KV_GROUPS):
        first = y[:, g * 2 * CMP_HIDDEN: g * 2 * CMP_HIDDEN + CMP_HIDDEN]
        second = y[:, g * 2 * CMP_HIDDEN + CMP_HIDDEN: (g + 1) * 2 * CMP_HIDDEN]
        pre = first + pltpu.roll(second, n - 1, 0) + posc
        acts.append(_gelu_tanh(pre).astype(BF))
    act = jnp.concatenate(acts, axis=1)
    if transposed:
        o_ref[...] = _dot_nt(w2b_ref[...], act).astype(BF)
    else:
        o_ref[...] = _dot(act, w2b_ref[...]).astype(BF)


def _compress(xk, w1big, pos8, w1, b1, w2big, transposed):
    B, n, width = xk.shape
    full = lambda a: pl.BlockSpec(a.shape, lambda b: (0,) * a.ndim)
    out = (KV_WIDTH, n) if transposed else (n, 2 * KV_WIDTH)
    return pl.pallas_call(
        functools.partial(_compress_kernel, transposed),
        grid=(B,),
        in_specs=[pl.BlockSpec((None, n, width), lambda b: (b, 0, 0)),
                  full(w1big), full(pos8), full(w1), full(b1), full(w2big)],
        out_specs=pl.BlockSpec((None,) + out, lambda b: (b, 0, 0)),
        out_shape=jax.ShapeDtypeStruct((B,) + out, BF),
        compiler_params=_params("arbitrary"),
        name="compress",
    )(xk, w1big, pos8, w1, b1, w2big)


def _masked_exp(s, mask):
    sm = jnp.where(mask, s, NEG)
    m = jnp.max(sm, axis=-1, keepdims=True)
    p = jnp.where(mask, jnp.exp(sm - m), 0.0)
    return p, jnp.sum(p, axis=-1, keepdims=True)


def _safe_inv(l):
    return jnp.where(l > 0.0, 1.0 / jnp.where(l > 0.0, l, 1.0), 0.0)


def _softmax_tile(s, m_old):
    m_new = jnp.maximum(m_old, jnp.max(s, axis=-1, keepdims=True))
    return m_new, jnp.exp2(s - m_new)


def _attn_kernel(q_ref, kc_ref, vct_ref, ks_ref, vst_ref, kw_ref, vwt_ref, bg_ref, ovt_ref, o_ref):
    ci = pl.program_id(1)
    q0 = ci * Q_CHUNK
    Q, H, G = Q_CHUNK, HEADS_PER_GROUP, N_KV_GROUPS
    R = H * Q
    n_blk = ovt_ref.shape[0]
    units = [(bb, g) for bb in range(q_ref.shape[0]) for g in range(G)]
    U = len(units)
    sig = [jax.nn.sigmoid(bg_ref[bb]) for bb in range(q_ref.shape[0])]
    t_q = q0 + lax.broadcasted_iota(I32, (Q, 1), 0)
    t_r = jnp.concatenate([t_q] * H, axis=0)

    def rows4(a):
        return jnp.concatenate([a] * H, axis=0)

    def q_rows(bb, g):
        return jnp.concatenate(
            [q_ref[bb, :, (g * H + h) * LANES:(g * H + h + 1) * LANES] for h in range(H)], axis=0)

    gcs = [slice(g * LANES, (g + 1) * LANES) for g in range(G)]
    grs = [slice(g * HEAD_DIM, (g + 1) * HEAD_DIM) for g in range(G)]
    qp = [q_rows(bb, g) for bb, g in units]

    def pv_t(vt, p):
        vt1 = jnp.concatenate([vt, jnp.ones((DEN_ROWS, vt.shape[1]), BF)], axis=0)
        return _dot_nt(vt1, p.astype(BF))

    w0 = pl.multiple_of((jnp.maximum(q0 - WINDOW, 0) // LANES) * LANES, LANES)
    wkeys = WINDOW + 2 * Q_CHUNK
    s3 = [_dot_nt(qp[u], kw_ref[bb, pl.ds(w0, wkeys), gcs[g]]) for u, (bb, g) in enumerate(units)]
    s1 = [_dot_nt(qp[u], kc_ref[bb, :, gcs[g]]) for u, (bb, g) in enumerate(units)]

    n_idx = lax.broadcasted_iota(I32, s1[0].shape, 1)
    m1 = (n_idx * CMP_STRIDE + (CMP_BLOCK - 1)) <= t_r
    o1, psums = [], []
    for u, (bb, g) in enumerate(units):
        p1, l1 = _masked_exp(s1[u], m1)
        iv = _safe_inv(l1)
        o1.append(pv_t(vct_ref[bb, grs[g], :], p1))
        p1n = p1 * iv
        psum = p1n[0:Q]
        for h in range(1, H):
            psum = psum + p1n[h * Q:(h + 1) * Q]
        psums.append(psum)
    psum = jnp.concatenate(psums, axis=0)
    hi = psum.astype(BF)
    lo = (psum - hi.astype(F32)).astype(BF)
    ps_t = _dot_nt(ovt_ref[...], hi) + _dot_nt(ovt_ref[...], lo)

    kpos3 = w0 + lax.broadcasted_iota(I32, (Q, wkeys), 1)
    bias3 = rows4(jnp.where((kpos3 <= t_q) & (kpos3 > t_q - WINDOW), 0.0, NEG))
    win = []
    for u, (bb, g) in enumerate(units):
        _, p3 = _softmax_tile(s3[u] + bias3, jnp.full((R, 1), NEG, F32))
        win.append(pv_t(vwt_ref[bb, grs[g], pl.ds(w0, wkeys)], p3))

    sig_t = [jnp.concatenate([s, s], axis=0).T for s in sig]
    lane_lo = lax.broadcasted_iota(I32, (1, LANES), 1) < HEAD_DIM

    def gate_row(bb, g, branch):
        rows = [sig_t[bb][branch * N_HEADS + g * H + h:branch * N_HEADS + g * H + h + 1, :]
                for h in range(H)]
        return jnp.concatenate([jnp.where(lane_lo, rows[2 * k], rows[2 * k + 1])
                                for k in range(H // 2)], axis=1)

    def normalised(ot, may_be_empty):
        den = ot[HEAD_DIM:HEAD_DIM + 1, :]
        return ot[:HEAD_DIM] * (_safe_inv(den) if may_be_empty else 1.0 / den)

    early = [gate_row(bb, g, 0) * normalised(o1[u], True)
             + gate_row(bb, g, 2) * normalised(win[u], False)
             for u, (bb, g) in enumerate(units)]
    gate_sel = [gate_row(bb, g, 1) for bb, g in units]

    j = lax.broadcasted_iota(I32, ps_t.shape, 0)
    forced = (j == 0) | (j == ci) | (j == ci - 1)
    score = jnp.where(forced, FORCE_SCORE, jnp.where(j <= ci, ps_t, NEG))
    rank = jnp.zeros(ps_t.shape, I32)
    for jp in range(n_blk):
        c = score[jp:jp + 1, :]
        beats = (c > score) | ((c == score) & (j > jp))
        rank = rank + beats.astype(I32)
    bias_t = jnp.where(rank < N_SELECT, 0.0, NEG)
    pad_t = jnp.concatenate([jnp.zeros((HEAD_DIM, U * Q), F32), bias_t,
                             jnp.zeros((LANES - HEAD_DIM - n_blk, U * Q), F32)], axis=0)
    sel_bias = pad_t.T.astype(BF)

    qa = [qp[u] + rows4(sel_bias[u * Q:(u + 1) * Q]) for u in range(U)]

    def sweep(n_tiles):
        past = (n_tiles - 1) * SEL_KEY_TILE
        keys = n_tiles * SEL_KEY_TILE

        def run():
            kpos = past + lax.broadcasted_iota(I32, (Q, SEL_KEY_TILE), 1)
            bias = rows4(jnp.where(kpos <= t_q, 0.0, NEG))
            s = [_dot_nt(qa[u], ks_ref[bb, 0:keys, gcs[g]]) for u, (bb, g) in enumerate(units)]
            out = []
            for u, (bb, g) in enumerate(units):
                s_last = s[u][:, past:] + bias
                m = jnp.max(s_last, axis=-1, keepdims=True)
                if past:
                    m = jnp.maximum(m, jnp.max(s[u][:, :past], axis=-1, keepdims=True))
                acc = pv_t(vst_ref[bb, grs[g], past:keys], jnp.exp2(s_last - m))
                if past:
                    acc = acc + pv_t(vst_ref[bb, grs[g], 0:past], jnp.exp2(s[u][:, :past] - m))
                out.append(acc)
            return tuple(out)
        return run

    blocks_per_tile = SEL_KEY_TILE // SEL_BLOCK
    max_tiles = n_blk // blocks_per_tile
    sel = lax.switch(ci // blocks_per_tile, [sweep(n) for n in range(1, max_tiles + 1)])

    for u, (bb, g) in enumerate(units):
        out = (early[u] + gate_sel[u] * normalised(sel[u], False)).T
        for k in range(H // 2):
            slab = jnp.concatenate([out[(2 * k) * Q:(2 * k + 1) * Q],
                                    out[(2 * k + 1) * Q:(2 * k + 2) * Q]], axis=1)
            c0 = (g * (H // 2) + k) * LANES
            o_ref[bb, :, c0:c0 + LANES] = slab.astype(BF)


def _attention(q, kc, vc, ks, vs, kw, vw, bg, ovt):
    B, S, _ = q.shape
    nq = S // Q_CHUNK
    nb = ATTN_BATCH if B % ATTN_BATCH == 0 else 1
    per_b = lambda a: pl.BlockSpec((nb,) + a.shape[1:], lambda b, i: (b, 0, 0))
    full = lambda a: pl.BlockSpec(a.shape, lambda b, i: (0,) * a.ndim)
    return pl.pallas_call(
        _attn_kernel,
        grid=(B // nb, nq),
        in_specs=[pl.BlockSpec((nb, Q_CHUNK, N_HEADS * LANES), lambda b, i: (b, i, 0)),
                  per_b(kc), per_b(vc), per_b(ks), per_b(vs), per_b(kw), per_b(vw),
                  pl.BlockSpec((nb, Q_CHUNK, LANES), lambda b, i: (b, i, 0)),
                  full(ovt)],
        out_specs=pl.BlockSpec((nb, Q_CHUNK, ATTN_WIDTH), lambda b, i: (b, i, 0)),
        out_shape=jax.ShapeDtypeStruct((B, S, ATTN_WIDTH), BF),
        compiler_params=_params("arbitrary", "arbitrary"),
        name="nsa_attention",
    )(q, kc, vc, ks, vs, kw, vw, bg, ovt)


def _mixer_out_kernel(upc_ref, upp_ref, o_ref, gm_ref, x_ref, mod_ref, pw_ref, psc_ref,
                      wup_ref, wua_ref, wo_ref, g2_ref, wr_ref, br_ref, tri_ref,
                      x1_ref, h2t_ref, crk_ref, rw_ref, cnt_ref, cnt_scr):
    i = pl.program_id(1)

    @pl.when((pl.program_id(0) == 0) & (i == 0))
    def _():
        cnt_scr[...] = jnp.zeros_like(cnt_scr)

    tm = upc_ref.shape[0]
    prev = upp_ref[...] * (i > 0).astype(F32)
    ext = jnp.concatenate([prev, upc_ref[...]], axis=0)
    t = i * tm + lax.broadcasted_iota(I32, (tm, 1), 0)
    ys = []
    for gi, w in enumerate(POOL_WINDOWS):
        u = ext[:, gi * POOL_GROUP:(gi + 1) * POOL_GROUP]
        acc = u
        shift = 1
        while shift < w:
            acc = acc + pltpu.roll(acc, shift, 0)
            shift *= 2
        inv_cnt = 1.0 / jnp.minimum(t + 1, w).astype(F32)
        p = acc[POOL_HALO:] * inv_cnt - u[POOL_HALO:]
        ys.append(_dot(p.astype(BF), pw_ref[gi]))
    y = jnp.concatenate(ys, axis=1) * psc_ref[...]
    y_pool = _dot(y.astype(BF), wup_ref[...])
    y_attn = _dot(o_ref[...], wua_ref[...])
    gm = gm_ref[...].astype(F32)
    mix = gm[:, :D_MODEL] * y_pool + gm[:, D_MODEL:] * y_attn
    x1 = x_ref[...] + mod_ref[2:3, :] * _dot(mix.astype(BF), wo_ref[...])
    x1_ref[...] = x1
    r = lax.rsqrt(jnp.mean(x1 * x1, axis=-1, keepdims=True) + EPS)
    h2f = x1 * r * g2_ref[...] * (1.0 + mod_ref[4:5, :]) + mod_ref[3:4, :]
    h2 = h2f.astype(BF)
    for k in range(ROW_TILES):
        h2t_ref[pl.ds(k, tm, stride=ROW_TILES), :] = h2f[:, k * LANES:(k + 1) * LANES]

    lt = _dot_nt(wr_ref[...], h2) + br_ref[...]
    lg = lt[0:N_GROUPS]
    gmax = jnp.max(lg, axis=0, keepdims=True)
    gi_ = lax.broadcasted_iota(I32, lg.shape, 0)
    gidx = jnp.min(jnp.where(lg == gmax, gi_, N_GROUPS), axis=0, keepdims=True)
    gp = 1.0 / jnp.sum(jnp.exp(lg - gmax), axis=0, keepdims=True)
    E = EXPERTS_PER_GROUP
    le = jnp.zeros((E, tm), F32)
    for gg in range(N_GROUPS):
        le = jnp.where(gidx == gg, lt[8 + gg * E:8 + (gg + 1) * E], le)
    ei = lax.broadcasted_iota(I32, le.shape, 0)
    v1 = jnp.max(le, axis=0, keepdims=True)
    i1 = jnp.min(jnp.where(le == v1, ei, E), axis=0, keepdims=True)
    rest = jnp.where(ei == i1, -jnp.inf, le)
    v2 = jnp.max(rest, axis=0, keepdims=True)
    i2 = jnp.min(jnp.where(rest == v2, ei, E), axis=0, keepdims=True)
    e = jnp.exp(v2 - v1)
    wa = gp / (1.0 + e)
    wb = gp * e / (1.0 + e)
    lo = jnp.minimum(i1, i2)
    hi = jnp.maximum(i1, i2)
    pair = lax.shift_right_logical(lo * (2 * E - 1 - lo), 1) + hi - lo - 1
    cls = gidx * PAIRS_PER_GROUP + pair
    first_lo = i1 < i2
    w_lo = jnp.where(first_lo, wa, wb)
    w_hi = jnp.where(first_lo, wb, wa)
    oh = lax.broadcasted_iota(I32, (N_CLASS_PAD, tm), 0) == cls
    before = _dot(oh.astype(BF), tri_ref[...]) + cnt_scr[:, 0:1]
    rank = jnp.sum(jnp.where(oh, before, 0.0), axis=0, keepdims=True).astype(I32)
    cnt_scr[...] = cnt_scr[...] + jnp.sum(oh.astype(F32), axis=1, keepdims=True)
    cnt_ref[...] = cnt_scr[...]
    row = lax.broadcasted_iota(I32, (8, tm), 0)
    crk_ref[...] = jnp.where(row == 0, cls, jnp.where(row == 1, rank, 0))
    rw_ref[...] = jnp.where(row == 0, w_lo, jnp.where(row == 1, w_hi, 0.0))


def _mixer_out(up, o, gm, x, mod, pw, psc, wup, wua, wo, g2, wr, br):
    B, S, D = x.shape
    tm = ROUTE_TILE
    nt = S // tm
    tri = jnp.asarray(np.triu(np.ones((tm, tm), np.float32), k=1), BF)
    row = lambda n: pl.BlockSpec((None, tm, n), lambda b, i: (b, i, 0))
    full = lambda a: pl.BlockSpec(a.shape, lambda b, i: (0,) * a.ndim)
    per = tm // POOL_HALO
    return pl.pallas_call(
        _mixer_out_kernel,
        grid=(B, nt),
        in_specs=[row(POOL_WIDTH),
                  pl.BlockSpec((None, POOL_HALO, POOL_WIDTH),
                               lambda b, i: (b, jnp.maximum(i * per - 1, 0), 0)),
                  row(ATTN_WIDTH), row(2 * D_MODEL), row(D),
                  pl.BlockSpec((None, 6, D), lambda b, i: (b, 0, 0)),
                  full(pw), full(psc), full(wup), full(wua), full(wo), full(g2),
                  full(wr), full(br), full(tri)],
        out_specs=[row(D),
                   pl.BlockSpec((tm * ROW_TILES, LANES), lambda b, i: (b * nt + i, 0)),
                   pl.BlockSpec((None, 8, tm), lambda b, i: (b * nt + i, 0, 0)),
                   pl.BlockSpec((None, 8, tm), lambda b, i: (b * nt + i, 0, 0)),
                   pl.BlockSpec((N_CLASS_PAD, LANES), lambda b, i: (0, 0))],
        out_shape=[jax.ShapeDtypeStruct((B, S, D), F32),
                   jax.ShapeDtypeStruct((B * S * ROW_TILES, LANES), F32),
                   jax.ShapeDtypeStruct((B * nt, 8, tm), I32),
                   jax.ShapeDtypeStruct((B * nt, 8, tm), F32),
                   jax.ShapeDtypeStruct((N_CLASS_PAD, LANES), F32)],
        scratch_shapes=[pltpu.VMEM((N_CLASS_PAD, LANES), F32)],
        compiler_params=_params("arbitrary", "arbitrary"),
        name="mixer_out_router",
    )(up, up, o, gm, x, mod, pw, psc, wup, wua, wo, g2, wr, br, tri)


def _plan_kernel(crk_ref, cnt_ref, etab_ref, pos_ref, tmap_ref):
    C = N_CLASS_PAD
    cnt = cnt_ref[:, 0:1].astype(I32)
    ntile = lax.shift_right_logical(cnt + (EXPERT_TILE - 1), EXPERT_TILE_LOG2)
    ntile_f = ntile.astype(F32)
    r = lax.broadcasted_iota(I32, (C, C), 0)
    c = lax.broadcasted_iota(I32, (C, C), 1)
    lower = (c < r).astype(BF)
    first = _dot(lower, jnp.broadcast_to(ntile_f, (C, LANES)).astype(BF))[:, 0:1]
    last = first + ntile_f
    total = jnp.sum(ntile_f, axis=0, keepdims=True)
    off = (first * EXPERT_TILE).astype(I32)

    def body(i, carry):
        cls = crk_ref[i, 0:1, :]
        rank = crk_ref[i, 1:2, :]
        oh = lax.broadcasted_iota(I32, (C, cls.shape[1]), 0) == cls
        pos_ref[pl.ds(i, 1), :] = jnp.sum(jnp.where(oh, off, 0), axis=0, keepdims=True) + rank
        return carry

    lax.fori_loop(0, crk_ref.shape[0], body, 0)

    nj = tmap_ref.shape[1]
    j = lax.broadcasted_iota(I32, (1, nj), 1).astype(F32)
    jj = jnp.minimum(j, total - 1.0)
    tcls = jnp.sum((last <= jj).astype(I32), axis=0, keepdims=True)
    oh2 = lax.broadcasted_iota(I32, (C, nj), 0) == tcls
    elo = jnp.sum(jnp.where(oh2, etab_ref[:, 0:1], 0), axis=0, keepdims=True)
    ehi = jnp.sum(jnp.where(oh2, etab_ref[:, 1:2], 0), axis=0, keepdims=True)
    row = lax.broadcasted_iota(I32, (8, nj), 0)
    tmap_ref[...] = jnp.where(
        row == 0, elo, jnp.where(row == 1, ehi, jnp.where(
            row == 2, (j < total).astype(I32), jnp.where(row == 3, jj.astype(I32), 0))))


def _plan(crk, cnt, n_tiles_pad):
    nt, _, tm = crk.shape
    etab = np.zeros((N_CLASS_PAD, LANES), np.int32)
    cid = 0
    for g in range(N_GROUPS):
        for lo in range(EXPERTS_PER_GROUP):
            for hi in range(lo + 1, EXPERTS_PER_GROUP):
                etab[cid, 0] = g * EXPERTS_PER_GROUP + lo
                etab[cid, 1] = g * EXPERTS_PER_GROUP + hi
                cid += 1
    etab = jnp.asarray(etab)
    full = lambda a: pl.BlockSpec(a.shape, lambda i: (0,) * a.ndim)
    return pl.pallas_call(
        _plan_kernel,
        grid=(1,),
        in_specs=[full(crk), full(cnt), full(etab)],
        out_specs=[pl.BlockSpec((nt, tm), lambda i: (0, 0)),
                   pl.BlockSpec((8, n_tiles_pad), lambda i: (0, 0))],
        out_shape=[jax.ShapeDtypeStruct((nt, tm), I32),
                   jax.ShapeDtypeStruct((8, n_tiles_pad), I32)],
        compiler_params=_params("arbitrary"),
        name="moe_plan",
    )(crk, cnt, etab)


def _dispatch_kernel(pos_ref, h_ref, xs_in_ref, xs_ref, sem):
    del xs_in_ref
    i = pl.program_id(0)
    rows = h_ref.shape[0] // ROW_TILES

    def issue(r8, carry):
        for k in range(DMA_UNROLL):
            r = r8 * DMA_UNROLL + k
            p = pos_ref[i * rows + r]
            pltpu.make_async_copy(
                h_ref.at[pl.ds(pl.multiple_of(r * ROW_TILES, ROW_TILES), ROW_TILES)],
                xs_ref.at[pl.ds(pl.multiple_of(p * ROW_TILES, ROW_TILES), ROW_TILES)],
                sem).start()
        return carry

    lax.fori_loop(0, rows // DMA_UNROLL, issue, 0)
    pltpu.make_async_copy(h_ref, xs_ref.at[pl.ds(0, rows * ROW_TILES)], sem).wait()


def _dispatch(pos, h2t, n_sorted_rows):
    rows = DISPATCH_ROWS
    n = h2t.shape[0] // ROW_TILES
    zeros = jnp.zeros((n_sorted_rows * ROW_TILES, LANES), F32)
    return pl.pallas_call(
        _dispatch_kernel,
        grid_spec=pltpu.PrefetchScalarGridSpec(
            num_scalar_prefetch=1,
            grid=(n // rows,),
            in_specs=[pl.BlockSpec((rows * ROW_TILES, LANES), lambda i, p: (i, 0)),
                      pl.BlockSpec(memory_space=pl.ANY)],
            out_specs=pl.BlockSpec(memory_space=pl.ANY),
            scratch_shapes=[pltpu.SemaphoreType.DMA]),
        out_shape=jax.ShapeDtypeStruct(zeros.shape, F32),
        input_output_aliases={2: 0},
        compiler_params=_params("arbitrary"),
        name="moe_dispatch",
    )(pos, h2t, zeros)


def _expert_kernel(elo_ref, ehi_ref, valid_ref, blk_ref, x_ref, wg0, wu0, wd0, wg1, wu1, wd1, o_ref):
    del elo_ref, ehi_ref, blk_ref
    T = EXPERT_TILE
    valid = valid_ref[pl.program_id(0)] > 0

    @pl.when(jnp.logical_not(valid))
    def _():
        o_ref[...] = jnp.zeros_like(o_ref)

    @pl.when(valid)
    def _():
        x = jnp.concatenate([x_ref[pl.ds(k, T, stride=ROW_TILES), :] for k in range(ROW_TILES)],
                            axis=1).astype(BF)
        for half, (wg, wu, wd) in enumerate(((wg0, wu0, wd0), (wg1, wu1, wd1))):
            a = _dot(x, wg[...])
            b = _dot(x, wu[...])
            he = (a * jax.nn.sigmoid(a)) * b
            y = _dot(he.astype(BF), wd[...])
            for k in range(ROW_TILES):
                o_ref[pl.ds(half * ROW_TILES + k, T, stride=2 * ROW_TILES), :] = (
                    y[:, k * LANES:(k + 1) * LANES])


def _experts(tmap, xs, wg, wu, wd, n_tiles):
    T = EXPERT_TILE
    D, F = D_MODEL, EXPERT_FF
    lo = lambda shape: pl.BlockSpec((None,) + shape, lambda j, elo, ehi, v, blk: (elo[j], 0, 0))
    hi = lambda shape: pl.BlockSpec((None,) + shape, lambda j, elo, ehi, v, blk: (ehi[j], 0, 0))
    return pl.pallas_call(
        _expert_kernel,
        grid_spec=pltpu.PrefetchScalarGridSpec(
            num_scalar_prefetch=4,
            grid=(n_tiles,),
            in_specs=[pl.BlockSpec((T * ROW_TILES, LANES), lambda j, elo, ehi, v, blk: (blk[j], 0)),
                      lo((D, F)), lo((D, F)), lo((F, D)), hi((D, F)), hi((D, F)), hi((F, D))],
            out_specs=pl.BlockSpec((T * 2 * ROW_TILES, LANES), lambda j, elo, ehi, v, blk: (j, 0))),
        out_shape=jax.ShapeDtypeStruct((n_tiles * T * 2 * ROW_TILES, LANES), F32),
        compiler_params=_params("arbitrary"),
        name="moe_experts",
    )(tmap[0], tmap[1], tmap[2], tmap[3], xs, wg, wu, wd, wg, wu, wd)


def _combine_kernel(pos_ref, ys_ref, w_ref, x1_ref, mod_ref, fg_ref, o_ref, buf0, buf1, sem):
    i = pl.program_id(0)
    last = pl.num_programs(0) - 1
    tm = x1_ref.shape[0]
    R2 = 2 * ROW_TILES

    def gather(step, buf, s):
        for r in range(tm):
            p = pos_ref[step * tm + r]
            pltpu.make_async_copy(ys_ref.at[pl.ds(pl.multiple_of(p * R2, R2), R2)],
                                  buf.at[pl.ds(r * R2, R2)], sem.at[s]).start()

    def wait(buf, s):
        pltpu.make_async_copy(ys_ref.at[pl.ds(0, tm * R2)], buf, sem.at[s]).wait()

    def step(buf, s, nbuf, ns):
        def run():
            wait(buf, s)
            gather(jnp.minimum(i + 1, last), nbuf, ns)
            ylo = jnp.concatenate([buf[pl.ds(k, tm, stride=R2), :] for k in range(ROW_TILES)], axis=1)
            yhi = jnp.concatenate([buf[pl.ds(ROW_TILES + k, tm, stride=R2), :]
                                   for k in range(ROW_TILES)], axis=1)
            y = w_ref[:, 0:1] * ylo + w_ref[:, 1:2] * yhi
            x2 = x1_ref[...] + mod_ref[5:6, :] * y
            r = lax.rsqrt(jnp.mean(x2 * x2, axis=-1, keepdims=True) + EPS)
            o_ref[...] = x2 * r * fg_ref[...]

            @pl.when(i == last)
            def _():
                wait(nbuf, ns)
        return run

    @pl.when(i == 0)
    def _():
        gather(0, buf0, 0)

    lax.cond(i % 2 == 0, step(buf0, 0, buf1, 1), step(buf1, 1, buf0, 0))


def _combine(pos, ys, w, x1, mod, fg):
    N, D = x1.shape
    B = mod.shape[0]
    tm = COMBINE_ROWS
    per_b = (N // B) // tm
    return pl.pallas_call(
        _combine_kernel,
        grid_spec=pltpu.PrefetchScalarGridSpec(
            num_scalar_prefetch=1,
            grid=(N // tm,),
            in_specs=[pl.BlockSpec(memory_space=pl.ANY),
                      pl.BlockSpec((tm, 2), lambda i, p: (i, 0)),
                      pl.BlockSpec((tm, D), lambda i, p: (i, 0)),
                      pl.BlockSpec((None, 6, D), lambda i, p: (i // per_b, 0, 0)),
                      pl.BlockSpec((1, D), lambda i, p: (0, 0))],
            out_specs=pl.BlockSpec((tm, D), lambda i, p: (i, 0)),
            scratch_shapes=[pltpu.VMEM((tm * 2 * ROW_TILES, LANES), F32),
                            pltpu.VMEM((tm * 2 * ROW_TILES, LANES), F32),
                            pltpu.SemaphoreType.DMA((2,))]),
        out_shape=jax.ShapeDtypeStruct((N, D), F32),
        compiler_params=_params("arbitrary"),
        name="moe_combine",
    )(pos, ys, w, x1, mod, fg)


def _arrange_w_in(w):
    cuts = np.cumsum([0, POOL_WIDTH, ATTN_WIDTH] + [KV_WIDTH] * 6 + [N_BRANCH * N_HEADS, 2 * D_MODEL])
    pool, q, kc, vc, ks, vs, kw, vw, bg, mg = [
        w[:, int(cuts[i]):int(cuts[i + 1])] for i in range(len(cuts) - 1)]
    bg = jnp.pad(bg, ((0, 0), (0, LANES - bg.shape[1])))
    main = jnp.concatenate([pool, q, kc, vc, ks, kw, bg, mg], axis=1).astype(BF)
    return main, jnp.concatenate([vs, vw], axis=1).T.astype(BF)


def _compress_weights(pos, w1, b1, w2, transposed):
    eye = jnp.eye(N_KV_GROUPS, dtype=F32)
    halves = CMP_BLOCK // CMP_STRIDE
    w1r = w1.reshape(halves, CMP_STRIDE, HEAD_DIM, CMP_HIDDEN)
    w1big = jnp.einsum('hidc,gk->igdkhc', w1r, eye).reshape(
        CMP_STRIDE * N_KV_GROUPS * HEAD_DIM, N_KV_GROUPS * halves * CMP_HIDDEN)
    if transposed:
        w2big = jnp.einsum('cd,gk->kdgc', w2, eye).reshape(
            N_KV_GROUPS * HEAD_DIM, N_KV_GROUPS * CMP_HIDDEN)
    else:
        w2big = jnp.einsum('cd,gk,r->gckrd', w2, eye, jnp.asarray([1.0, 0.0], F32)).reshape(
            N_KV_GROUPS * CMP_HIDDEN, N_KV_GROUPS * 2 * HEAD_DIM)
    pos8 = jnp.broadcast_to(pos.reshape(1, CMP_BLOCK * HEAD_DIM), (8, CMP_BLOCK * HEAD_DIM))
    return (w1big.astype(BF), pos8.astype(BF), w1.astype(BF), b1.reshape(1, CMP_HIDDEN),
            w2big.astype(BF))


def _selection_tables(S):
    n_chunks = S // CMP_STRIDE
    n_cmp = n_chunks - CMP_BLOCK // CMP_STRIDE + 1
    n_blk = S // SEL_BLOCK
    s1 = np.arange(n_cmp)[:, None] * CMP_STRIDE
    s2 = np.arange(n_blk)[None, :] * SEL_BLOCK
    ovl = np.clip(np.minimum(s1 + CMP_BLOCK, s2 + SEL_BLOCK) - np.maximum(s1, s2), 0, None) / CMP_BLOCK
    ovt = np.zeros((n_blk, n_chunks), np.float32)
    ovt[:, :n_cmp] = ovl.T
    return jnp.asarray(ovt, BF)


def kernel(x, c, ada_w, ada_b, norm1_g, w_in, pool_w, pool_scale, cmp_pos, cmp_w1, cmp_b1, cmp_w2,
           w_up_pool, w_up_attn, w_out, norm2_g, router_g_w, router_g_b, router_e_w, router_e_b,
           exp_w_gate, exp_w_up, exp_w_down, final_g):
    B, S, D = x.shape
    N = B * S
    assert ada_w.shape[0] == 1, "the final norm is fused into the last layer's combine step"
    for l in range(ada_w.shape[0]):
        mod = _ada(c, ada_w[l], ada_b[l]).reshape(B, 6, D)
        (up, q, kc, vc, ks, kw, bg, gm, vst, vwt) = _inproj(
            x, mod, norm1_g[l].reshape(1, D), *_arrange_w_in(w_in[l]))
        n_chunks = S // CMP_STRIDE
        kcc = _compress(kc.reshape(B, n_chunks, CMP_STRIDE * KV_WIDTH),
                        *_compress_weights(cmp_pos[l, 0], cmp_w1[l, 0], cmp_b1[l, 0], cmp_w2[l, 0], False),
                        False)
        vct = _compress(vc.reshape(B, n_chunks, CMP_STRIDE * KV_WIDTH),
                        *_compress_weights(cmp_pos[l, 1], cmp_w1[l, 1], cmp_b1[l, 1], cmp_w2[l, 1], True),
                        True)
        o = _attention(q, kcc, vct, ks, vst, kw, vwt, bg, _selection_tables(S))
        wr = jnp.zeros((8 + N_EXPERTS, D), F32)
        wr = wr.at[0:N_GROUPS].set(router_g_w[l].T).at[8:].set(router_e_w[l].T).astype(BF)
        br = jnp.zeros((8 + N_EXPERTS, 1), F32)
        br = br.at[0:N_GROUPS, 0].set(router_g_b[l]).at[8:, 0].set(router_e_b[l])
        x1, h2t, crk, rw, cnt = _mixer_out(
            up, o, gm, x, mod, pool_w[l].astype(BF), pool_scale[l].reshape(1, POOL_WIDTH),
            w_up_pool[l].astype(BF), w_up_attn[l].astype(BF), w_out[l].astype(BF),
            norm2_g[l].reshape(1, D), wr, br)
        n_tiles = N // EXPERT_TILE + N_CLASS
        n_tiles_pad = -(-n_tiles // LANES) * LANES
        pos, tmap = _plan(crk, cnt, n_tiles_pad)
        pos = pos.reshape(N)
        xs = _dispatch(pos, h2t, n_tiles * EXPERT_TILE)
        ys = _experts(tmap, xs, exp_w_gate[l].astype(BF), exp_w_up[l].astype(BF),
                      exp_w_down[l].astype(BF), n_tiles)
        w = jnp.stack([rw[:, 0, :].reshape(N), rw[:, 1, :].reshape(N)], axis=1)
        y = _combine(pos, ys, w, x1.reshape(N, D), mod, final_g.reshape(1, D))
        x = y.reshape(B, S, D)
    return x
```

```python
import functools

import numpy as np
import jax
import jax.numpy as jnp
from jax import lax
from jax.experimental import pallas as pl
from jax.experimental.pallas import tpu as pltpu

BF = jnp.bfloat16
F32 = jnp.float32
I32 = jnp.int32

D_MODEL = 1024
POOL_WIDTH = 512
POOL_WINDOWS = (2, 4, 8, 16)
POOL_GROUP = 128
POOL_HALO = 16
N_HEADS = 8
HEAD_DIM = 64
N_KV_GROUPS = 2
HEADS_PER_GROUP = 4
ATTN_WIDTH = 512
KV_WIDTH = 128
CMP_BLOCK = 32
CMP_STRIDE = 16
CMP_HIDDEN = 256
SEL_BLOCK = 64
N_SELECT = 8
WINDOW = 512
Q_CHUNK = 64
N_BRANCH = 3
N_GROUPS = 4
EXPERTS_PER_GROUP = 8
N_EXPERTS = 32
EXPERT_FF = 512
EPS = 1e-6
NEG = -1e30
FORCE_SCORE = 1e4
QK_SCALE = HEAD_DIM ** -0.5
LOG2E = 1.4426950408889634

PAIRS_PER_GROUP = EXPERTS_PER_GROUP * (EXPERTS_PER_GROUP - 1) // 2
N_CLASS = N_GROUPS * PAIRS_PER_GROUP
N_CLASS_PAD = 128
ROUTE_TILE = 512
EXPERT_TILE_LOG2 = 8
EXPERT_TILE = 1 << EXPERT_TILE_LOG2
DISPATCH_ROWS = 1024
DISPATCH_BUFFERS = 3
COMBINE_ROWS = 256
DMA_UNROLL = 8

LANES = 128
ROW_TILES = D_MODEL // LANES
SEL_KEY_TILE = 512
ATTN_BATCH = 2
DEN_ROWS = 16
VMEM_LIMIT = 56 * 1024 * 1024

C_POOL = 0
C_Q = C_POOL + POOL_WIDTH
C_KC = C_Q + ATTN_WIDTH
C_VC = C_KC + KV_WIDTH
C_KS = C_VC + KV_WIDTH
C_KW = C_KS + KV_WIDTH
C_BG = C_KW + KV_WIDTH
C_MG = C_BG + LANES
C_END = C_MG + 2 * D_MODEL


def _dot(a, b):
    return jnp.dot(a, b, preferred_element_type=F32)


def _dot_nt(a, b):
    return lax.dot_general(a, b, (((1,), (1,)), ((), ())), preferred_element_type=F32)


def _params(*sem):
    return pltpu.CompilerParams(dimension_semantics=sem, vmem_limit_bytes=VMEM_LIMIT)


def _ada_kernel(c_ref, w_ref, b_ref, o_ref):
    o_ref[...] = _dot(c_ref[...].astype(BF), w_ref[...].astype(BF)) + b_ref[...]


def _ada(c, w, b):
    B, D = c.shape
    n = w.shape[1]
    tn = 1024
    return pl.pallas_call(
        _ada_kernel,
        grid=(n // tn,),
        in_specs=[pl.BlockSpec((B, D), lambda j: (0, 0)),
                  pl.BlockSpec((D, tn), lambda j: (0, j)),
                  pl.BlockSpec((1, tn), lambda j: (0, j))],
        out_specs=pl.BlockSpec((B, tn), lambda j: (0, j)),
        out_shape=jax.ShapeDtypeStruct((B, n), F32),
        compiler_params=_params("arbitrary"),
        name="ada_mod",
    )(c, w, b.reshape(1, n))


def _inproj_kernel(x_ref, mod_ref, g_ref, w_ref, wvt_ref, blk_ref, up_ref, q_ref, kc_ref, vc_ref,
                   ks_ref, kw_ref, bg_ref, gm_ref, vst_ref, vwt_ref):
    x = x_ref[...]
    r = lax.rsqrt(jnp.mean(x * x, axis=-1, keepdims=True) + EPS)
    h = x * r * g_ref[...] * (1.0 + mod_ref[1:2, :]) + mod_ref[0:1, :]
    hb = h.astype(BF)

    def proj(a, b):
        return _dot(hb, w_ref[:, a:b])

    def spread(v, fill):
        pad = jnp.full((v.shape[0], HEAD_DIM), fill, v.dtype)
        pieces = []
        for i in range(v.shape[1] // HEAD_DIM):
            pieces += [v[:, i * HEAD_DIM:(i + 1) * HEAD_DIM], pad]
        return jnp.concatenate(pieces, axis=1)

    up_ref[...] = proj(C_POOL, C_Q)
    q_ref[...] = spread(proj(C_Q, C_KC) * QK_SCALE, 0.0).astype(BF)
    kc_ref[...] = proj(C_KC, C_VC).astype(BF)
    vc_ref[...] = proj(C_VC, C_KS).astype(BF)
    ks_ref[...] = (spread(proj(C_KS, C_KW) * LOG2E, 0.0) + blk_ref[...]).astype(BF)
    kw_ref[...] = spread(proj(C_KW, C_BG) * LOG2E, 0.0).astype(BF)
    vt = _dot_nt(wvt_ref[...], hb)
    vst_ref[...] = vt[:KV_WIDTH].astype(BF)
    vwt_ref[...] = vt[KV_WIDTH:].astype(BF)
    bg_ref[...] = proj(C_BG, C_MG)
    gm_ref[...] = jax.nn.sigmoid(proj(C_MG, C_END)).astype(BF)


def _inproj(x, mod, g, w, wvt):
    B, S, D = x.shape
    tm = 512
    blk = np.zeros((S, 2 * LANES), np.float32)
    for gg in range(N_KV_GROUPS):
        blk[np.arange(S), gg * LANES + HEAD_DIM + np.arange(S) // SEL_BLOCK] = 1.0
    blk = jnp.asarray(blk)
    widths = [(POOL_WIDTH, F32), (N_HEADS * LANES, BF), (KV_WIDTH, BF), (KV_WIDTH, BF),
              (2 * KV_WIDTH, BF), (2 * KV_WIDTH, BF), (LANES, F32), (2 * D_MODEL, BF)]
    row = lambda n: pl.BlockSpec((None, tm, n), lambda b, i: (b, i, 0))
    col = pl.BlockSpec((None, KV_WIDTH, tm), lambda b, i: (b, 0, i))
    return pl.pallas_call(
        _inproj_kernel,
        grid=(B, S // tm),
        in_specs=[row(D),
                  pl.BlockSpec((None, 6, D), lambda b, i: (b, 0, 0)),
                  pl.BlockSpec((1, D), lambda b, i: (0, 0)),
                  pl.BlockSpec((D, C_END), lambda b, i: (0, 0)),
                  pl.BlockSpec((2 * KV_WIDTH, D), lambda b, i: (0, 0)),
                  pl.BlockSpec((tm, 2 * LANES), lambda b, i: (i, 0))],
        out_specs=[row(n) for n, _ in widths] + [col, col],
        out_shape=[jax.ShapeDtypeStruct((B, S, n), dt) for n, dt in widths]
        + [jax.ShapeDtypeStruct((B, KV_WIDTH, S), BF)] * 2,
        compiler_params=_params("arbitrary", "arbitrary"),
        name="norm1_inproj",
    )(x, mod, g, w, wvt, blk)


def _gelu_tanh(x):
    return 0.5 * x * (1.0 + jnp.tanh(0.7978845608028654 * (x + 0.044715 * x * x * x)))


def _compress_kernel(transposed, x_ref, w1b_ref, pos_ref, w1_ref, b1_ref, w2b_ref, o_ref):
    y = _dot(x_ref[...], w1b_ref[...])
    posc = _dot(pos_ref[...], w1_ref[...])[0:1, :] + b1_ref[...]
    n = y.shape[0]
    acts = []
    for g in range(N_KV_GROUPS):
        first = y[:, g * 2 * CMP_HIDDEN: g * 2 * CMP_HIDDEN + CMP_HIDDEN]
        second = y[:, g * 2 * CMP_HIDDEN + CMP_HIDDEN: (g + 1) * 2 * CMP_HIDDEN]
        pre = first + pltpu.roll(second, n - 1, 0) + posc
        acts.append(_gelu_tanh(pre).astype(BF))
    act = jnp.concatenate(acts, axis=1)
    if transposed:
        o_ref[...] = _dot_nt(w2b_ref[...], act).astype(BF)
    else:
        o_ref[...] = _dot(act, w2b_ref[...]).astype(BF)


def _compress(xk, w1big, pos8, w1, b1, w2big, transposed):
    B, n, width = xk.shape
    full = lambda a: pl.BlockSpec(a.shape, lambda b: (0,) * a.ndim)
    out = (KV_WIDTH, n) if transposed else (n, 2 * KV_WIDTH)
    return pl.pallas_call(
        functools.partial(_compress_kernel, transposed),
        grid=(B,),
        in_specs=[pl.BlockSpec((None, n, width), lambda b: (b, 0, 0)),
                  full(w1big), full(pos8), full(w1), full(b1), full(w2big)],
        out_specs=pl.BlockSpec((None,) + out, lambda b: (b, 0, 0)),
        out_shape=jax.ShapeDtypeStruct((B,) + out, BF),
        compiler_params=_params("arbitrary"),
        name="compress",
    )(xk, w1big, pos8, w1, b1, w2big)


def _masked_exp(s, mask):
    sm = jnp.where(mask, s, NEG)
    m = jnp.max(sm, axis=-1, keepdims=True)
    p = jnp.where(mask, jnp.exp(sm - m), 0.0)
    return p, jnp.sum(p, axis=-1, keepdims=True)


def _safe_inv(l):
    return jnp.where(l > 0.0, 1.0 / jnp.where(l > 0.0, l, 1.0), 0.0)


def _softmax_tile(s, m_old):
    m_new = jnp.maximum(m_old, jnp.max(s, axis=-1, keepdims=True))
    return m_new, jnp.exp2(s - m_new)


def _attn_kernel(q_ref, kc_ref, vct_ref, ks_ref, vst_ref, kw_ref, vwt_ref, bg_ref, ovt_ref, o_ref):
    ci = pl.program_id(1)
    q0 = ci * Q_CHUNK
    Q, H, G = Q_CHUNK, HEADS_PER_GROUP, N_KV_GROUPS
    R = H * Q
    n_blk = ovt_ref.shape[0]
    units = [(bb, g) for bb in range(q_ref.shape[0]) for g in range(G)]
    U = len(units)
    sig = [jax.nn.sigmoid(bg_ref[bb]) for bb in range(q_ref.shape[0])]
    t_q = q0 + lax.broadcasted_iota(I32, (Q, 1), 0)
    t_r = jnp.concatenate([t_q] * H, axis=0)

    def rows4(a):
        return jnp.concatenate([a] * H, axis=0)

    def q_rows(bb, g):
        return jnp.concatenate(
            [q_ref[bb, :, (g * H + h) * LANES:(g * H + h + 1) * LANES] for h in range(H)], axis=0)

    gcs = [slice(g * LANES, (g + 1) * LANES) for g in range(G)]
    grs = [slice(g * HEAD_DIM, (g + 1) * HEAD_DIM) for g in range(G)]
    qp = [q_rows(bb, g) for bb, g in units]

    def pv_t(vt, p):
        vt1 = jnp.concatenate([vt, jnp.ones((DEN_ROWS, vt.shape[1]), BF)], axis=0)
        return _dot_nt(vt1, p.astype(BF))

    w0 = pl.multiple_of((jnp.maximum(q0 - WINDOW, 0) // LANES) * LANES, LANES)
    wkeys = WINDOW + 2 * Q_CHUNK
    s3 = [_dot_nt(qp[u], kw_ref[bb, pl.ds(w0, wkeys), gcs[g]]) for u, (bb, g) in enumerate(units)]
    s1 = [_dot_nt(qp[u], kc_ref[bb, :, gcs[g]]) for u, (bb, g) in enumerate(units)]

    n_idx = lax.broadcasted_iota(I32, s1[0].shape, 1)
    m1 = (n_idx * CMP_STRIDE + (CMP_BLOCK - 1)) <= t_r
    o1, psums = [], []
    for u, (bb, g) in enumerate(units):
        p1, l1 = _masked_exp(s1[u], m1)
        iv = _safe_inv(l1)
        o1.append(pv_t(vct_ref[bb, grs[g], :], p1))
        p1n = p1 * iv
        psum = p1n[0:Q]
        for h in range(1, H):
            psum = psum + p1n[h * Q:(h + 1) * Q]
        psums.append(psum)
    psum = jnp.concatenate(psums, axis=0)
    hi = psum.astype(BF)
    lo = (psum - hi.astype(F32)).astype(BF)
    ps_t = _dot_nt(ovt_ref[...], hi) + _dot_nt(ovt_ref[...], lo)

    kpos3 = w0 + lax.broadcasted_iota(I32, (Q, wkeys), 1)
    bias3 = rows4(jnp.where((kpos3 <= t_q) & (kpos3 > t_q - WINDOW), 0.0, NEG))
    win = []
    for u, (bb, g) in enumerate(units):
        _, p3 = _softmax_tile(s3[u] + bias3, jnp.full((R, 1), NEG, F32))
        win.append(pv_t(vwt_ref[bb, grs[g], pl.ds(w0, wkeys)], p3))

    sig_t = [jnp.concatenate([s, s], axis=0).T for s in sig]
    lane_lo = lax.broadcasted_iota(I32, (1, LANES), 1) < HEAD_DIM

    def gate_row(bb, g, branch):
        rows = [sig_t[bb][branch * N_HEADS + g * H + h:branch * N_HEADS + g * H + h + 1, :]
                for h in range(H)]
        return jnp.concatenate([jnp.where(lane_lo, rows[2 * k], rows[2 * k + 1])
                                for k in range(H // 2)], axis=1)

    def normalised(ot, may_be_empty):
        den = ot[HEAD_DIM:HEAD_DIM + 1, :]
        return ot[:HEAD_DIM] * (_safe_inv(den) if may_be_empty else 1.0 / den)

    early = [gate_row(bb, g, 0) * normalised(o1[u], True)
             + gate_row(bb, g, 2) * normalised(win[u], False)
             for u, (bb, g) in enumerate(units)]
    gate_sel = [gate_row(bb, g, 1) for bb, g in units]

    j = lax.broadcasted_iota(I32, ps_t.shape, 0)
    forced = (j == 0) | (j == ci) | (j == ci - 1)
    score = jnp.where(forced, FORCE_SCORE, jnp.where(j <= ci, ps_t, NEG))
    rank = jnp.zeros(ps_t.shape, I32)
    for jp in range(n_blk):
        c = score[jp:jp + 1, :]
        beats = (c > score) | ((c == score) & (j > jp))
        rank = rank + beats.astype(I32)
    bias_t = jnp.where(rank < N_SELECT, 0.0, NEG)
    pad_t = jnp.concatenate([jnp.zeros((HEAD_DIM, U * Q), F32), bias_t,
                             jnp.zeros((LANES - HEAD_DIM - n_blk, U * Q), F32)], axis=0)
    sel_bias = pad_t.T.astype(BF)

    qa = [qp[u] + rows4(sel_bias[u * Q:(u + 1) * Q]) for u in range(U)]

    def sweep(n_tiles):
        past = (n_tiles - 1) * SEL_KEY_TILE
        keys = n_tiles * SEL_KEY_TILE

        def run():
            kpos = past + lax.broadcasted_iota(I32, (Q, SEL_KEY_TILE), 1)
            bias = rows4(jnp.where(kpos <= t_q, 0.0, NEG))
            s = [_dot_nt(qa[u], ks_ref[bb, 0:keys, gcs[g]]) for u, (bb, g) in enumerate(units)]
            out = []
            for u, (bb, g) in enumerate(units):
                s_last = s[u][:, past:] + bias
                m = jnp.max(s_last, axis=-1, keepdims=True)
                if past:
                    m = jnp.maximum(m, jnp.max(s[u][:, :past], axis=-1, keepdims=True))
                acc = pv_t(vst_ref[bb, grs[g], past:keys], jnp.exp2(s_last - m))
                if past:
                    acc = acc + pv_t(vst_ref[bb, grs[g], 0:past], jnp.exp2(s[u][:, :past] - m))
                out.append(acc)
            return tuple(out)
        return run

    blocks_per_tile = SEL_KEY_TILE // SEL_BLOCK
    max_tiles = n_blk // blocks_per_tile
    sel = lax.switch(ci // blocks_per_tile, [sweep(n) for n in range(1, max_tiles + 1)])

    for u, (bb, g) in enumerate(units):
        out = (early[u] + gate_sel[u] * normalised(sel[u], False)).T
        for k in range(H // 2):
            slab = jnp.concatenate([out[(2 * k) * Q:(2 * k + 1) * Q],
                                    out[(2 * k + 1) * Q:(2 * k + 2) * Q]], axis=1)
            c0 = (g * (H // 2) + k) * LANES
            o_ref[bb, :, c0:c0 + LANES] = slab.astype(BF)


def _attention(q, kc, vc, ks, vs, kw, vw, bg, ovt):
    B, S, _ = q.shape
    nq = S // Q_CHUNK
    nb = ATTN_BATCH if B % ATTN_BATCH == 0 else 1
    per_b = lambda a: pl.BlockSpec((nb,) + a.shape[1:], lambda b, i: (b, 0, 0))
    full = lambda a: pl.BlockSpec(a.shape, lambda b, i: (0,) * a.ndim)
    return pl.pallas_call(
        _attn_kernel,
        grid=(B // nb, nq),
        in_specs=[pl.BlockSpec((nb, Q_CHUNK, N_HEADS * LANES), lambda b, i: (b, i, 0)),
                  per_b(kc), per_b(vc), per_b(ks), per_b(vs), per_b(kw), per_b(vw),
                  pl.BlockSpec((nb, Q_CHUNK, LANES), lambda b, i: (b, i, 0)),
                  full(ovt)],
        out_specs=pl.BlockSpec((nb, Q_CHUNK, ATTN_WIDTH), lambda b, i: (b, i, 0)),
        out_shape=jax.ShapeDtypeStruct((B, S, ATTN_WIDTH), BF),
        compiler_params=_params("arbitrary", "arbitrary"),
        name="nsa_attention",
    )(q, kc, vc, ks, vs, kw, vw, bg, ovt)


def _mixer_out_kernel(upc_ref, upp_ref, o_ref, gm_ref, x_ref, mod_ref, pw_ref, psc_ref,
                      wup_ref, wua_ref, wo_ref, g2_ref, wr_ref, br_ref, tri_ref,
                      x1_ref, h2t_ref, crk_ref, rw_ref, cnt_ref, cnt_scr):
    i = pl.program_id(1)

    @pl.when((pl.program_id(0) == 0) & (i == 0))
    def _():
        cnt_scr[...] = jnp.zeros_like(cnt_scr)

    tm = upc_ref.shape[0]
    prev = upp_ref[...] * (i > 0).astype(F32)
    ext = jnp.concatenate([prev, upc_ref[...]], axis=0)
    t = i * tm + lax.broadcasted_iota(I32, (tm, 1), 0)
    ys = []
    for gi, w in enumerate(POOL_WINDOWS):
        u = ext[:, gi * POOL_GROUP:(gi + 1) * POOL_GROUP]
        acc = u
        shift = 1
        while shift < w:
            acc = acc + pltpu.roll(acc, shift, 0)
            shift *= 2
        inv_cnt = 1.0 / jnp.minimum(t + 1, w).astype(F32)
        p = acc[POOL_HALO:] * inv_cnt - u[POOL_HALO:]
        ys.append(_dot(p.astype(BF), pw_ref[gi]))
    y = jnp.concatenate(ys, axis=1) * psc_ref[...]
    y_pool = _dot(y.astype(BF), wup_ref[...])
    y_attn = _dot(o_ref[...], wua_ref[...])
    gm = gm_ref[...].astype(F32)
    mix = gm[:, :D_MODEL] * y_pool + gm[:, D_MODEL:] * y_attn
    x1 = x_ref[...] + mod_ref[2:3, :] * _dot(mix.astype(BF), wo_ref[...])
    x1_ref[...] = x1
    r = lax.rsqrt(jnp.mean(x1 * x1, axis=-1, keepdims=True) + EPS)
    h2f = x1 * r * g2_ref[...] * (1.0 + mod_ref[4:5, :]) + mod_ref[3:4, :]
    h2 = h2f.astype(BF)
    for k in range(ROW_TILES):
        h2t_ref[pl.ds(k, tm, stride=ROW_TILES), :] = h2f[:, k * LANES:(k + 1) * LANES]

    lt = _dot_nt(wr_ref[...], h2) + br_ref[...]
    lg = lt[0:N_GROUPS]
    gmax = jnp.max(lg, axis=0, keepdims=True)
    gi_ = lax.broadcasted_iota(I32, lg.shape, 0)
    gidx = jnp.min(jnp.where(lg == gmax, gi_, N_GROUPS), axis=0, keepdims=True)
    gp = 1.0 / jnp.sum(jnp.exp(lg - gmax), axis=0, keepdims=True)
    E = EXPERTS_PER_GROUP
    le = jnp.zeros((E, tm), F32)
    for gg in range(N_GROUPS):
        le = jnp.where(gidx == gg, lt[8 + gg * E:8 + (gg + 1) * E], le)
    ei = lax.broadcasted_iota(I32, le.shape, 0)
    v1 = jnp.max(le, axis=0, keepdims=True)
    i1 = jnp.min(jnp.where(le == v1, ei, E), axis=0, keepdims=True)
    rest = jnp.where(ei == i1, -jnp.inf, le)
    v2 = jnp.max(rest, axis=0, keepdims=True)
    i2 = jnp.min(jnp.where(rest == v2, ei, E), axis=0, keepdims=True)
    e = jnp.exp(v2 - v1)
    wa = gp / (1.0 + e)
    wb = gp * e / (1.0 + e)
    lo = jnp.minimum(i1, i2)
    hi = jnp.maximum(i1, i2)
    pair = lax.shift_right_logical(lo * (2 * E - 1 - lo), 1) + hi - lo - 1
    cls = gidx * PAIRS_PER_GROUP + pair
    first_lo = i1 < i2
    w_lo = jnp.where(first_lo, wa, wb)
    w_hi = jnp.where(first_lo, wb, wa)
    oh = lax.broadcasted_iota(I32, (N_CLASS_PAD, tm), 0) == cls
    before = _dot(oh.astype(BF), tri_ref[...]) + cnt_scr[:, 0:1]
    rank = jnp.sum(jnp.where(oh, before, 0.0), axis=0, keepdims=True).astype(I32)
    cnt_scr[...] = cnt_scr[...] + jnp.sum(oh.astype(F32), axis=1, keepdims=True)
    cnt_ref[...] = cnt_scr[...]
    row = lax.broadcasted_iota(I32, (8, tm), 0)
    crk_ref[...] = jnp.where(row == 0, cls, jnp.where(row == 1, rank, 0))
    rw_ref[...] = jnp.where(row == 0, w_lo, jnp.where(row == 1, w_hi, 0.0))


def _mixer_out(up, o, gm, x, mod, pw, psc, wup, wua, wo, g2, wr, br):
    B, S, D = x.shape
    tm = ROUTE_TILE
    nt = S // tm
    tri = jnp.asarray(np.triu(np.ones((tm, tm), np.float32), k=1), BF)
    row = lambda n: pl.BlockSpec((None, tm, n), lambda b, i: (b, i, 0))
    full = lambda a: pl.BlockSpec(a.shape, lambda b, i: (0,) * a.ndim)
    per = tm // POOL_HALO
    return pl.pallas_call(
        _mixer_out_kernel,
        grid=(B, nt),
        in_specs=[row(POOL_WIDTH),
                  pl.BlockSpec((None, POOL_HALO, POOL_WIDTH),
                               lambda b, i: (b, jnp.maximum(i * per - 1, 0), 0)),
                  row(ATTN_WIDTH), row(2 * D_MODEL), row(D),
                  pl.BlockSpec((None, 6, D), lambda b, i: (b, 0, 0)),
                  full(pw), full(psc), full(wup), full(wua), full(wo), full(g2),
                  full(wr), full(br), full(tri)],
        out_specs=[row(D),
                   pl.BlockSpec((tm * ROW_TILES, LANES), lambda b, i: (b * nt + i, 0)),
                   pl.BlockSpec((None, 8, tm), lambda b, i: (b * nt + i, 0, 0)),
                   pl.BlockSpec((None, 8, tm), lambda b, i: (b * nt + i, 0, 0)),
                   pl.BlockSpec((N_CLASS_PAD, LANES), lambda b, i: (0, 0))],
        out_shape=[jax.ShapeDtypeStruct((B, S, D), F32),
                   jax.ShapeDtypeStruct((B * S * ROW_TILES, LANES), F32),
                   jax.ShapeDtypeStruct((B * nt, 8, tm), I32),
                   jax.ShapeDtypeStruct((B * nt, 8, tm), F32),
                   jax.ShapeDtypeStruct((N_CLASS_PAD, LANES), F32)],
        scratch_shapes=[pltpu.VMEM((N_CLASS_PAD, LANES), F32)],
        compiler_params=_params("arbitrary", "arbitrary"),
        name="mixer_out_router",
    )(up, up, o, gm, x, mod, pw, psc, wup, wua, wo, g2, wr, br, tri)


def _plan_kernel(crk_ref, cnt_ref, etab_ref, pos_ref, tmap_ref):
    C = N_CLASS_PAD
    cnt = cnt_ref[:, 0:1].astype(I32)
    ntile = lax.shift_right_logical(cnt + (EXPERT_TILE - 1), EXPERT_TILE_LOG2)
    ntile_f = ntile.astype(F32)
    r = lax.broadcasted_iota(I32, (C, C), 0)
    c = lax.broadcasted_iota(I32, (C, C), 1)
    lower = (c < r).astype(BF)
    first = _dot(lower, jnp.broadcast_to(ntile_f, (C, LANES)).astype(BF))[:, 0:1]
    last = first + ntile_f
    total = jnp.sum(ntile_f, axis=0, keepdims=True)
    off = (first * EXPERT_TILE).astype(I32)

    def body(i, carry):
        cls = crk_ref[i, 0:1, :]
        rank = crk_ref[i, 1:2, :]
        oh = lax.broadcasted_iota(I32, (C, cls.shape[1]), 0) == cls
        pos_ref[pl.ds(i, 1), :] = jnp.sum(jnp.where(oh, off, 0), axis=0, keepdims=True) + rank
        return carry

    lax.fori_loop(0, crk_ref.shape[0], body, 0)

    nj = tmap_ref.shape[1]
    j = lax.broadcasted_iota(I32, (1, nj), 1).astype(F32)
    jj = jnp.minimum(j, total - 1.0)
    tcls = jnp.sum((last <= jj).astype(I32), axis=0, keepdims=True)
    oh2 = lax.broadcasted_iota(I32, (C, nj), 0) == tcls
    elo = jnp.sum(jnp.where(oh2, etab_ref[:, 0:1], 0), axis=0, keepdims=True)
    ehi = jnp.sum(jnp.where(oh2, etab_ref[:, 1:2], 0), axis=0, keepdims=True)
    row = lax.broadcasted_iota(I32, (8, nj), 0)
    tmap_ref[...] = jnp.where(
        row == 0, elo, jnp.where(row == 1, ehi, jnp.where(
            row == 2, (j < total).astype(I32), jnp.where(row == 3, jj.astype(I32), 0))))


def _plan(crk, cnt, n_tiles_pad):
    nt, _, tm = crk.shape
    etab = np.zeros((N_CLASS_PAD, LANES), np.int32)
    cid = 0
    for g in range(N_GROUPS):
        for lo in range(EXPERTS_PER_GROUP):
            for hi in range(lo + 1, EXPERTS_PER_GROUP):
                etab[cid, 0] = g * EXPERTS_PER_GROUP + lo
                etab[cid, 1] = g * EXPERTS_PER_GROUP + hi
                cid += 1
    etab = jnp.asarray(etab)
    full = lambda a: pl.BlockSpec(a.shape, lambda i: (0,) * a.ndim)
    return pl.pallas_call(
        _plan_kernel,
        grid=(1,),
        in_specs=[full(crk), full(cnt), full(etab)],
        out_specs=[pl.BlockSpec((nt, tm), lambda i: (0, 0)),
                   pl.BlockSpec((8, n_tiles_pad), lambda i: (0, 0))],
        out_shape=[jax.ShapeDtypeStruct((nt, tm), I32),
                   jax.ShapeDtypeStruct((8, n_tiles_pad), I32)],
        compiler_params=_params("arbitrary"),
        name="moe_plan",
    )(crk, cnt, etab)


def _dispatch_kernel(pos_ref, h_ref, xs_in_ref, xs_ref, buf, load_sem, scatter_sem):
    del xs_in_ref
    i = pl.program_id(0)
    n = pl.num_programs(0)
    rows = DISPATCH_ROWS
    step_rows = rows * ROW_TILES

    def load(step, slot):
        start = pl.multiple_of(step * step_rows, step_rows)
        return pltpu.make_async_copy(h_ref.at[pl.ds(start, step_rows)], buf.at[slot],
                                     load_sem.at[slot])

    def wait_scatter(slot):
        pltpu.make_async_copy(buf.at[slot], xs_ref.at[pl.ds(0, step_rows)],
                              scatter_sem.at[slot]).wait()

    slot = i % DISPATCH_BUFFERS
    nslot = (i + 1) % DISPATCH_BUFFERS

    @pl.when(i == 0)
    def _():
        load(0, 0).start()

    @pl.when(i >= DISPATCH_BUFFERS - 1)
    def _():
        wait_scatter(nslot)

    @pl.when(i + 1 < n)
    def _():
        load(i + 1, nslot).start()

    load(i, slot).wait()

    def issue(r8, carry):
        for k in range(DMA_UNROLL):
            r = r8 * DMA_UNROLL + k
            p = pos_ref[i * rows + r]
            pltpu.make_async_copy(
                buf.at[slot, pl.ds(pl.multiple_of(r * ROW_TILES, ROW_TILES), ROW_TILES)],
                xs_ref.at[pl.ds(pl.multiple_of(p * ROW_TILES, ROW_TILES), ROW_TILES)],
                scatter_sem.at[slot]).start()
        return carry

    lax.fori_loop(0, rows // DMA_UNROLL, issue, 0)

    @pl.when(i == n - 1)
    def _():
        wait_scatter(slot)

        @pl.when(n >= 2)
        def _():
            wait_scatter((i + DISPATCH_BUFFERS - 1) % DISPATCH_BUFFERS)


def _dispatch(pos, h2t, n_sorted_rows):
    rows = DISPATCH_ROWS
    n = h2t.shape[0] // ROW_TILES
    zeros = jnp.zeros((n_sorted_rows * ROW_TILES, LANES), F32)
    return pl.pallas_call(
        _dispatch_kernel,
        grid_spec=pltpu.PrefetchScalarGridSpec(
            num_scalar_prefetch=1,
            grid=(n // rows,),
            in_specs=[pl.BlockSpec(memory_space=pl.ANY), pl.BlockSpec(memory_space=pl.ANY)],
            out_specs=pl.BlockSpec(memory_space=pl.ANY),
            scratch_shapes=[pltpu.VMEM((DISPATCH_BUFFERS, rows * ROW_TILES, LANES), F32),
                            pltpu.SemaphoreType.DMA((DISPATCH_BUFFERS,)),
                            pltpu.SemaphoreType.DMA((DISPATCH_BUFFERS,))]),
        out_shape=jax.ShapeDtypeStruct(zeros.shape, F32),
        input_output_aliases={2: 0},
        compiler_params=_params("arbitrary"),
        name="moe_dispatch",
    )(pos, h2t, zeros)


def _expert_kernel(elo_ref, ehi_ref, valid_ref, blk_ref, x_ref, wg0, wu0, wd0, wg1, wu1, wd1, o_ref):
    del elo_ref, ehi_ref, blk_ref
    T = EXPERT_TILE
    valid = valid_ref[pl.program_id(0)] > 0

    @pl.when(jnp.logical_not(valid))
    def _():
        o_ref[...] = jnp.zeros_like(o_ref)

    @pl.when(valid)
    def _():
        x = jnp.concatenate([x_ref[pl.ds(k, T, stride=ROW_TILES), :] for k in range(ROW_TILES)],
                            axis=1).astype(BF)
        for half, (wg, wu, wd) in enumerate(((wg0, wu0, wd0), (wg1, wu1, wd1))):
            a = _dot(x, wg[...])
            b = _dot(x, wu[...])
            he = (a * jax.nn.sigmoid(a)) * b
            y = _dot(he.astype(BF), wd[...])
            for k in range(ROW_TILES):
                o_ref[pl.ds(half * ROW_TILES + k, T, stride=2 * ROW_TILES), :] = (
                    y[:, k * LANES:(k + 1) * LANES])


def _experts(tmap, xs, wg, wu, wd, n_tiles):
    T = EXPERT_TILE
    D, F = D_MODEL, EXPERT_FF
    lo = lambda shape: pl.BlockSpec((None,) + shape, lambda j, elo, ehi, v, blk: (elo[j], 0, 0))
    hi = lambda shape: pl.BlockSpec((None,) + shape, lambda j, elo, ehi, v, blk: (ehi[j], 0, 0))
    return pl.pallas_call(
        _expert_kernel,
        grid_spec=pltpu.PrefetchScalarGridSpec(
            num_scalar_prefetch=4,
            grid=(n_tiles,),
            in_specs=[pl.BlockSpec((T * ROW_TILES, LANES), lambda j, elo, ehi, v, blk: (blk[j], 0)),
                      lo((D, F)), lo((D, F)), lo((F, D)), hi((D, F)), hi((D, F)), hi((F, D))],
            out_specs=pl.BlockSpec((T * 2 * ROW_TILES, LANES), lambda j, elo, ehi, v, blk: (j, 0))),
        out_shape=jax.ShapeDtypeStruct((n_tiles * T * 2 * ROW_TILES, LANES), F32),
        compiler_params=_params("arbitrary"),
        name="moe_experts",
    )(tmap[0], tmap[1], tmap[2], tmap[3], xs, wg, wu, wd, wg, wu, wd)


def _combine_kernel(pos_ref, ys_ref, w_ref, x1_ref, mod_ref, fg_ref, o_ref, buf, sem):
    i = pl.program_id(0)
    tm = x1_ref.shape[0]
    R2 = 2 * ROW_TILES

    def gather(step, slot):
        def issue(r8, carry):
            for k in range(DMA_UNROLL):
                r = r8 * DMA_UNROLL + k
                p = pos_ref[step * tm + r]
                pltpu.make_async_copy(ys_ref.at[pl.ds(pl.multiple_of(p * R2, R2), R2)],
                                      buf.at[slot, pl.ds(pl.multiple_of(r * R2, R2), R2)],
                                      sem.at[slot]).start()
            return carry
        lax.fori_loop(0, tm // DMA_UNROLL, issue, 0)

    @pl.when(i == 0)
    def _():
        gather(0, 0)

    slot = i % 2

    @pl.when(i + 1 < pl.num_programs(0))
    def _():
        gather(i + 1, 1 - slot)

    pltpu.make_async_copy(ys_ref.at[pl.ds(0, tm * R2)], buf.at[slot], sem.at[slot]).wait()
    ylo = jnp.concatenate([buf[slot, pl.ds(k, tm, stride=R2), :] for k in range(ROW_TILES)], axis=1)
    yhi = jnp.concatenate([buf[slot, pl.ds(ROW_TILES + k, tm, stride=R2), :] for k in range(ROW_TILES)],
                          axis=1)
    y = w_ref[:, 0:1] * ylo + w_ref[:, 1:2] * yhi
    x2 = x1_ref[...] + mod_ref[5:6, :] * y
    r = lax.rsqrt(jnp.mean(x2 * x2, axis=-1, keepdims=True) + EPS)
    o_ref[...] = x2 * r * fg_ref[...]


def _combine(pos, ys, w, x1, mod, fg):
    N, D = x1.shape
    B = mod.shape[0]
    tm = COMBINE_ROWS
    per_b = (N // B) // tm
    return pl.pallas_call(
        _combine_kernel,
        grid_spec=pltpu.PrefetchScalarGridSpec(
            num_scalar_prefetch=1,
            grid=(N // tm,),
            in_specs=[pl.BlockSpec(memory_space=pl.ANY),
                      pl.BlockSpec((tm, 2), lambda i, p: (i, 0)),
                      pl.BlockSpec((tm, D), lambda i, p: (i, 0)),
                      pl.BlockSpec((None, 6, D), lambda i, p: (i // per_b, 0, 0)),
                      pl.BlockSpec((1, D), lambda i, p: (0, 0))],
            out_specs=pl.BlockSpec((tm, D), lambda i, p: (i, 0)),
            scratch_shapes=[pltpu.VMEM((2, tm * 2 * ROW_TILES, LANES), F32),
                            pltpu.SemaphoreType.DMA((2,))]),
        out_shape=jax.ShapeDtypeStruct((N, D), F32),
        compiler_params=_params("arbitrary"),
        name="moe_combine",
    )(pos, ys, w, x1, mod, fg)


def _arrange_w_in(w):
    cuts = np.cumsum([0, POOL_WIDTH, ATTN_WIDTH] + [KV_WIDTH] * 6 + [N_BRANCH * N_HEADS, 2 * D_MODEL])
    pool, q, kc, vc, ks, vs, kw, vw, bg, mg = [
        w[:, int(cuts[i]):int(cuts[i + 1])] for i in range(len(cuts) - 1)]
    bg = jnp.pad(bg, ((0, 0), (0, LANES - bg.shape[1])))
    main = jnp.concatenate([pool, q, kc, vc, ks, kw, bg, mg], axis=1).astype(BF)
    return main, jnp.concatenate([vs, vw], axis=1).T.astype(BF)


def _compress_weights(pos, w1, b1, w2, transposed):
    eye = jnp.eye(N_KV_GROUPS, dtype=F32)
    halves = CMP_BLOCK // CMP_STRIDE
    w1r = w1.reshape(halves, CMP_STRIDE, HEAD_DIM, CMP_HIDDEN)
    w1big = jnp.einsum('hidc,gk->igdkhc', w1r, eye).reshape(
        CMP_STRIDE * N_KV_GROUPS * HEAD_DIM, N_KV_GROUPS * halves * CMP_HIDDEN)
    if transposed:
        w2big = jnp.einsum('cd,gk->kdgc', w2, eye).reshape(
            N_KV_GROUPS * HEAD_DIM, N_KV_GROUPS * CMP_HIDDEN)
    else:
        w2big = jnp.einsum('cd,gk,r->gckrd', w2, eye, jnp.asarray([1.0, 0.0], F32)).reshape(
            N_KV_GROUPS * CMP_HIDDEN, N_KV_GROUPS * 2 * HEAD_DIM)
    pos8 = jnp.broadcast_to(pos.reshape(1, CMP_BLOCK * HEAD_DIM), (8, CMP_BLOCK * HEAD_DIM))
    return (w1big.astype(BF), pos8.astype(BF), w1.astype(BF), b1.reshape(1, CMP_HIDDEN),
            w2big.astype(BF))


def _selection_tables(S):
    n_chunks = S // CMP_STRIDE
    n_cmp = n_chunks - CMP_BLOCK // CMP_STRIDE + 1
    n_blk = S // SEL_BLOCK
    s1 = np.arange(n_cmp)[:, None] * CMP_STRIDE
    s2 = np.arange(n_blk)[None, :] * SEL_BLOCK
    ovl = np.clip(np.minimum(s1 + CMP_BLOCK, s2 + SEL_BLOCK) - np.maximum(s1, s2), 0, None) / CMP_BLOCK
    ovt = np.zeros((n_blk, n_chunks), np.float32)
    ovt[:, :n_cmp] = ovl.T
    return jnp.asarray(ovt, BF)


def kernel(x, c, ada_w, ada_b, norm1_g, w_in, pool_w, pool_scale, cmp_pos, cmp_w1, cmp_b1, cmp_w2,
           w_up_pool, w_up_attn, w_out, norm2_g, router_g_w, router_g_b, router_e_w, router_e_b,
           exp_w_gate, exp_w_up, exp_w_down, final_g):
    B, S, D = x.shape
    N = B * S
    assert ada_w.shape[0] == 1, "the final norm is fused into the last layer's combine step"
    for l in range(ada_w.shape[0]):
        mod = _ada(c, ada_w[l], ada_b[l]).reshape(B, 6, D)
        (up, q, kc, vc, ks, kw, bg, gm, vst, vwt) = _inproj(
            x, mod, norm1_g[l].reshape(1, D), *_arrange_w_in(w_in[l]))
        n_chunks = S // CMP_STRIDE
        kcc = _compress(kc.reshape(B, n_chunks, CMP_STRIDE * KV_WIDTH),
                        *_compress_weights(cmp_pos[l, 0], cmp_w1[l, 0], cmp_b1[l, 0], cmp_w2[l, 0], False),
                        False)
        vct = _compress(vc.reshape(B, n_chunks, CMP_STRIDE * KV_WIDTH),
                        *_compress_weights(cmp_pos[l, 1], cmp_w1[l, 1], cmp_b1[l, 1], cmp_w2[l, 1], True),
                        True)
        o = _attention(q, kcc, vct, ks, vst, kw, vwt, bg, _selection_tables(S))
        wr = jnp.zeros((8 + N_EXPERTS, D), F32)
        wr = wr.at[0:N_GROUPS].set(router_g_w[l].T).at[8:].set(router_e_w[l].T).astype(BF)
        br = jnp.zeros((8 + N_EXPERTS, 1), F32)
        br = br.at[0:N_GROUPS, 0].set(router_g_b[l]).at[8:, 0].set(router_e_b[l])
        x1, h2t, crk, rw, cnt = _mixer_out(
            up, o, gm, x, mod, pool_w[l].astype(BF), pool_scale[l].reshape(1, POOL_WIDTH),
            w_up_pool[l].astype(BF), w_up_attn[l].astype(BF), w_out[l].astype(BF),
            norm2_g[l].reshape(1, D), wr, br)
        n_tiles = N // EXPERT_TILE + N_CLASS
        n_tiles_pad = -(-n_tiles // LANES) * LANES
        pos, tmap = _plan(crk, cnt, n_tiles_pad)
        pos = pos.reshape(N)
        xs = _dispatch(pos, h2t, n_tiles * EXPERT_TILE)
        ys = _experts(tmap, xs, exp_w_gate[l].astype(BF), exp_w_up[l].astype(BF),
                      exp_w_down[l].astype(BF), n_tiles)
        w = jnp.stack([rw[:, 0, :].reshape(N), rw[:, 1, :].reshape(N)], axis=1)
        y = _combine(pos, ys, w, x1.reshape(N, D), mod, final_g.reshape(1, D))
        x = y.reshape(B, S, D)
    return x
```

```python
import functools

import numpy as np
import jax
import jax.numpy as jnp
from jax import lax
from jax.experimental import pallas as pl
from jax.experimental.pallas import tpu as pltpu

BF = jnp.bfloat16
F32 = jnp.float32
I32 = jnp.int32

D_MODEL = 1024
POOL_WIDTH = 512
POOL_WINDOWS = (2, 4, 8, 16)
POOL_GROUP = 128
POOL_HALO = 16
N_HEADS = 8
HEAD_DIM = 64
N_KV_GROUPS = 2
HEADS_PER_GROUP = 4
ATTN_WIDTH = 512
KV_WIDTH = 128
CMP_BLOCK = 32
CMP_STRIDE = 16
CMP_HIDDEN = 256
SEL_BLOCK = 64
N_SELECT = 8
WINDOW = 512
Q_CHUNK = 64
N_BRANCH = 3
N_GROUPS = 4
EXPERTS_PER_GROUP = 8
N_EXPERTS = 32
EXPERT_FF = 512
EPS = 1e-6
NEG = -1e30
FORCE_SCORE = 1e4
QK_SCALE = HEAD_DIM ** -0.5
LOG2E = 1.4426950408889634

PAIRS_PER_GROUP = EXPERTS_PER_GROUP * (EXPERTS_PER_GROUP - 1) // 2
N_CLASS = N_GROUPS * PAIRS_PER_GROUP
N_CLASS_PAD = 128
ROUTE_TILE = 512
EXPERT_TILE_LOG2 = 8
EXPERT_TILE = 1 << EXPERT_TILE_LOG2
DISPATCH_ROWS = 1024
DISPATCH_BUFFERS = 3
COMBINE_ROWS = 256
DMA_UNROLL = 8

LANES = 128
ROW_TILES = D_MODEL // LANES
SEL_KEY_TILE = 256
ATTN_BATCH = 2
DEN_ROWS = 16
VT_ROWS = HEAD_DIM + DEN_ROWS
VMEM_LIMIT = 56 * 1024 * 1024

C_POOL = 0
C_Q = C_POOL + POOL_WIDTH
C_KC = C_Q + ATTN_WIDTH
C_VC = C_KC + KV_WIDTH
C_KS = C_VC + KV_WIDTH
C_KW = C_KS + KV_WIDTH
C_BG = C_KW + KV_WIDTH
C_MG = C_BG + LANES
C_END = C_MG + 2 * D_MODEL


def _dot(a, b):
    return jnp.dot(a, b, preferred_element_type=F32)


def _dot_nt(a, b):
    return lax.dot_general(a, b, (((1,), (1,)), ((), ())), preferred_element_type=F32)


def _params(*sem):
    return pltpu.CompilerParams(dimension_semantics=sem, vmem_limit_bytes=VMEM_LIMIT)


def _ada_kernel(c_ref, w_ref, b_ref, o_ref):
    o_ref[...] = _dot(c_ref[...].astype(BF), w_ref[...].astype(BF)) + b_ref[...]


def _ada(c, w, b):
    B, D = c.shape
    n = w.shape[1]
    tn = 1024
    return pl.pallas_call(
        _ada_kernel,
        grid=(n // tn,),
        in_specs=[pl.BlockSpec((B, D), lambda j: (0, 0)),
                  pl.BlockSpec((D, tn), lambda j: (0, j)),
                  pl.BlockSpec((1, tn), lambda j: (0, j))],
        out_specs=pl.BlockSpec((B, tn), lambda j: (0, j)),
        out_shape=jax.ShapeDtypeStruct((B, n), F32),
        compiler_params=_params("arbitrary"),
        name="ada_mod",
    )(c, w, b.reshape(1, n))


def _inproj_kernel(x_ref, mod_ref, g_ref, w_ref, wvt_ref, blk_ref, up_ref, q_ref, kc_ref, vc_ref,
                   ks_ref, kw_ref, bg_ref, gm_ref, vst_ref, vwt_ref):
    x = x_ref[...]
    r = lax.rsqrt(jnp.mean(x * x, axis=-1, keepdims=True) + EPS)
    h = x * r * g_ref[...] * (1.0 + mod_ref[1:2, :]) + mod_ref[0:1, :]
    hb = h.astype(BF)

    def proj(a, b):
        return _dot(hb, w_ref[:, a:b])

    def spread(v, fill):
        pad = jnp.full((v.shape[0], HEAD_DIM), fill, v.dtype)
        pieces = []
        for i in range(v.shape[1] // HEAD_DIM):
            pieces += [v[:, i * HEAD_DIM:(i + 1) * HEAD_DIM], pad]
        return jnp.concatenate(pieces, axis=1)

    up_ref[...] = proj(C_POOL, C_Q)
    q_ref[...] = spread(proj(C_Q, C_KC) * QK_SCALE, 0.0).astype(BF)
    kc_ref[...] = proj(C_KC, C_VC).astype(BF)
    vc_ref[...] = proj(C_VC, C_KS).astype(BF)
    ks_ref[...] = (spread(proj(C_KS, C_KW) * LOG2E, 0.0) + blk_ref[...]).astype(BF)
    kw_ref[...] = spread(proj(C_KW, C_BG) * LOG2E, 0.0).astype(BF)
    vt = _dot_nt(wvt_ref[...], hb)
    ones = jnp.ones((DEN_ROWS, vt.shape[1]), BF)
    for a, ref in enumerate((vst_ref, vwt_ref)):
        for g in range(N_KV_GROUPS):
            r0 = a * KV_WIDTH + g * HEAD_DIM
            ref[g * VT_ROWS:g * VT_ROWS + HEAD_DIM, :] = vt[r0:r0 + HEAD_DIM].astype(BF)
            ref[g * VT_ROWS + HEAD_DIM:(g + 1) * VT_ROWS, :] = ones
    bg_ref[...] = proj(C_BG, C_MG)
    gm_ref[...] = jax.nn.sigmoid(proj(C_MG, C_END)).astype(BF)


def _inproj(x, mod, g, w, wvt):
    B, S, D = x.shape
    tm = 512
    blk = np.zeros((S, 2 * LANES), np.float32)
    for gg in range(N_KV_GROUPS):
        blk[np.arange(S), gg * LANES + HEAD_DIM + np.arange(S) // SEL_BLOCK] = 1.0
    blk = jnp.asarray(blk)
    widths = [(POOL_WIDTH, F32), (N_HEADS * LANES, BF), (KV_WIDTH, BF), (KV_WIDTH, BF),
              (2 * KV_WIDTH, BF), (2 * KV_WIDTH, BF), (LANES, F32), (2 * D_MODEL, BF)]
    row = lambda n: pl.BlockSpec((None, tm, n), lambda b, i: (b, i, 0))
    col = pl.BlockSpec((None, N_KV_GROUPS * VT_ROWS, tm), lambda b, i: (b, 0, i))
    return pl.pallas_call(
        _inproj_kernel,
        grid=(B, S // tm),
        in_specs=[row(D),
                  pl.BlockSpec((None, 6, D), lambda b, i: (b, 0, 0)),
                  pl.BlockSpec((1, D), lambda b, i: (0, 0)),
                  pl.BlockSpec((D, C_END), lambda b, i: (0, 0)),
                  pl.BlockSpec((2 * KV_WIDTH, D), lambda b, i: (0, 0)),
                  pl.BlockSpec((tm, 2 * LANES), lambda b, i: (i, 0))],
        out_specs=[row(n) for n, _ in widths] + [col, col],
        out_shape=[jax.ShapeDtypeStruct((B, S, n), dt) for n, dt in widths]
        + [jax.ShapeDtypeStruct((B, N_KV_GROUPS * VT_ROWS, S), BF)] * 2,
        compiler_params=_params("arbitrary", "arbitrary"),
        name="norm1_inproj",
    )(x, mod, g, w, wvt, blk)


def _gelu_tanh(x):
    return 0.5 * x * (1.0 + jnp.tanh(0.7978845608028654 * (x + 0.044715 * x * x * x)))


def _compress_kernel(transposed, x_ref, w1b_ref, pos_ref, w1_ref, b1_ref, w2b_ref, o_ref):
    y = _dot(x_ref[...], w1b_ref[...])
    posc = _dot(pos_ref[...], w1_ref[...])[0:1, :] + b1_ref[...]
    n = y.shape[0]
    acts = []
    for g in range(N_KV_GROUPS):
        first = y[:, g * 2 * CMP_HIDDEN: g * 2 * CMP_HIDDEN + CMP_HIDDEN]
        second = y[:, g * 2 * CMP_HIDDEN + CMP_HIDDEN: (g + 1) * 2 * CMP_HIDDEN]
        pre = first + pltpu.roll(second, n - 1, 0) + posc
        acts.append(_gelu_tanh(pre).astype(BF))
    act = jnp.concatenate(acts, axis=1)
    if transposed:
        vt = _dot_nt(w2b_ref[...], act)
        for g in range(N_KV_GROUPS):
            o_ref[g * VT_ROWS:g * VT_ROWS + HEAD_DIM, :] = (
                vt[g * HEAD_DIM:(g + 1) * HEAD_DIM].astype(BF))
            o_ref[g * VT_ROWS + HEAD_DIM:(g + 1) * VT_ROWS, :] = jnp.ones((DEN_ROWS, n), BF)
    else:
        o_ref[...] = _dot(act, w2b_ref[...]).astype(BF)


def _compress(xk, w1big, pos8, w1, b1, w2big, transposed):
    B, n, width = xk.shape
    full = lambda a: pl.BlockSpec(a.shape, lambda b: (0,) * a.ndim)
    out = (N_KV_GROUPS * VT_ROWS, n) if transposed else (n, 2 * KV_WIDTH)
    return pl.pallas_call(
        functools.partial(_compress_kernel, transposed),
        grid=(B,),
        in_specs=[pl.BlockSpec((None, n, width), lambda b: (b, 0, 0)),
                  full(w1big), full(pos8), full(w1), full(b1), full(w2big)],
        out_specs=pl.BlockSpec((None,) + out, lambda b: (b, 0, 0)),
        out_shape=jax.ShapeDtypeStruct((B,) + out, BF),
        compiler_params=_params("arbitrary"),
        name="compress",
    )(xk, w1big, pos8, w1, b1, w2big)


def _masked_exp(s, mask):
    sm = jnp.where(mask, s, NEG)
    m = jnp.max(sm, axis=-1, keepdims=True)
    p = jnp.where(mask, jnp.exp(sm - m), 0.0)
    return p, jnp.sum(p, axis=-1, keepdims=True)


def _safe_inv(l):
    return jnp.where(l > 0.0, 1.0 / jnp.where(l > 0.0, l, 1.0), 0.0)


def _softmax_tile(s, m_old):
    m_new = jnp.maximum(m_old, jnp.max(s, axis=-1, keepdims=True))
    return m_new, jnp.exp2(s - m_new)


def _attn_kernel(q_ref, kc_ref, vct_ref, ks_ref, vst_ref, kw_ref, vwt_ref, bg_ref, ovt_ref, o_ref):
    ci = pl.program_id(1)
    q0 = ci * Q_CHUNK
    Q, H, G = Q_CHUNK, HEADS_PER_GROUP, N_KV_GROUPS
    R = H * Q
    n_blk = ovt_ref.shape[0]
    units = [(bb, g) for bb in range(q_ref.shape[0]) for g in range(G)]
    U = len(units)
    sig = [jax.nn.sigmoid(bg_ref[bb]) for bb in range(q_ref.shape[0])]
    t_q = q0 + lax.broadcasted_iota(I32, (Q, 1), 0)
    t_r = jnp.concatenate([t_q] * H, axis=0)

    def rows4(a):
        return jnp.concatenate([a] * H, axis=0)

    def q_rows(bb, g):
        return jnp.concatenate(
            [q_ref[bb, :, (g * H + h) * LANES:(g * H + h + 1) * LANES] for h in range(H)], axis=0)

    gcs = [slice(g * LANES, (g + 1) * LANES) for g in range(G)]
    grs = [slice(g * VT_ROWS, (g + 1) * VT_ROWS) for g in range(G)]
    qp = [q_rows(bb, g) for bb, g in units]

    def pv_t(vt1, p):
        return _dot_nt(vt1, p.astype(BF))

    w0 = pl.multiple_of((jnp.maximum(q0 - WINDOW, 0) // LANES) * LANES, LANES)
    wkeys = WINDOW + 2 * Q_CHUNK
    s3 = [_dot_nt(qp[u], kw_ref[bb, pl.ds(w0, wkeys), gcs[g]]) for u, (bb, g) in enumerate(units)]
    s1 = [_dot_nt(qp[u], kc_ref[bb, :, gcs[g]]) for u, (bb, g) in enumerate(units)]

    n_idx = lax.broadcasted_iota(I32, s1[0].shape, 1)
    m1 = (n_idx * CMP_STRIDE + (CMP_BLOCK - 1)) <= t_r
    o1, psums = [], []
    for u, (bb, g) in enumerate(units):
        p1, l1 = _masked_exp(s1[u], m1)
        iv = _safe_inv(l1)
        o1.append(pv_t(vct_ref[bb, grs[g], :], p1))
        p1n = p1 * iv
        psum = p1n[0:Q]
        for h in range(1, H):
            psum = psum + p1n[h * Q:(h + 1) * Q]
        psums.append(psum)
    psum = jnp.concatenate(psums, axis=0)
    hi = psum.astype(BF)
    lo = (psum - hi.astype(F32)).astype(BF)
    ps_t = _dot_nt(ovt_ref[...], hi) + _dot_nt(ovt_ref[...], lo)

    kpos3 = w0 + lax.broadcasted_iota(I32, (Q, wkeys), 1)
    bias3 = rows4(jnp.where((kpos3 <= t_q) & (kpos3 > t_q - WINDOW), 0.0, NEG))
    win = []
    for u, (bb, g) in enumerate(units):
        _, p3 = _softmax_tile(s3[u] + bias3, jnp.full((R, 1), NEG, F32))
        win.append(pv_t(vwt_ref[bb, grs[g], pl.ds(w0, wkeys)], p3))

    sig_t = [jnp.concatenate([s, s], axis=0).T for s in sig]
    lane_lo = lax.broadcasted_iota(I32, (1, LANES), 1) < HEAD_DIM

    def gate_row(bb, g, branch):
        rows = [sig_t[bb][branch * N_HEADS + g * H + h:branch * N_HEADS + g * H + h + 1, :]
                for h in range(H)]
        return jnp.concatenate([jnp.where(lane_lo, rows[2 * k], rows[2 * k + 1])
                                for k in range(H // 2)], axis=1)

    def normalised(ot, may_be_empty):
        den = ot[HEAD_DIM:HEAD_DIM + 1, :]
        return ot[:HEAD_DIM] * (_safe_inv(den) if may_be_empty else 1.0 / den)

    early = [gate_row(bb, g, 0) * normalised(o1[u], True)
             + gate_row(bb, g, 2) * normalised(win[u], False)
             for u, (bb, g) in enumerate(units)]
    gate_sel = [gate_row(bb, g, 1) for bb, g in units]

    j = lax.broadcasted_iota(I32, ps_t.shape, 0)
    forced = (j == 0) | (j == ci) | (j == ci - 1)
    score = jnp.where(forced, FORCE_SCORE, jnp.where(j <= ci, ps_t, NEG))
    rank = jnp.zeros(ps_t.shape, I32)
    for jp in range(n_blk):
        c = score[jp:jp + 1, :]
        beats = (c > score) | ((c == score) & (j > jp))
        rank = rank + beats.astype(I32)
    bias_t = jnp.where(rank < N_SELECT, 0.0, NEG)
    pad_t = jnp.concatenate([jnp.zeros((HEAD_DIM, U * Q), F32), bias_t,
                             jnp.zeros((LANES - HEAD_DIM - n_blk, U * Q), F32)], axis=0)
    sel_bias = pad_t.T.astype(BF)

    qa = [qp[u] + rows4(sel_bias[u * Q:(u + 1) * Q]) for u in range(U)]

    def sweep(n_tiles):
        past = (n_tiles - 1) * SEL_KEY_TILE
        keys = n_tiles * SEL_KEY_TILE

        def run():
            kpos = past + lax.broadcasted_iota(I32, (Q, SEL_KEY_TILE), 1)
            bias = rows4(jnp.where(kpos <= t_q, 0.0, NEG))
            s = [_dot_nt(qa[u], ks_ref[bb, 0:keys, gcs[g]]) for u, (bb, g) in enumerate(units)]
            out = []
            for u, (bb, g) in enumerate(units):
                s_last = s[u][:, past:] + bias
                m = jnp.max(s_last, axis=-1, keepdims=True)
                if past:
                    m = jnp.maximum(m, jnp.max(s[u][:, :past], axis=-1, keepdims=True))
                acc = pv_t(vst_ref[bb, grs[g], past:keys], jnp.exp2(s_last - m))
                if past:
                    acc = acc + pv_t(vst_ref[bb, grs[g], 0:past], jnp.exp2(s[u][:, :past] - m))
                out.append(acc)
            return tuple(out)
        return run

    blocks_per_tile = SEL_KEY_TILE // SEL_BLOCK
    max_tiles = n_blk // blocks_per_tile
    sel = lax.switch(ci // blocks_per_tile, [sweep(n) for n in range(1, max_tiles + 1)])

    for u, (bb, g) in enumerate(units):
        out = (early[u] + gate_sel[u] * normalised(sel[u], False)).T
        for k in range(H // 2):
            slab = jnp.concatenate([out[(2 * k) * Q:(2 * k + 1) * Q],
                                    out[(2 * k + 1) * Q:(2 * k + 2) * Q]], axis=1)
            c0 = (g * (H // 2) + k) * LANES
            o_ref[bb, :, c0:c0 + LANES] = slab.astype(BF)


def _attention(q, kc, vc, ks, vs, kw, vw, bg, ovt):
    B, S, _ = q.shape
    nq = S // Q_CHUNK
    nb = ATTN_BATCH if B % ATTN_BATCH == 0 else 1
    per_b = lambda a: pl.BlockSpec((nb,) + a.shape[1:], lambda b, i: (b, 0, 0))
    full = lambda a: pl.BlockSpec(a.shape, lambda b, i: (0,) * a.ndim)
    return pl.pallas_call(
        _attn_kernel,
        grid=(B // nb, nq),
        in_specs=[pl.BlockSpec((nb, Q_CHUNK, N_HEADS * LANES), lambda b, i: (b, i, 0)),
                  per_b(kc), per_b(vc), per_b(ks), per_b(vs), per_b(kw), per_b(vw),
                  pl.BlockSpec((nb, Q_CHUNK, LANES), lambda b, i: (b, i, 0)),
                  full(ovt)],
        out_specs=pl.BlockSpec((nb, Q_CHUNK, ATTN_WIDTH), lambda b, i: (b, i, 0)),
        out_shape=jax.ShapeDtypeStruct((B, S, ATTN_WIDTH), BF),
        compiler_params=_params("arbitrary", "arbitrary"),
        name="nsa_attention",
    )(q, kc, vc, ks, vs, kw, vw, bg, ovt)


def _mixer_out_kernel(upc_ref, upp_ref, o_ref, gm_ref, x_ref, mod_ref, pw_ref, psc_ref,
                      wup_ref, wua_ref, wo_ref, g2_ref, wr_ref, br_ref, tri_ref,
                      x1_ref, h2t_ref, crk_ref, rw_ref, cnt_ref, cnt_scr):
    i = pl.program_id(1)

    @pl.when((pl.program_id(0) == 0) & (i == 0))
    def _():
        cnt_scr[...] = jnp.zeros_like(cnt_scr)

    tm = upc_ref.shape[0]
    prev = upp_ref[...] * (i > 0).astype(F32)
    ext = jnp.concatenate([prev, upc_ref[...]], axis=0)
    t = i * tm + lax.broadcasted_iota(I32, (tm, 1), 0)
    ys = []
    for gi, w in enumerate(POOL_WINDOWS):
        u = ext[:, gi * POOL_GROUP:(gi + 1) * POOL_GROUP]
        acc = u
        shift = 1
        while shift < w:
            acc = acc + pltpu.roll(acc, shift, 0)
            shift *= 2
        inv_cnt = 1.0 / jnp.minimum(t + 1, w).astype(F32)
        p = acc[POOL_HALO:] * inv_cnt - u[POOL_HALO:]
        ys.append(_dot(p.astype(BF), pw_ref[gi]))
    y = jnp.concatenate(ys, axis=1) * psc_ref[...]
    y_pool = _dot(y.astype(BF), wup_ref[...])
    y_attn = _dot(o_ref[...], wua_ref[...])
    gm = gm_ref[...].astype(F32)
    mix = gm[:, :D_MODEL] * y_pool + gm[:, D_MODEL:] * y_attn
    x1 = x_ref[...] + mod_ref[2:3, :] * _dot(mix.astype(BF), wo_ref[...])
    x1_ref[...] = x1
    r = lax.rsqrt(jnp.mean(x1 * x1, axis=-1, keepdims=True) + EPS)
    h2f = x1 * r * g2_ref[...] * (1.0 + mod_ref[4:5, :]) + mod_ref[3:4, :]
    h2 = h2f.astype(BF)
    for k in range(ROW_TILES):
        h2t_ref[pl.ds(k, tm, stride=ROW_TILES), :] = h2f[:, k * LANES:(k + 1) * LANES]

    lt = _dot_nt(wr_ref[...], h2) + br_ref[...]
    lg = lt[0:N_GROUPS]
    gmax = jnp.max(lg, axis=0, keepdims=True)
    gi_ = lax.broadcasted_iota(I32, lg.shape, 0)
    gidx = jnp.min(jnp.where(lg == gmax, gi_, N_GROUPS), axis=0, keepdims=True)
    gp = 1.0 / jnp.sum(jnp.exp(lg - gmax), axis=0, keepdims=True)
    E = EXPERTS_PER_GROUP
    le = jnp.zeros((E, tm), F32)
    for gg in range(N_GROUPS):
        le = jnp.where(gidx == gg, lt[8 + gg * E:8 + (gg + 1) * E], le)
    ei = lax.broadcasted_iota(I32, le.shape, 0)
    v1 = jnp.max(le, axis=0, keepdims=True)
    i1 = jnp.min(jnp.where(le == v1, ei, E), axis=0, keepdims=True)
    rest = jnp.where(ei == i1, -jnp.inf, le)
    v2 = jnp.max(rest, axis=0, keepdims=True)
    i2 = jnp.min(jnp.where(rest == v2, ei, E), axis=0, keepdims=True)
    e = jnp.exp(v2 - v1)
    wa = gp / (1.0 + e)
    wb = gp * e / (1.0 + e)
    lo = jnp.minimum(i1, i2)
    hi = jnp.maximum(i1, i2)
    pair = lax.shift_right_logical(lo * (2 * E - 1 - lo), 1) + hi - lo - 1
    cls = gidx * PAIRS_PER_GROUP + pair
    first_lo = i1 < i2
    w_lo = jnp.where(first_lo, wa, wb)
    w_hi = jnp.where(first_lo, wb, wa)
    oh = lax.broadcasted_iota(I32, (N_CLASS_PAD, tm), 0) == cls
    before = _dot(oh.astype(BF), tri_ref[...]) + cnt_scr[:, 0:1]
    rank = jnp.sum(jnp.where(oh, before, 0.0), axis=0, keepdims=True).astype(I32)
    cnt_scr[...] = cnt_scr[...] + jnp.sum(oh.astype(F32), axis=1, keepdims=True)
    cnt_ref[...] = cnt_scr[...]
    row = lax.broadcasted_iota(I32, (8, tm), 0)
    crk_ref[...] = jnp.where(row == 0, cls, jnp.where(row == 1, rank, 0))
    rw_ref[...] = jnp.where(row == 0, w_lo, jnp.where(row == 1, w_hi, 0.0))


def _mixer_out(up, o, gm, x, mod, pw, psc, wup, wua, wo, g2, wr, br):
    B, S, D = x.shape
    tm = ROUTE_TILE
    nt = S // tm
    tri = jnp.asarray(np.triu(np.ones((tm, tm), np.float32), k=1), BF)
    row = lambda n: pl.BlockSpec((None, tm, n), lambda b, i: (b, i, 0))
    full = lambda a: pl.BlockSpec(a.shape, lambda b, i: (0,) * a.ndim)
    per = tm // POOL_HALO
    return pl.pallas_call(
        _mixer_out_kernel,
        grid=(B, nt),
        in_specs=[row(POOL_WIDTH),
                  pl.BlockSpec((None, POOL_HALO, POOL_WIDTH),
                               lambda b, i: (b, jnp.maximum(i * per - 1, 0), 0)),
                  row(ATTN_WIDTH), row(2 * D_MODEL), row(D),
                  pl.BlockSpec((None, 6, D), lambda b, i: (b, 0, 0)),
                  full(pw), full(psc), full(wup), full(wua), full(wo), full(g2),
                  full(wr), full(br), full(tri)],
        out_specs=[row(D),
                   pl.BlockSpec((tm * ROW_TILES, LANES), lambda b, i: (b * nt + i, 0)),
                   pl.BlockSpec((None, 8, tm), lambda b, i: (b * nt + i, 0, 0)),
                   pl.BlockSpec((None, 8, tm), lambda b, i: (b * nt + i, 0, 0)),
                   pl.BlockSpec((N_CLASS_PAD, LANES), lambda b, i: (0, 0))],
        out_shape=[jax.ShapeDtypeStruct((B, S, D), F32),
                   jax.ShapeDtypeStruct((B * S * ROW_TILES, LANES), F32),
                   jax.ShapeDtypeStruct((B * nt, 8, tm), I32),
                   jax.ShapeDtypeStruct((B * nt, 8, tm), F32),
                   jax.ShapeDtypeStruct((N_CLASS_PAD, LANES), F32)],
        scratch_shapes=[pltpu.VMEM((N_CLASS_PAD, LANES), F32)],
        compiler_params=_params("arbitrary", "arbitrary"),
        name="mixer_out_router",
    )(up, up, o, gm, x, mod, pw, psc, wup, wua, wo, g2, wr, br, tri)


def _plan_kernel(crk_ref, cnt_ref, etab_ref, pos_ref, tmap_ref):
    C = N_CLASS_PAD
    cnt = cnt_ref[:, 0:1].astype(I32)
    ntile = lax.shift_right_logical(cnt + (EXPERT_TILE - 1), EXPERT_TILE_LOG2)
    ntile_f = ntile.astype(F32)
    r = lax.broadcasted_iota(I32, (C, C), 0)
    c = lax.broadcasted_iota(I32, (C, C), 1)
    lower = (c < r).astype(BF)
    first = _dot(lower, jnp.broadcast_to(ntile_f, (C, LANES)).astype(BF))[:, 0:1]
    last = first + ntile_f
    total = jnp.sum(ntile_f, axis=0, keepdims=True)
    off = (first * EXPERT_TILE).astype(I32)

    def body(i, carry):
        cls = crk_ref[i, 0:1, :]
        rank = crk_ref[i, 1:2, :]
        oh = lax.broadcasted_iota(I32, (C, cls.shape[1]), 0) == cls
        pos_ref[pl.ds(i, 1), :] = jnp.sum(jnp.where(oh, off, 0), axis=0, keepdims=True) + rank
        return carry

    lax.fori_loop(0, crk_ref.shape[0], body, 0)

    nj = tmap_ref.shape[1]
    j = lax.broadcasted_iota(I32, (1, nj), 1).astype(F32)
    jj = jnp.minimum(j, total - 1.0)
    tcls = jnp.sum((last <= jj).astype(I32), axis=0, keepdims=True)
    oh2 = lax.broadcasted_iota(I32, (C, nj), 0) == tcls
    elo = jnp.sum(jnp.where(oh2, etab_ref[:, 0:1], 0), axis=0, keepdims=True)
    ehi = jnp.sum(jnp.where(oh2, etab_ref[:, 1:2], 0), axis=0, keepdims=True)
    row = lax.broadcasted_iota(I32, (8, nj), 0)
    tmap_ref[...] = jnp.where(
        row == 0, elo, jnp.where(row == 1, ehi, jnp.where(
            row == 2, (j < total).astype(I32), jnp.where(row == 3, jj.astype(I32), 0))))


def _plan(crk, cnt, n_tiles_pad):
    nt, _, tm = crk.shape
    etab = np.zeros((N_CLASS_PAD, LANES), np.int32)
    cid = 0
    for g in range(N_GROUPS):
        for lo in range(EXPERTS_PER_GROUP):
            for hi in range(lo + 1, EXPERTS_PER_GROUP):
                etab[cid, 0] = g * EXPERTS_PER_GROUP + lo
                etab[cid, 1] = g * EXPERTS_PER_GROUP + hi
                cid += 1
    etab = jnp.asarray(etab)
    full = lambda a: pl.BlockSpec(a.shape, lambda i: (0,) * a.ndim)
    return pl.pallas_call(
        _plan_kernel,
        grid=(1,),
        in_specs=[full(crk), full(cnt), full(etab)],
        out_specs=[pl.BlockSpec((nt, tm), lambda i: (0, 0)),
                   pl.BlockSpec((8, n_tiles_pad), lambda i: (0, 0))],
        out_shape=[jax.ShapeDtypeStruct((nt, tm), I32),
                   jax.ShapeDtypeStruct((8, n_tiles_pad), I32)],
        compiler_params=_params("arbitrary"),
        name="moe_plan",
    )(crk, cnt, etab)


def _dispatch_kernel(pos_ref, h_ref, xs_in_ref, xs_ref, buf, load_sem, scatter_sem):
    del xs_in_ref
    i = pl.program_id(0)
    n = pl.num_programs(0)
    rows = DISPATCH_ROWS
    step_rows = rows * ROW_TILES

    def load(step, slot):
        start = pl.multiple_of(step * step_rows, step_rows)
        return pltpu.make_async_copy(h_ref.at[pl.ds(start, step_rows)], buf.at[slot],
                                     load_sem.at[slot])

    def wait_scatter(slot):
        pltpu.make_async_copy(buf.at[slot], xs_ref.at[pl.ds(0, step_rows)],
                              scatter_sem.at[slot]).wait()

    slot = i % DISPATCH_BUFFERS
    nslot = (i + 1) % DISPATCH_BUFFERS

    @pl.when(i == 0)
    def _():
        load(0, 0).start()

    @pl.when(i >= DISPATCH_BUFFERS - 1)
    def _():
        wait_scatter(nslot)

    @pl.when(i + 1 < n)
    def _():
        load(i + 1, nslot).start()

    load(i, slot).wait()

    def issue(r8, carry):
        for k in range(DMA_UNROLL):
            r = r8 * DMA_UNROLL + k
            p = pos_ref[i * rows + r]
            pltpu.make_async_copy(
                buf.at[slot, pl.ds(pl.multiple_of(r * ROW_TILES, ROW_TILES), ROW_TILES)],
                xs_ref.at[pl.ds(pl.multiple_of(p * ROW_TILES, ROW_TILES), ROW_TILES)],
                scatter_sem.at[slot]).start()
        return carry

    lax.fori_loop(0, rows // DMA_UNROLL, issue, 0)

    @pl.when(i == n - 1)
    def _():
        wait_scatter(slot)

        @pl.when(n >= 2)
        def _():
            wait_scatter((i + DISPATCH_BUFFERS - 1) % DISPATCH_BUFFERS)


def _dispatch(pos, h2t, n_sorted_rows):
    rows = DISPATCH_ROWS
    n = h2t.shape[0] // ROW_TILES
    zeros = jnp.zeros((n_sorted_rows * ROW_TILES, LANES), F32)
    return pl.pallas_call(
        _dispatch_kernel,
        grid_spec=pltpu.PrefetchScalarGridSpec(
            num_scalar_prefetch=1,
            grid=(n // rows,),
            in_specs=[pl.BlockSpec(memory_space=pl.ANY), pl.BlockSpec(memory_space=pl.ANY)],
            out_specs=pl.BlockSpec(memory_space=pl.ANY),
            scratch_shapes=[pltpu.VMEM((DISPATCH_BUFFERS, rows * ROW_TILES, LANES), F32),
                            pltpu.SemaphoreType.DMA((DISPATCH_BUFFERS,)),
                            pltpu.SemaphoreType.DMA((DISPATCH_BUFFERS,))]),
        out_shape=jax.ShapeDtypeStruct(zeros.shape, F32),
        input_output_aliases={2: 0},
        compiler_params=_params("arbitrary"),
        name="moe_dispatch",
    )(pos, h2t, zeros)


def _expert_kernel(elo_ref, ehi_ref, valid_ref, blk_ref, x_ref, wg0, wu0, wd0, wg1, wu1, wd1, o_ref):
    del elo_ref, ehi_ref, blk_ref
    T = EXPERT_TILE
    valid = valid_ref[pl.program_id(0)] > 0

    @pl.when(jnp.logical_not(valid))
    def _():
        o_ref[...] = jnp.zeros_like(o_ref)

    @pl.when(valid)
    def _():
        x = jnp.concatenate([x_ref[pl.ds(k, T, stride=ROW_TILES), :] for k in range(ROW_TILES)],
                            axis=1).astype(BF)
        for half, (wg, wu, wd) in enumerate(((wg0, wu0, wd0), (wg1, wu1, wd1))):
            a = _dot(x, wg[...])
            b = _dot(x, wu[...])
            he = (a * jax.nn.sigmoid(a)) * b
            y = _dot(he.astype(BF), wd[...])
            for k in range(ROW_TILES):
                o_ref[pl.ds(half * ROW_TILES + k, T, stride=2 * ROW_TILES), :] = (
                    y[:, k * LANES:(k + 1) * LANES])


def _experts(tmap, xs, wg, wu, wd, n_tiles):
    T = EXPERT_TILE
    D, F = D_MODEL, EXPERT_FF
    lo = lambda shape: pl.BlockSpec((None,) + shape, lambda j, elo, ehi, v, blk: (elo[j], 0, 0))
    hi = lambda shape: pl.BlockSpec((None,) + shape, lambda j, elo, ehi, v, blk: (ehi[j], 0, 0))
    return pl.pallas_call(
        _expert_kernel,
        grid_spec=pltpu.PrefetchScalarGridSpec(
            num_scalar_prefetch=4,
            grid=(n_tiles,),
            in_specs=[pl.BlockSpec((T * ROW_TILES, LANES), lambda j, elo, ehi, v, blk: (blk[j], 0)),
                      lo((D, F)), lo((D, F)), lo((F, D)), hi((D, F)), hi((D, F)), hi((F, D))],
            out_specs=pl.BlockSpec((T * 2 * ROW_TILES, LANES), lambda j, elo, ehi, v, blk: (j, 0))),
        out_shape=jax.ShapeDtypeStruct((n_tiles * T * 2 * ROW_TILES, LANES), F32),
        compiler_params=_params("arbitrary"),
        name="moe_experts",
    )(tmap[0], tmap[1], tmap[2], tmap[3], xs, wg, wu, wd, wg, wu, wd)


def _combine_kernel(pos_ref, ys_ref, w_ref, x1_ref, mod_ref, fg_ref, o_ref, buf, sem):
    i = pl.program_id(0)
    tm = x1_ref.shape[0]
    R2 = 2 * ROW_TILES

    def gather(step, slot):
        def issue(r8, carry):
            for k in range(DMA_UNROLL):
                r = r8 * DMA_UNROLL + k
                p = pos_ref[step * tm + r]
                pltpu.make_async_copy(ys_ref.at[pl.ds(pl.multiple_of(p * R2, R2), R2)],
                                      buf.at[slot, pl.ds(pl.multiple_of(r * R2, R2), R2)],
                                      sem.at[slot]).start()
            return carry
        lax.fori_loop(0, tm // DMA_UNROLL, issue, 0)

    @pl.when(i == 0)
    def _():
        gather(0, 0)

    slot = i % 2

    @pl.when(i + 1 < pl.num_programs(0))
    def _():
        gather(i + 1, 1 - slot)

    pltpu.make_async_copy(ys_ref.at[pl.ds(0, tm * R2)], buf.at[slot], sem.at[slot]).wait()
    ylo = jnp.concatenate([buf[slot, pl.ds(k, tm, stride=R2), :] for k in range(ROW_TILES)], axis=1)
    yhi = jnp.concatenate([buf[slot, pl.ds(ROW_TILES + k, tm, stride=R2), :] for k in range(ROW_TILES)],
                          axis=1)
    y = w_ref[:, 0:1] * ylo + w_ref[:, 1:2] * yhi
    x2 = x1_ref[...] + mod_ref[5:6, :] * y
    r = lax.rsqrt(jnp.mean(x2 * x2, axis=-1, keepdims=True) + EPS)
    o_ref[...] = x2 * r * fg_ref[...]


def _combine(pos, ys, w, x1, mod, fg):
    N, D = x1.shape
    B = mod.shape[0]
    tm = COMBINE_ROWS
    per_b = (N // B) // tm
    return pl.pallas_call(
        _combine_kernel,
        grid_spec=pltpu.PrefetchScalarGridSpec(
            num_scalar_prefetch=1,
            grid=(N // tm,),
            in_specs=[pl.BlockSpec(memory_space=pl.ANY),
                      pl.BlockSpec((tm, 2), lambda i, p: (i, 0)),
                      pl.BlockSpec((tm, D), lambda i, p: (i, 0)),
                      pl.BlockSpec((None, 6, D), lambda i, p: (i // per_b, 0, 0)),
                      pl.BlockSpec((1, D), lambda i, p: (0, 0))],
            out_specs=pl.BlockSpec((tm, D), lambda i, p: (i, 0)),
            scratch_shapes=[pltpu.VMEM((2, tm * 2 * ROW_TILES, LANES), F32),
                            pltpu.SemaphoreType.DMA((2,))]),
        out_shape=jax.ShapeDtypeStruct((N, D), F32),
        compiler_params=_params("arbitrary"),
        name="moe_combine",
    )(pos, ys, w, x1, mod, fg)


def _arrange_w_in(w):
    cuts = np.cumsum([0, POOL_WIDTH, ATTN_WIDTH] + [KV_WIDTH] * 6 + [N_BRANCH * N_HEADS, 2 * D_MODEL])
    pool, q, kc, vc, ks, vs, kw, vw, bg, mg = [
        w[:, int(cuts[i]):int(cuts[i + 1])] for i in range(len(cuts) - 1)]
    bg = jnp.pad(bg, ((0, 0), (0, LANES - bg.shape[1])))
    main = jnp.concatenate([pool, q, kc, vc, ks, kw, bg, mg], axis=1).astype(BF)
    return main, jnp.concatenate([vs, vw], axis=1).T.astype(BF)


def _compress_weights(pos, w1, b1, w2, transposed):
    eye = jnp.eye(N_KV_GROUPS, dtype=F32)
    halves = CMP_BLOCK // CMP_STRIDE
    w1r = w1.reshape(halves, CMP_STRIDE, HEAD_DIM, CMP_HIDDEN)
    w1big = jnp.einsum('hidc,gk->igdkhc', w1r, eye).reshape(
        CMP_STRIDE * N_KV_GROUPS * HEAD_DIM, N_KV_GROUPS * halves * CMP_HIDDEN)
    if transposed:
        w2big = jnp.einsum('cd,gk->kdgc', w2, eye).reshape(
            N_KV_GROUPS * HEAD_DIM, N_KV_GROUPS * CMP_HIDDEN)
    else:
        w2big = jnp.einsum('cd,gk,r->gckrd', w2, eye, jnp.asarray([1.0, 0.0], F32)).reshape(
            N_KV_GROUPS * CMP_HIDDEN, N_KV_GROUPS * 2 * HEAD_DIM)
    pos8 = jnp.broadcast_to(pos.reshape(1, CMP_BLOCK * HEAD_DIM), (8, CMP_BLOCK * HEAD_DIM))
    return (w1big.astype(BF), pos8.astype(BF), w1.astype(BF), b1.reshape(1, CMP_HIDDEN),
            w2big.astype(BF))


def _selection_tables(S):
    n_chunks = S // CMP_STRIDE
    n_cmp = n_chunks - CMP_BLOCK // CMP_STRIDE + 1
    n_blk = S // SEL_BLOCK
    s1 = np.arange(n_cmp)[:, None] * CMP_STRIDE
    s2 = np.arange(n_blk)[None, :] * SEL_BLOCK
    ovl = np.clip(np.minimum(s1 + CMP_BLOCK, s2 + SEL_BLOCK) - np.maximum(s1, s2), 0, None) / CMP_BLOCK
    ovt = np.zeros((n_blk, n_chunks), np.float32)
    ovt[:, :n_cmp] = ovl.T
    return jnp.asarray(ovt, BF)


def kernel(x, c, ada_w, ada_b, norm1_g, w_in, pool_w, pool_scale, cmp_pos, cmp_w1, cmp_b1, cmp_w2,
           w_up_pool, w_up_attn, w_out, norm2_g, router_g_w, router_g_b, router_e_w, router_e_b,
           exp_w_gate, exp_w_up, exp_w_down, final_g):
    B, S, D = x.shape
    N = B * S
    assert ada_w.shape[0] == 1, "the final norm is fused into the last layer's combine step"
    for l in range(ada_w.shape[0]):
        mod = _ada(c, ada_w[l], ada_b[l]).reshape(B, 6, D)
        (up, q, kc, vc, ks, kw, bg, gm, vst, vwt) = _inproj(
            x, mod, norm1_g[l].reshape(1, D), *_arrange_w_in(w_in[l]))
        n_chunks = S // CMP_STRIDE
        kcc = _compress(kc.reshape(B, n_chunks, CMP_STRIDE * KV_WIDTH),
                        *_compress_weights(cmp_pos[l, 0], cmp_w1[l, 0], cmp_b1[l, 0], cmp_w2[l, 0], False),
                        False)
        vct = _compress(vc.reshape(B, n_chunks, CMP_STRIDE * KV_WIDTH),
                        *_compress_weights(cmp_pos[l, 1], cmp_w1[l, 1], cmp_b1[l, 1], cmp_w2[l, 1], True),
                        True)
        o = _attention(q, kcc, vct, ks, vst, kw, vwt, bg, _selection_tables(S))
        wr = jnp.zeros((8 + N_EXPERTS, D), F32)
        wr = wr.at[0:N_GROUPS].set(router_g_w[l].T).at[8:].set(router_e_w[l].T).astype(BF)
        br = jnp.zeros((8 + N_EXPERTS, 1), F32)
        br = br.at[0:N_GROUPS, 0].set(router_g_b[l]).at[8:, 0].set(router_e_b[l])
        x1, h2t, crk, rw, cnt = _mixer_out(
            up, o, gm, x, mod, pool_w[l].astype(BF), pool_scale[l].reshape(1, POOL_WIDTH),
            w_up_pool[l].astype(BF), w_up_attn[l].astype(BF), w_out[l].astype(BF),
            norm2_g[l].reshape(1, D), wr, br)
        n_tiles = N // EXPERT_TILE + N_CLASS
        n_tiles_pad = -(-n_tiles // LANES) * LANES
        pos, tmap = _plan(crk, cnt, n_tiles_pad)
        pos = pos.reshape(N)
        xs = _dispatch(pos, h2t, n_tiles * EXPERT_TILE)
        ys = _experts(tmap, xs, exp_w_gate[l].astype(BF), exp_w_up[l].astype(BF),
                      exp_w_down[l].astype(BF), n_tiles)
        w = jnp.stack([rw[:, 0, :].reshape(N), rw[:, 1, :].reshape(N)], axis=1)
        y = _combine(pos, ys, w, x1.reshape(N, D), mod, final_g.reshape(1, D))
        x = y.reshape(B, S, D)
    return x
```

```python
import functools

import numpy as np
import jax
import jax.numpy as jnp
from jax import lax
from jax.experimental import pallas as pl
from jax.experimental.pallas import tpu as pltpu

BF = jnp.bfloat16
F32 = jnp.float32
I32 = jnp.int32

D_MODEL = 1024
POOL_WIDTH = 512
POOL_WINDOWS = (2, 4, 8, 16)
POOL_GROUP = 128
POOL_HALO = 16
N_HEADS = 8
HEAD_DIM = 64
N_KV_GROUPS = 2
HEADS_PER_GROUP = 4
ATTN_WIDTH = 512
KV_WIDTH = 128
CMP_BLOCK = 32
CMP_STRIDE = 16
CMP_HIDDEN = 256
SEL_BLOCK = 64
N_SELECT = 8
WINDOW = 512
Q_CHUNK = 64
N_BRANCH = 3
N_GROUPS = 4
EXPERTS_PER_GROUP = 8
N_EXPERTS = 32
EXPERT_FF = 512
EPS = 1e-6
NEG = -1e30
FORCE_SCORE = 1e4
QK_SCALE = HEAD_DIM ** -0.5
LOG2E = 1.4426950408889634

PAIRS_PER_GROUP = EXPERTS_PER_GROUP * (EXPERTS_PER_GROUP - 1) // 2
N_CLASS = N_GROUPS * PAIRS_PER_GROUP
N_CLASS_PAD = 128
ROUTE_TILE = 512
EXPERT_TILE_LOG2 = 8
EXPERT_TILE = 1 << EXPERT_TILE_LOG2
DISPATCH_ROWS = 1024
DISPATCH_BUFFERS = 3
COMBINE_ROWS = 256
DMA_UNROLL = 8

LANES = 128
ROW_TILES = D_MODEL // LANES
SEL_KEY_TILE = 256
ATTN_BATCH = 2
DEN_ROWS = 16
VT_ROWS = HEAD_DIM + DEN_ROWS
VMEM_LIMIT = 56 * 1024 * 1024

C_POOL = 0
C_Q = C_POOL + POOL_WIDTH
C_KC = C_Q + ATTN_WIDTH
C_VC = C_KC + KV_WIDTH
C_KS = C_VC + KV_WIDTH
C_KW = C_KS + KV_WIDTH
C_BG = C_KW + KV_WIDTH
C_MG = C_BG + LANES
C_END = C_MG + 2 * D_MODEL


def _dot(a, b):
    return jnp.dot(a, b, preferred_element_type=F32)


def _dot_nt(a, b):
    return lax.dot_general(a, b, (((1,), (1,)), ((), ())), preferred_element_type=F32)


def _params(*sem):
    return pltpu.CompilerParams(dimension_semantics=sem, vmem_limit_bytes=VMEM_LIMIT)


def _ada_kernel(c_ref, w_ref, b_ref, o_ref):
    o_ref[...] = _dot(c_ref[...].astype(BF), w_ref[...].astype(BF)) + b_ref[...]


def _ada(c, w, b):
    B, D = c.shape
    n = w.shape[1]
    tn = 1024
    return pl.pallas_call(
        _ada_kernel,
        grid=(n // tn,),
        in_specs=[pl.BlockSpec((B, D), lambda j: (0, 0)),
                  pl.BlockSpec((D, tn), lambda j: (0, j)),
                  pl.BlockSpec((1, tn), lambda j: (0, j))],
        out_specs=pl.BlockSpec((B, tn), lambda j: (0, j)),
        out_shape=jax.ShapeDtypeStruct((B, n), F32),
        compiler_params=_params("arbitrary"),
        name="ada_mod",
    )(c, w, b.reshape(1, n))


def _inproj_kernel(x_ref, mod_ref, g_ref, w_ref, wvt_ref, blk_ref, up_ref, q_ref, kc_ref, vc_ref,
                   ks_ref, kw_ref, bg_ref, gm_ref, vst_ref, vwt_ref, stage_ref):
    x = x_ref[...]
    r = lax.rsqrt(jnp.mean(x * x, axis=-1, keepdims=True) + EPS)
    h = x * r * g_ref[...] * (1.0 + mod_ref[1:2, :]) + mod_ref[0:1, :]
    hb = h.astype(BF)

    def proj(a, b):
        return _dot(hb, w_ref[:, a:b])

    def spread(v, fill):
        pad = jnp.full((v.shape[0], HEAD_DIM), fill, v.dtype)
        pieces = []
        for i in range(v.shape[1] // HEAD_DIM):
            pieces += [v[:, i * HEAD_DIM:(i + 1) * HEAD_DIM], pad]
        return jnp.concatenate(pieces, axis=1)

    up_ref[...] = proj(C_POOL, C_Q)
    q_ref[...] = spread(proj(C_Q, C_KC) * QK_SCALE, 0.0).astype(BF)
    for src, ref in ((C_KC, kc_ref), (C_VC, vc_ref)):
        stage_ref[...] = proj(src, src + KV_WIDTH)
        for t in range(CMP_STRIDE):
            ref[:, t * KV_WIDTH:(t + 1) * KV_WIDTH] = (
                stage_ref[pl.ds(t, ref.shape[0], stride=CMP_STRIDE), :].astype(BF))
    ks_ref[...] = (spread(proj(C_KS, C_KW) * LOG2E, 0.0) + blk_ref[...]).astype(BF)
    kw_ref[...] = spread(proj(C_KW, C_BG) * LOG2E, 0.0).astype(BF)
    vt = _dot_nt(wvt_ref[...], hb)
    ones = jnp.ones((DEN_ROWS, vt.shape[1]), BF)
    for a, ref in enumerate((vst_ref, vwt_ref)):
        for g in range(N_KV_GROUPS):
            r0 = a * KV_WIDTH + g * HEAD_DIM
            ref[g * VT_ROWS:g * VT_ROWS + HEAD_DIM, :] = vt[r0:r0 + HEAD_DIM].astype(BF)
            ref[g * VT_ROWS + HEAD_DIM:(g + 1) * VT_ROWS, :] = ones
    bg_ref[...] = proj(C_BG, C_MG)
    gm_ref[...] = jax.nn.sigmoid(proj(C_MG, C_END)).astype(BF)


def _inproj(x, mod, g, w, wvt):
    B, S, D = x.shape
    tm = 512
    blk = np.zeros((S, 2 * LANES), np.float32)
    for gg in range(N_KV_GROUPS):
        blk[np.arange(S), gg * LANES + HEAD_DIM + np.arange(S) // SEL_BLOCK] = 1.0
    blk = jnp.asarray(blk)
    outs = [(1, POOL_WIDTH, F32), (1, N_HEADS * LANES, BF),
            (CMP_STRIDE, CMP_STRIDE * KV_WIDTH, BF), (CMP_STRIDE, CMP_STRIDE * KV_WIDTH, BF),
            (1, 2 * KV_WIDTH, BF), (1, 2 * KV_WIDTH, BF), (1, LANES, F32), (1, 2 * D_MODEL, BF)]
    row = lambda n, per=1: pl.BlockSpec((None, tm // per, n), lambda b, i: (b, i, 0))
    col = pl.BlockSpec((None, N_KV_GROUPS * VT_ROWS, tm), lambda b, i: (b, 0, i))
    return pl.pallas_call(
        _inproj_kernel,
        grid=(B, S // tm),
        in_specs=[row(D),
                  pl.BlockSpec((None, 6, D), lambda b, i: (b, 0, 0)),
                  pl.BlockSpec((1, D), lambda b, i: (0, 0)),
                  pl.BlockSpec((D, C_END), lambda b, i: (0, 0)),
                  pl.BlockSpec((2 * KV_WIDTH, D), lambda b, i: (0, 0)),
                  pl.BlockSpec((tm, 2 * LANES), lambda b, i: (i, 0))],
        out_specs=[row(n, per) for per, n, _ in outs] + [col, col],
        out_shape=[jax.ShapeDtypeStruct((B, S // per, n), dt) for per, n, dt in outs]
        + [jax.ShapeDtypeStruct((B, N_KV_GROUPS * VT_ROWS, S), BF)] * 2,
        scratch_shapes=[pltpu.VMEM((tm, KV_WIDTH), F32)],
        compiler_params=_params("arbitrary", "arbitrary"),
        name="norm1_inproj",
    )(x, mod, g, w, wvt, blk)


def _gelu_tanh(x):
    return 0.5 * x * (1.0 + jnp.tanh(0.7978845608028654 * (x + 0.044715 * x * x * x)))


def _compress_kernel(transposed, x_ref, w1b_ref, pos_ref, w1_ref, b1_ref, w2b_ref, o_ref):
    y = _dot(x_ref[...], w1b_ref[...])
    posc = _dot(pos_ref[...], w1_ref[...])[0:1, :] + b1_ref[...]
    n = y.shape[0]
    acts = []
    for g in range(N_KV_GROUPS):
        first = y[:, g * 2 * CMP_HIDDEN: g * 2 * CMP_HIDDEN + CMP_HIDDEN]
        second = y[:, g * 2 * CMP_HIDDEN + CMP_HIDDEN: (g + 1) * 2 * CMP_HIDDEN]
        pre = first + pltpu.roll(second, n - 1, 0) + posc
        acts.append(_gelu_tanh(pre).astype(BF))
    act = jnp.concatenate(acts, axis=1)
    if transposed:
        vt = _dot_nt(w2b_ref[...], act)
        for g in range(N_KV_GROUPS):
            o_ref[g * VT_ROWS:g * VT_ROWS + HEAD_DIM, :] = (
                vt[g * HEAD_DIM:(g + 1) * HEAD_DIM].astype(BF))
            o_ref[g * VT_ROWS + HEAD_DIM:(g + 1) * VT_ROWS, :] = jnp.ones((DEN_ROWS, n), BF)
    else:
        o_ref[...] = _dot(act, w2b_ref[...]).astype(BF)


def _compress(xk, w1big, pos8, w1, b1, w2big, transposed):
    B, n, width = xk.shape
    full = lambda a: pl.BlockSpec(a.shape, lambda b: (0,) * a.ndim)
    out = (N_KV_GROUPS * VT_ROWS, n) if transposed else (n, 2 * KV_WIDTH)
    return pl.pallas_call(
        functools.partial(_compress_kernel, transposed),
        grid=(B,),
        in_specs=[pl.BlockSpec((None, n, width), lambda b: (b, 0, 0)),
                  full(w1big), full(pos8), full(w1), full(b1), full(w2big)],
        out_specs=pl.BlockSpec((None,) + out, lambda b: (b, 0, 0)),
        out_shape=jax.ShapeDtypeStruct((B,) + out, BF),
        compiler_params=_params("arbitrary"),
        name="compress",
    )(xk, w1big, pos8, w1, b1, w2big)


def _masked_exp(s, mask):
    sm = jnp.where(mask, s, NEG)
    m = jnp.max(sm, axis=-1, keepdims=True)
    p = jnp.where(mask, jnp.exp(sm - m), 0.0)
    return p, jnp.sum(p, axis=-1, keepdims=True)


def _safe_inv(l):
    return jnp.where(l > 0.0, 1.0 / jnp.where(l > 0.0, l, 1.0), 0.0)


def _softmax_tile(s, m_old):
    m_new = jnp.maximum(m_old, jnp.max(s, axis=-1, keepdims=True))
    return m_new, jnp.exp2(s - m_new)


def _attn_kernel(q_ref, kc_ref, vct_ref, ks_ref, vst_ref, kw_ref, vwt_ref, bg_ref, ovt_ref, o_ref):
    ci = pl.program_id(1)
    q0 = ci * Q_CHUNK
    Q, H, G = Q_CHUNK, HEADS_PER_GROUP, N_KV_GROUPS
    R = H * Q
    n_blk = ovt_ref.shape[0]
    units = [(bb, g) for bb in range(q_ref.shape[0]) for g in range(G)]
    U = len(units)
    sig = [jax.nn.sigmoid(bg_ref[bb]) for bb in range(q_ref.shape[0])]
    t_q = q0 + lax.broadcasted_iota(I32, (Q, 1), 0)
    t_r = jnp.concatenate([t_q] * H, axis=0)

    def rows4(a):
        return jnp.concatenate([a] * H, axis=0)

    def q_rows(bb, g):
        return jnp.concatenate(
            [q_ref[bb, :, (g * H + h) * LANES:(g * H + h + 1) * LANES] for h in range(H)], axis=0)

    gcs = [slice(g * LANES, (g + 1) * LANES) for g in range(G)]
    grs = [slice(g * VT_ROWS, (g + 1) * VT_ROWS) for g in range(G)]
    qp = [q_rows(bb, g) for bb, g in units]

    def pv_t(vt1, p):
        return _dot_nt(vt1, p.astype(BF))

    w0 = pl.multiple_of((jnp.maximum(q0 - WINDOW, 0) // LANES) * LANES, LANES)
    wkeys = WINDOW + 2 * Q_CHUNK
    s3 = [_dot_nt(qp[u], kw_ref[bb, pl.ds(w0, wkeys), gcs[g]]) for u, (bb, g) in enumerate(units)]
    s1 = [_dot_nt(qp[u], kc_ref[bb, :, gcs[g]]) for u, (bb, g) in enumerate(units)]

    n_idx = lax.broadcasted_iota(I32, s1[0].shape, 1)
    m1 = (n_idx * CMP_STRIDE + (CMP_BLOCK - 1)) <= t_r
    o1, psums = [], []
    for u, (bb, g) in enumerate(units):
        p1, l1 = _masked_exp(s1[u], m1)
        iv = _safe_inv(l1)
        o1.append(pv_t(vct_ref[bb, grs[g], :], p1))
        p1n = p1 * iv
        psum = p1n[0:Q]
        for h in range(1, H):
            psum = psum + p1n[h * Q:(h + 1) * Q]
        psums.append(psum)
    psum = jnp.concatenate(psums, axis=0)
    hi = psum.astype(BF)
    lo = (psum - hi.astype(F32)).astype(BF)
    ps_t = _dot_nt(ovt_ref[...], hi) + _dot_nt(ovt_ref[...], lo)

    kpos3 = w0 + lax.broadcasted_iota(I32, (Q, wkeys), 1)
    bias3 = rows4(jnp.where((kpos3 <= t_q) & (kpos3 > t_q - WINDOW), 0.0, NEG))
    win = []
    for u, (bb, g) in enumerate(units):
        _, p3 = _softmax_tile(s3[u] + bias3, jnp.full((R, 1), NEG, F32))
        win.append(pv_t(vwt_ref[bb, grs[g], pl.ds(w0, wkeys)], p3))

    sig_t = [jnp.concatenate([s, s], axis=0).T for s in sig]
    lane_lo = lax.broadcasted_iota(I32, (1, LANES), 1) < HEAD_DIM

    def gate_row(bb, g, branch):
        rows = [sig_t[bb][branch * N_HEADS + g * H + h:branch * N_HEADS + g * H + h + 1, :]
                for h in range(H)]
        return jnp.concatenate([jnp.where(lane_lo, rows[2 * k], rows[2 * k + 1])
                                for k in range(H // 2)], axis=1)

    def normalised(ot, may_be_empty):
        den = ot[HEAD_DIM:HEAD_DIM + 1, :]
        return ot[:HEAD_DIM] * (_safe_inv(den) if may_be_empty else 1.0 / den)

    early = [gate_row(bb, g, 0) * normalised(o1[u], True)
             + gate_row(bb, g, 2) * normalised(win[u], False)
             for u, (bb, g) in enumerate(units)]
    gate_sel = [gate_row(bb, g, 1) for bb, g in units]

    j = lax.broadcasted_iota(I32, ps_t.shape, 0)
    forced = (j == 0) | (j == ci) | (j == ci - 1)
    score = jnp.where(forced, FORCE_SCORE, jnp.where(j <= ci, ps_t, NEG))
    rank = jnp.zeros(ps_t.shape, I32)
    for jp in range(n_blk):
        c = score[jp:jp + 1, :]
        beats = (c > score) | ((c == score) & (j > jp))
        rank = rank + beats.astype(I32)
    bias_t = jnp.where(rank < N_SELECT, 0.0, NEG)
    pad_t = jnp.concatenate([jnp.zeros((HEAD_DIM, U * Q), F32), bias_t,
                             jnp.zeros((LANES - HEAD_DIM - n_blk, U * Q), F32)], axis=0)
    sel_bias = pad_t.T.astype(BF)

    qa = [qp[u] + rows4(sel_bias[u * Q:(u + 1) * Q]) for u in range(U)]

    def sweep(n_tiles):
        past = (n_tiles - 1) * SEL_KEY_TILE
        keys = n_tiles * SEL_KEY_TILE

        def run():
            kpos = past + lax.broadcasted_iota(I32, (Q, SEL_KEY_TILE), 1)
            bias = rows4(jnp.where(kpos <= t_q, 0.0, NEG))
            s = [_dot_nt(qa[u], ks_ref[bb, 0:keys, gcs[g]]) for u, (bb, g) in enumerate(units)]
            out = []
            for u, (bb, g) in enumerate(units):
                s_last = s[u][:, past:] + bias
                m = jnp.max(s_last, axis=-1, keepdims=True)
                if past:
                    m = jnp.maximum(m, jnp.max(s[u][:, :past], axis=-1, keepdims=True))
                acc = pv_t(vst_ref[bb, grs[g], past:keys], jnp.exp2(s_last - m))
                if past:
                    acc = acc + pv_t(vst_ref[bb, grs[g], 0:past], jnp.exp2(s[u][:, :past] - m))
                out.append(acc)
            return tuple(out)
        return run

    blocks_per_tile = SEL_KEY_TILE // SEL_BLOCK
    max_tiles = n_blk // blocks_per_tile
    sel = lax.switch(ci // blocks_per_tile, [sweep(n) for n in range(1, max_tiles + 1)])

    for u, (bb, g) in enumerate(units):
        out = (early[u] + gate_sel[u] * normalised(sel[u], False)).T
        for k in range(H // 2):
            slab = jnp.concatenate([out[(2 * k) * Q:(2 * k + 1) * Q],
                                    out[(2 * k + 1) * Q:(2 * k + 2) * Q]], axis=1)
            c0 = (g * (H // 2) + k) * LANES
            o_ref[bb, :, c0:c0 + LANES] = slab.astype(BF)


def _attention(q, kc, vc, ks, vs, kw, vw, bg, ovt):
    B, S, _ = q.shape
    nq = S // Q_CHUNK
    nb = ATTN_BATCH if B % ATTN_BATCH == 0 else 1
    per_b = lambda a: pl.BlockSpec((nb,) + a.shape[1:], lambda b, i: (b, 0, 0))
    full = lambda a: pl.BlockSpec(a.shape, lambda b, i: (0,) * a.ndim)
    return pl.pallas_call(
        _attn_kernel,
        grid=(B // nb, nq),
        in_specs=[pl.BlockSpec((nb, Q_CHUNK, N_HEADS * LANES), lambda b, i: (b, i, 0)),
                  per_b(kc), per_b(vc), per_b(ks), per_b(vs), per_b(kw), per_b(vw),
                  pl.BlockSpec((nb, Q_CHUNK, LANES), lambda b, i: (b, i, 0)),
                  full(ovt)],
        out_specs=pl.BlockSpec((nb, Q_CHUNK, ATTN_WIDTH), lambda b, i: (b, i, 0)),
        out_shape=jax.ShapeDtypeStruct((B, S, ATTN_WIDTH), BF),
        compiler_params=_params("arbitrary", "arbitrary"),
        name="nsa_attention",
    )(q, kc, vc, ks, vs, kw, vw, bg, ovt)


def _mixer_out_kernel(upc_ref, upp_ref, o_ref, gm_ref, x_ref, mod_ref, pw_ref, psc_ref,
                      wup_ref, wua_ref, wo_ref, g2_ref, wr_ref, br_ref, tri_ref,
                      x1_ref, h2t_ref, crk_ref, rw_ref, cnt_ref, cnt_scr):
    i = pl.program_id(1)

    @pl.when((pl.program_id(0) == 0) & (i == 0))
    def _():
        cnt_scr[...] = jnp.zeros_like(cnt_scr)

    tm = upc_ref.shape[0]
    prev = upp_ref[...] * (i > 0).astype(F32)
    ext = jnp.concatenate([prev, upc_ref[...]], axis=0)
    t = i * tm + lax.broadcasted_iota(I32, (tm, 1), 0)
    ys = []
    for gi, w in enumerate(POOL_WINDOWS):
        u = ext[:, gi * POOL_GROUP:(gi + 1) * POOL_GROUP]
        acc = u
        shift = 1
        while shift < w:
            acc = acc + pltpu.roll(acc, shift, 0)
            shift *= 2
        inv_cnt = 1.0 / jnp.minimum(t + 1, w).astype(F32)
        p = acc[POOL_HALO:] * inv_cnt - u[POOL_HALO:]
        ys.append(_dot(p.astype(BF), pw_ref[gi]))
    y = jnp.concatenate(ys, axis=1) * psc_ref[...]
    y_pool = _dot(y.astype(BF), wup_ref[...])
    y_attn = _dot(o_ref[...], wua_ref[...])
    gm = gm_ref[...].astype(F32)
    mix = gm[:, :D_MODEL] * y_pool + gm[:, D_MODEL:] * y_attn
    x1 = x_ref[...] + mod_ref[2:3, :] * _dot(mix.astype(BF), wo_ref[...])
    x1_ref[...] = x1
    r = lax.rsqrt(jnp.mean(x1 * x1, axis=-1, keepdims=True) + EPS)
    h2f = x1 * r * g2_ref[...] * (1.0 + mod_ref[4:5, :]) + mod_ref[3:4, :]
    h2 = h2f.astype(BF)
    for k in range(ROW_TILES):
        h2t_ref[pl.ds(k, tm, stride=ROW_TILES), :] = h2f[:, k * LANES:(k + 1) * LANES]

    lt = _dot_nt(wr_ref[...], h2) + br_ref[...]
    lg = lt[0:N_GROUPS]
    gmax = jnp.max(lg, axis=0, keepdims=True)
    gi_ = lax.broadcasted_iota(I32, lg.shape, 0)
    gidx = jnp.min(jnp.where(lg == gmax, gi_, N_GROUPS), axis=0, keepdims=True)
    gp = 1.0 / jnp.sum(jnp.exp(lg - gmax), axis=0, keepdims=True)
    E = EXPERTS_PER_GROUP
    le = jnp.zeros((E, tm), F32)
    for gg in range(N_GROUPS):
        le = jnp.where(gidx == gg, lt[8 + gg * E:8 + (gg + 1) * E], le)
    ei = lax.broadcasted_iota(I32, le.shape, 0)
    v1 = jnp.max(le, axis=0, keepdims=True)
    i1 = jnp.min(jnp.where(le == v1, ei, E), axis=0, keepdims=True)
    rest = jnp.where(ei == i1, -jnp.inf, le)
    v2 = jnp.max(rest, axis=0, keepdims=True)
    i2 = jnp.min(jnp.where(rest == v2, ei, E), axis=0, keepdims=True)
    e = jnp.exp(v2 - v1)
    wa = gp / (1.0 + e)
    wb = gp * e / (1.0 + e)
    lo = jnp.minimum(i1, i2)
    hi = jnp.maximum(i1, i2)
    pair = lax.shift_right_logical(lo * (2 * E - 1 - lo), 1) + hi - lo - 1
    cls = gidx * PAIRS_PER_GROUP + pair
    first_lo = i1 < i2
    w_lo = jnp.where(first_lo, wa, wb)
    w_hi = jnp.where(first_lo, wb, wa)
    oh = lax.broadcasted_iota(I32, (N_CLASS_PAD, tm), 0) == cls
    before = _dot(oh.astype(BF), tri_ref[...]) + cnt_scr[:, 0:1]
    rank = jnp.sum(jnp.where(oh, before, 0.0), axis=0, keepdims=True).astype(I32)
    cnt_scr[...] = cnt_scr[...] + jnp.sum(oh.astype(F32), axis=1, keepdims=True)
    cnt_ref[...] = cnt_scr[...]
    row = lax.broadcasted_iota(I32, (8, tm), 0)
    crk_ref[...] = jnp.where(row == 0, cls, jnp.where(row == 1, rank, 0))
    rw_ref[...] = jnp.where(row == 0, w_lo, jnp.where(row == 1, w_hi, 0.0)).T


def _mixer_out(up, o, gm, x, mod, pw, psc, wup, wua, wo, g2, wr, br):
    B, S, D = x.shape
    tm = ROUTE_TILE
    nt = S // tm
    tri = jnp.asarray(np.triu(np.ones((tm, tm), np.float32), k=1), BF)
    row = lambda n: pl.BlockSpec((None, tm, n), lambda b, i: (b, i, 0))
    full = lambda a: pl.BlockSpec(a.shape, lambda b, i: (0,) * a.ndim)
    per = tm // POOL_HALO
    return pl.pallas_call(
        _mixer_out_kernel,
        grid=(B, nt),
        in_specs=[row(POOL_WIDTH),
                  pl.BlockSpec((None, POOL_HALO, POOL_WIDTH),
                               lambda b, i: (b, jnp.maximum(i * per - 1, 0), 0)),
                  row(ATTN_WIDTH), row(2 * D_MODEL), row(D),
                  pl.BlockSpec((None, 6, D), lambda b, i: (b, 0, 0)),
                  full(pw), full(psc), full(wup), full(wua), full(wo), full(g2),
                  full(wr), full(br), full(tri)],
        out_specs=[row(D),
                   pl.BlockSpec((tm * ROW_TILES, LANES), lambda b, i: (b * nt + i, 0)),
                   pl.BlockSpec((None, 8, tm), lambda b, i: (b * nt + i, 0, 0)),
                   pl.BlockSpec((tm, 8), lambda b, i: (b * nt + i, 0)),
                   pl.BlockSpec((N_CLASS_PAD, LANES), lambda b, i: (0, 0))],
        out_shape=[jax.ShapeDtypeStruct((B, S, D), F32),
                   jax.ShapeDtypeStruct((B * S * ROW_TILES, LANES), F32),
                   jax.ShapeDtypeStruct((B * nt, 8, tm), I32),
                   jax.ShapeDtypeStruct((B * S, 8), F32),
                   jax.ShapeDtypeStruct((N_CLASS_PAD, LANES), F32)],
        scratch_shapes=[pltpu.VMEM((N_CLASS_PAD, LANES), F32)],
        compiler_params=_params("arbitrary", "arbitrary"),
        name="mixer_out_router",
    )(up, up, o, gm, x, mod, pw, psc, wup, wua, wo, g2, wr, br, tri)


def _plan_kernel(crk_ref, cnt_ref, etab_ref, pos_ref, tmap_ref):
    C = N_CLASS_PAD
    cnt = cnt_ref[:, 0:1].astype(I32)
    ntile = lax.shift_right_logical(cnt + (EXPERT_TILE - 1), EXPERT_TILE_LOG2)
    ntile_f = ntile.astype(F32)
    r = lax.broadcasted_iota(I32, (C, C), 0)
    c = lax.broadcasted_iota(I32, (C, C), 1)
    lower = (c < r).astype(BF)
    first = _dot(lower, jnp.broadcast_to(ntile_f, (C, LANES)).astype(BF))[:, 0:1]
    last = first + ntile_f
    total = jnp.sum(ntile_f, axis=0, keepdims=True)
    off = (first * EXPERT_TILE).astype(I32)

    def body(i, carry):
        cls = crk_ref[i, 0:1, :]
        rank = crk_ref[i, 1:2, :]
        oh = lax.broadcasted_iota(I32, (C, cls.shape[1]), 0) == cls
        pos_ref[pl.ds(i, 1), :] = jnp.sum(jnp.where(oh, off, 0), axis=0, keepdims=True) + rank
        return carry

    lax.fori_loop(0, crk_ref.shape[0], body, 0)

    nj = tmap_ref.shape[1]
    j = lax.broadcasted_iota(I32, (1, nj), 1).astype(F32)
    jj = jnp.minimum(j, total - 1.0)
    tcls = jnp.sum((last <= jj).astype(I32), axis=0, keepdims=True)
    oh2 = lax.broadcasted_iota(I32, (C, nj), 0) == tcls
    elo = jnp.sum(jnp.where(oh2, etab_ref[:, 0:1], 0), axis=0, keepdims=True)
    ehi = jnp.sum(jnp.where(oh2, etab_ref[:, 1:2], 0), axis=0, keepdims=True)
    row = lax.broadcasted_iota(I32, (8, nj), 0)
    tmap_ref[...] = jnp.where(
        row == 0, elo, jnp.where(row == 1, ehi, jnp.where(
            row == 2, (j < total).astype(I32), jnp.where(row == 3, jj.astype(I32), 0))))


def _plan(crk, cnt, n_tiles_pad):
    nt, _, tm = crk.shape
    etab = np.zeros((N_CLASS_PAD, LANES), np.int32)
    cid = 0
    for g in range(N_GROUPS):
        for lo in range(EXPERTS_PER_GROUP):
            for hi in range(lo + 1, EXPERTS_PER_GROUP):
                etab[cid, 0] = g * EXPERTS_PER_GROUP + lo
                etab[cid, 1] = g * EXPERTS_PER_GROUP + hi
                cid += 1
    etab = jnp.asarray(etab)
    full = lambda a: pl.BlockSpec(a.shape, lambda i: (0,) * a.ndim)
    return pl.pallas_call(
        _plan_kernel,
        grid=(1,),
        in_specs=[full(crk), full(cnt), full(etab)],
        out_specs=[pl.BlockSpec((nt, tm), lambda i: (0, 0)),
                   pl.BlockSpec((8, n_tiles_pad), lambda i: (0, 0))],
        out_shape=[jax.ShapeDtypeStruct((nt, tm), I32),
                   jax.ShapeDtypeStruct((8, n_tiles_pad), I32)],
        compiler_params=_params("arbitrary"),
        name="moe_plan",
    )(crk, cnt, etab)


def _dispatch_kernel(pos_ref, h_ref, xs_in_ref, xs_ref, buf, load_sem, scatter_sem):
    del xs_in_ref
    i = pl.program_id(0)
    n = pl.num_programs(0)
    rows = DISPATCH_ROWS
    step_rows = rows * ROW_TILES

    def load(step, slot):
        start = pl.multiple_of(step * step_rows, step_rows)
        return pltpu.make_async_copy(h_ref.at[pl.ds(start, step_rows)], buf.at[slot],
                                     load_sem.at[slot])

    def wait_scatter(slot):
        pltpu.make_async_copy(buf.at[slot], xs_ref.at[pl.ds(0, step_rows)],
                              scatter_sem.at[slot]).wait()

    slot = i % DISPATCH_BUFFERS
    nslot = (i + 1) % DISPATCH_BUFFERS

    @pl.when(i == 0)
    def _():
        load(0, 0).start()

    @pl.when(i >= DISPATCH_BUFFERS - 1)
    def _():
        wait_scatter(nslot)

    @pl.when(i + 1 < n)
    def _():
        load(i + 1, nslot).start()

    load(i, slot).wait()

    def issue(r8, carry):
        for k in range(DMA_UNROLL):
            r = r8 * DMA_UNROLL + k
            p = pos_ref[i * rows + r]
            pltpu.make_async_copy(
                buf.at[slot, pl.ds(pl.multiple_of(r * ROW_TILES, ROW_TILES), ROW_TILES)],
                xs_ref.at[pl.ds(pl.multiple_of(p * ROW_TILES, ROW_TILES), ROW_TILES)],
                scatter_sem.at[slot]).start()
        return carry

    lax.fori_loop(0, rows // DMA_UNROLL, issue, 0)

    @pl.when(i == n - 1)
    def _():
        wait_scatter(slot)

        @pl.when(n >= 2)
        def _():
            wait_scatter((i + DISPATCH_BUFFERS - 1) % DISPATCH_BUFFERS)


def _dispatch(pos, h2t, n_sorted_rows):
    rows = DISPATCH_ROWS
    n = h2t.shape[0] // ROW_TILES
    zeros = jnp.zeros((n_sorted_rows * ROW_TILES, LANES), F32)
    return pl.pallas_call(
        _dispatch_kernel,
        grid_spec=pltpu.PrefetchScalarGridSpec(
            num_scalar_prefetch=1,
            grid=(n // rows,),
            in_specs=[pl.BlockSpec(memory_space=pl.ANY), pl.BlockSpec(memory_space=pl.ANY)],
            out_specs=pl.BlockSpec(memory_space=pl.ANY),
            scratch_shapes=[pltpu.VMEM((DISPATCH_BUFFERS, rows * ROW_TILES, LANES), F32),
                            pltpu.SemaphoreType.DMA((DISPATCH_BUFFERS,)),
                            pltpu.SemaphoreType.DMA((DISPATCH_BUFFERS,))]),
        out_shape=jax.ShapeDtypeStruct(zeros.shape, F32),
        input_output_aliases={2: 0},
        compiler_params=_params("arbitrary"),
        name="moe_dispatch",
    )(pos, h2t, zeros)


def _expert_kernel(elo_ref, ehi_ref, valid_ref, blk_ref, x_ref, wg0, wu0, wd0, wg1, wu1, wd1, o_ref):
    del elo_ref, ehi_ref, blk_ref
    T = EXPERT_TILE
    valid = valid_ref[pl.program_id(0)] > 0

    @pl.when(jnp.logical_not(valid))
    def _():
        o_ref[...] = jnp.zeros_like(o_ref)

    @pl.when(valid)
    def _():
        x = jnp.concatenate([x_ref[pl.ds(k, T, stride=ROW_TILES), :] for k in range(ROW_TILES)],
                            axis=1).astype(BF)
        for half, (wg, wu, wd) in enumerate(((wg0, wu0, wd0), (wg1, wu1, wd1))):
            a = _dot(x, wg[...])
            b = _dot(x, wu[...])
            he = (a * jax.nn.sigmoid(a)) * b
            y = _dot(he.astype(BF), wd[...])
            for k in range(ROW_TILES):
                o_ref[pl.ds(half * ROW_TILES + k, T, stride=2 * ROW_TILES), :] = (
                    y[:, k * LANES:(k + 1) * LANES])


def _experts(tmap, xs, wg, wu, wd, n_tiles):
    T = EXPERT_TILE
    D, F = D_MODEL, EXPERT_FF
    lo = lambda shape: pl.BlockSpec((None,) + shape, lambda j, elo, ehi, v, blk: (elo[j], 0, 0))
    hi = lambda shape: pl.BlockSpec((None,) + shape, lambda j, elo, ehi, v, blk: (ehi[j], 0, 0))
    return pl.pallas_call(
        _expert_kernel,
        grid_spec=pltpu.PrefetchScalarGridSpec(
            num_scalar_prefetch=4,
            grid=(n_tiles,),
            in_specs=[pl.BlockSpec((T * ROW_TILES, LANES), lambda j, elo, ehi, v, blk: (blk[j], 0)),
                      lo((D, F)), lo((D, F)), lo((F, D)), hi((D, F)), hi((D, F)), hi((F, D))],
            out_specs=pl.BlockSpec((T * 2 * ROW_TILES, LANES), lambda j, elo, ehi, v, blk: (j, 0))),
        out_shape=jax.ShapeDtypeStruct((n_tiles * T * 2 * ROW_TILES, LANES), F32),
        compiler_params=_params("arbitrary"),
        name="moe_experts",
    )(tmap[0], tmap[1], tmap[2], tmap[3], xs, wg, wu, wd, wg, wu, wd)


def _combine_kernel(pos_ref, ys_ref, w_ref, x1_ref, mod_ref, fg_ref, o_ref, buf, sem):
    i = pl.program_id(0)
    tm = x1_ref.shape[0]
    R2 = 2 * ROW_TILES

    def gather(step, slot):
        def issue(r8, carry):
            for k in range(DMA_UNROLL):
                r = r8 * DMA_UNROLL + k
                p = pos_ref[step * tm + r]
                pltpu.make_async_copy(ys_ref.at[pl.ds(pl.multiple_of(p * R2, R2), R2)],
                                      buf.at[slot, pl.ds(pl.multiple_of(r * R2, R2), R2)],
                                      sem.at[slot]).start()
            return carry
        lax.fori_loop(0, tm // DMA_UNROLL, issue, 0)

    @pl.when(i == 0)
    def _():
        gather(0, 0)

    slot = i % 2

    @pl.when(i + 1 < pl.num_programs(0))
    def _():
        gather(i + 1, 1 - slot)

    pltpu.make_async_copy(ys_ref.at[pl.ds(0, tm * R2)], buf.at[slot], sem.at[slot]).wait()
    ylo = jnp.concatenate([buf[slot, pl.ds(k, tm, stride=R2), :] for k in range(ROW_TILES)], axis=1)
    yhi = jnp.concatenate([buf[slot, pl.ds(ROW_TILES + k, tm, stride=R2), :] for k in range(ROW_TILES)],
                          axis=1)
    y = w_ref[:, 0:1] * ylo + w_ref[:, 1:2] * yhi
    x2 = x1_ref[...] + mod_ref[5:6, :] * y
    r = lax.rsqrt(jnp.mean(x2 * x2, axis=-1, keepdims=True) + EPS)
    o_ref[...] = x2 * r * fg_ref[...]


def _combine(pos, ys, w, x1, mod, fg):
    N, D = x1.shape
    B = mod.shape[0]
    tm = COMBINE_ROWS
    per_b = (N // B) // tm
    return pl.pallas_call(
        _combine_kernel,
        grid_spec=pltpu.PrefetchScalarGridSpec(
            num_scalar_prefetch=1,
            grid=(N // tm,),
            in_specs=[pl.BlockSpec(memory_space=pl.ANY),
                      pl.BlockSpec((tm, w.shape[1]), lambda i, p: (i, 0)),
                      pl.BlockSpec((tm, D), lambda i, p: (i, 0)),
                      pl.BlockSpec((None, 6, D), lambda i, p: (i // per_b, 0, 0)),
                      pl.BlockSpec((1, D), lambda i, p: (0, 0))],
            out_specs=pl.BlockSpec((tm, D), lambda i, p: (i, 0)),
            scratch_shapes=[pltpu.VMEM((2, tm * 2 * ROW_TILES, LANES), F32),
                            pltpu.SemaphoreType.DMA((2,))]),
        out_shape=jax.ShapeDtypeStruct((N, D), F32),
        compiler_params=_params("arbitrary"),
        name="moe_combine",
    )(pos, ys, w, x1, mod, fg)


def _arrange_w_in(w):
    cuts = np.cumsum([0, POOL_WIDTH, ATTN_WIDTH] + [KV_WIDTH] * 6 + [N_BRANCH * N_HEADS, 2 * D_MODEL])
    pool, q, kc, vc, ks, vs, kw, vw, bg, mg = [
        w[:, int(cuts[i]):int(cuts[i + 1])] for i in range(len(cuts) - 1)]
    bg = jnp.pad(bg, ((0, 0), (0, LANES - bg.shape[1])))
    main = jnp.concatenate([pool, q, kc, vc, ks, kw, bg, mg], axis=1).astype(BF)
    return main, jnp.concatenate([vs, vw], axis=1).T.astype(BF)


def _compress_weights(pos, w1, b1, w2, transposed):
    eye = jnp.eye(N_KV_GROUPS, dtype=F32)
    halves = CMP_BLOCK // CMP_STRIDE
    w1r = w1.reshape(halves, CMP_STRIDE, HEAD_DIM, CMP_HIDDEN)
    w1big = jnp.einsum('hidc,gk->igdkhc', w1r, eye).reshape(
        CMP_STRIDE * N_KV_GROUPS * HEAD_DIM, N_KV_GROUPS * halves * CMP_HIDDEN)
    if transposed:
        w2big = jnp.einsum('cd,gk->kdgc', w2, eye).reshape(
            N_KV_GROUPS * HEAD_DIM, N_KV_GROUPS * CMP_HIDDEN)
    else:
        w2big = jnp.einsum('cd,gk,r->gckrd', w2, eye, jnp.asarray([1.0, 0.0], F32)).reshape(
            N_KV_GROUPS * CMP_HIDDEN, N_KV_GROUPS * 2 * HEAD_DIM)
    pos8 = jnp.broadcast_to(pos.reshape(1, CMP_BLOCK * HEAD_DIM), (8, CMP_BLOCK * HEAD_DIM))
    return (w1big.astype(BF), pos8.astype(BF), w1.astype(BF), b1.reshape(1, CMP_HIDDEN),
            w2big.astype(BF))


def _selection_tables(S):
    n_chunks = S // CMP_STRIDE
    n_cmp = n_chunks - CMP_BLOCK // CMP_STRIDE + 1
    n_blk = S // SEL_BLOCK
    s1 = np.arange(n_cmp)[:, None] * CMP_STRIDE
    s2 = np.arange(n_blk)[None, :] * SEL_BLOCK
    ovl = np.clip(np.minimum(s1 + CMP_BLOCK, s2 + SEL_BLOCK) - np.maximum(s1, s2), 0, None) / CMP_BLOCK
    ovt = np.zeros((n_blk, n_chunks), np.float32)
    ovt[:, :n_cmp] = ovl.T
    return jnp.asarray(ovt, BF)


def kernel(x, c, ada_w, ada_b, norm1_g, w_in, pool_w, pool_scale, cmp_pos, cmp_w1, cmp_b1, cmp_w2,
           w_up_pool, w_up_attn, w_out, norm2_g, router_g_w, router_g_b, router_e_w, router_e_b,
           exp_w_gate, exp_w_up, exp_w_down, final_g):
    B, S, D = x.shape
    N = B * S
    assert ada_w.shape[0] == 1, "the final norm is fused into the last layer's combine step"
    for l in range(ada_w.shape[0]):
        mod = _ada(c, ada_w[l], ada_b[l]).reshape(B, 6, D)
        (up, q, kc, vc, ks, kw, bg, gm, vst, vwt) = _inproj(
            x, mod, norm1_g[l].reshape(1, D), *_arrange_w_in(w_in[l]))
        kcc = _compress(kc, *_compress_weights(cmp_pos[l, 0], cmp_w1[l, 0], cmp_b1[l, 0],
                                               cmp_w2[l, 0], False), False)
        vct = _compress(vc, *_compress_weights(cmp_pos[l, 1], cmp_w1[l, 1], cmp_b1[l, 1],
                                               cmp_w2[l, 1], True), True)
        o = _attention(q, kcc, vct, ks, vst, kw, vwt, bg, _selection_tables(S))
        wr = jnp.zeros((8 + N_EXPERTS, D), F32)
        wr = wr.at[0:N_GROUPS].set(router_g_w[l].T).at[8:].set(router_e_w[l].T).astype(BF)
        br = jnp.zeros((8 + N_EXPERTS, 1), F32)
        br = br.at[0:N_GROUPS, 0].set(router_g_b[l]).at[8:, 0].set(router_e_b[l])
        x1, h2t, crk, rw, cnt = _mixer_out(
            up, o, gm, x, mod, pool_w[l].astype(BF), pool_scale[l].reshape(1, POOL_WIDTH),
            w_up_pool[l].astype(BF), w_up_attn[l].astype(BF), w_out[l].astype(BF),
            norm2_g[l].reshape(1, D), wr, br)
        n_tiles = N // EXPERT_TILE + N_CLASS
        n_tiles_pad = -(-n_tiles // LANES) * LANES
        pos, tmap = _plan(crk, cnt, n_tiles_pad)
        pos = pos.reshape(N)
        xs = _dispatch(pos, h2t, n_tiles * EXPERT_TILE)
        ys = _experts(tmap, xs, exp_w_gate[l].astype(BF), exp_w_up[l].astype(BF),
                      exp_w_down[l].astype(BF), n_tiles)
        y = _combine(pos, ys, rw, x1.reshape(N, D), mod, final_g.reshape(1, D))
        x = y.reshape(B, S, D)
    return x
```

```python
import functools

import numpy as np
import jax
import jax.numpy as jnp
from jax import lax
from jax.experimental import pallas as pl
from jax.experimental.pallas import tpu as pltpu

BF = jnp.bfloat16
F32 = jnp.float32
I32 = jnp.int32

D_MODEL = 1024
POOL_WIDTH = 512
POOL_WINDOWS = (2, 4, 8, 16)
POOL_GROUP = 128
POOL_HALO = 16
N_HEADS = 8
HEAD_DIM = 64
N_KV_GROUPS = 2
HEADS_PER_GROUP = 4
ATTN_WIDTH = 512
KV_WIDTH = 128
CMP_BLOCK = 32
CMP_STRIDE = 16
CMP_HIDDEN = 256
SEL_BLOCK = 64
N_SELECT = 8
WINDOW = 512
Q_CHUNK = 64
N_BRANCH = 3
N_GROUPS = 4
EXPERTS_PER_GROUP = 8
N_EXPERTS = 32
EXPERT_FF = 512
EPS = 1e-6
NEG = -1e30
FORCE_SCORE = 1e4
QK_SCALE = HEAD_DIM ** -0.5
LOG2E = 1.4426950408889634

PAIRS_PER_GROUP = EXPERTS_PER_GROUP * (EXPERTS_PER_GROUP - 1) // 2
N_CLASS = N_GROUPS * PAIRS_PER_GROUP
N_CLASS_PAD = 128
ROUTE_TILE = 512
EXPERT_TILE_LOG2 = 8
EXPERT_TILE = 1 << EXPERT_TILE_LOG2
DISPATCH_ROWS = 1024
DISPATCH_BUFFERS = 3
COMBINE_ROWS = 512
DMA_UNROLL = 8

LANES = 128
ROW_TILES = D_MODEL // LANES
SEL_KEY_TILE = 256
ATTN_BATCH = 2
DEN_ROWS = 16
VT_ROWS = HEAD_DIM + DEN_ROWS
VMEM_LIMIT = 56 * 1024 * 1024

C_POOL = 0
C_Q = C_POOL + POOL_WIDTH
C_KC = C_Q + ATTN_WIDTH
C_VC = C_KC + KV_WIDTH
C_KS = C_VC + KV_WIDTH
C_KW = C_KS + KV_WIDTH
C_BG = C_KW + KV_WIDTH
C_MG = C_BG + LANES
C_END = C_MG + 2 * D_MODEL


def _dot(a, b):
    return jnp.dot(a, b, preferred_element_type=F32)


def _dot_nt(a, b):
    return lax.dot_general(a, b, (((1,), (1,)), ((), ())), preferred_element_type=F32)


def _params(*sem):
    return pltpu.CompilerParams(dimension_semantics=sem, vmem_limit_bytes=VMEM_LIMIT)


def _ada_kernel(c_ref, w_ref, b_ref, o_ref):
    o_ref[...] = _dot(c_ref[...].astype(BF), w_ref[...].astype(BF)) + b_ref[...]


def _ada(c, w, b):
    B, D = c.shape
    n = w.shape[1]
    tn = 1024
    return pl.pallas_call(
        _ada_kernel,
        grid=(n // tn,),
        in_specs=[pl.BlockSpec((B, D), lambda j: (0, 0)),
                  pl.BlockSpec((D, tn), lambda j: (0, j)),
                  pl.BlockSpec((1, tn), lambda j: (0, j))],
        out_specs=pl.BlockSpec((B, tn), lambda j: (0, j)),
        out_shape=jax.ShapeDtypeStruct((B, n), F32),
        compiler_params=_params("arbitrary"),
        name="ada_mod",
    )(c, w, b.reshape(1, n))


def _inproj_kernel(x_ref, mod_ref, g_ref, w_ref, wvt_ref, blk_ref, up_ref, q_ref, kc_ref, vc_ref,
                   ks_ref, kw_ref, bg_ref, gm_ref, vst_ref, vwt_ref, stage_ref):
    x = x_ref[...]
    r = lax.rsqrt(jnp.mean(x * x, axis=-1, keepdims=True) + EPS)
    h = x * r * g_ref[...] * (1.0 + mod_ref[1:2, :]) + mod_ref[0:1, :]
    hb = h.astype(BF)

    def proj(a, b):
        return _dot(hb, w_ref[:, a:b])

    def spread(v, fill):
        pad = jnp.full((v.shape[0], HEAD_DIM), fill, v.dtype)
        pieces = []
        for i in range(v.shape[1] // HEAD_DIM):
            pieces += [v[:, i * HEAD_DIM:(i + 1) * HEAD_DIM], pad]
        return jnp.concatenate(pieces, axis=1)

    up_ref[...] = proj(C_POOL, C_Q)
    q_ref[...] = spread(proj(C_Q, C_KC) * QK_SCALE, 0.0).astype(BF)
    for src, ref in ((C_KC, kc_ref), (C_VC, vc_ref)):
        stage_ref[...] = proj(src, src + KV_WIDTH)
        for t in range(CMP_STRIDE):
            ref[:, t * KV_WIDTH:(t + 1) * KV_WIDTH] = (
                stage_ref[pl.ds(t, ref.shape[0], stride=CMP_STRIDE), :].astype(BF))
    ks_ref[...] = (spread(proj(C_KS, C_KW) * LOG2E, 0.0) + blk_ref[...]).astype(BF)
    kw_ref[...] = spread(proj(C_KW, C_BG) * LOG2E, 0.0).astype(BF)
    vt = _dot_nt(wvt_ref[...], hb)
    ones = jnp.ones((DEN_ROWS, vt.shape[1]), BF)
    for a, ref in enumerate((vst_ref, vwt_ref)):
        for g in range(N_KV_GROUPS):
            r0 = a * KV_WIDTH + g * HEAD_DIM
            ref[g * VT_ROWS:g * VT_ROWS + HEAD_DIM, :] = vt[r0:r0 + HEAD_DIM].astype(BF)
            ref[g * VT_ROWS + HEAD_DIM:(g + 1) * VT_ROWS, :] = ones
    bg_ref[...] = proj(C_BG, C_MG)
    gm_ref[...] = jax.nn.sigmoid(proj(C_MG, C_END)).astype(BF)


def _inproj(x, mod, g, w, wvt):
    B, S, D = x.shape
    tm = 512
    blk = np.zeros((S, 2 * LANES), np.float32)
    for gg in range(N_KV_GROUPS):
        blk[np.arange(S), gg * LANES + HEAD_DIM + np.arange(S) // SEL_BLOCK] = 1.0
    blk = jnp.asarray(blk)
    outs = [(1, POOL_WIDTH, F32), (1, N_HEADS * LANES, BF),
            (CMP_STRIDE, CMP_STRIDE * KV_WIDTH, BF), (CMP_STRIDE, CMP_STRIDE * KV_WIDTH, BF),
            (1, 2 * KV_WIDTH, BF), (1, 2 * KV_WIDTH, BF), (1, LANES, F32), (1, 2 * D_MODEL, BF)]
    row = lambda n, per=1: pl.BlockSpec((None, tm // per, n), lambda b, i: (b, i, 0))
    col = pl.BlockSpec((None, N_KV_GROUPS * VT_ROWS, tm), lambda b, i: (b, 0, i))
    return pl.pallas_call(
        _inproj_kernel,
        grid=(B, S // tm),
        in_specs=[row(D),
                  pl.BlockSpec((None, 6, D), lambda b, i: (b, 0, 0)),
                  pl.BlockSpec((1, D), lambda b, i: (0, 0)),
                  pl.BlockSpec((D, C_END), lambda b, i: (0, 0)),
                  pl.BlockSpec((2 * KV_WIDTH, D), lambda b, i: (0, 0)),
                  pl.BlockSpec((tm, 2 * LANES), lambda b, i: (i, 0))],
        out_specs=[row(n, per) for per, n, _ in outs] + [col, col],
        out_shape=[jax.ShapeDtypeStruct((B, S // per, n), dt) for per, n, dt in outs]
        + [jax.ShapeDtypeStruct((B, N_KV_GROUPS * VT_ROWS, S), BF)] * 2,
        scratch_shapes=[pltpu.VMEM((tm, KV_WIDTH), F32)],
        compiler_params=_params("arbitrary", "arbitrary"),
        name="norm1_inproj",
    )(x, mod, g, w, wvt, blk)


def _gelu_tanh(x):
    return 0.5 * x * (1.0 + jnp.tanh(0.7978845608028654 * (x + 0.044715 * x * x * x)))


def _compress_kernel(transposed, x_ref, w1b_ref, pos_ref, w1_ref, b1_ref, w2b_ref, o_ref):
    y = _dot(x_ref[...], w1b_ref[...])
    posc = _dot(pos_ref[...], w1_ref[...])[0:1, :] + b1_ref[...]
    n = y.shape[0]
    acts = []
    for g in range(N_KV_GROUPS):
        first = y[:, g * 2 * CMP_HIDDEN: g * 2 * CMP_HIDDEN + CMP_HIDDEN]
        second = y[:, g * 2 * CMP_HIDDEN + CMP_HIDDEN: (g + 1) * 2 * CMP_HIDDEN]
        pre = first + pltpu.roll(second, n - 1, 0) + posc
        acts.append(_gelu_tanh(pre).astype(BF))
    act = jnp.concatenate(acts, axis=1)
    if transposed:
        vt = _dot_nt(w2b_ref[...], act)
        for g in range(N_KV_GROUPS):
            o_ref[g * VT_ROWS:g * VT_ROWS + HEAD_DIM, :] = (
                vt[g * HEAD_DIM:(g + 1) * HEAD_DIM].astype(BF))
            o_ref[g * VT_ROWS + HEAD_DIM:(g + 1) * VT_ROWS, :] = jnp.ones((DEN_ROWS, n), BF)
    else:
        o_ref[...] = _dot(act, w2b_ref[...]).astype(BF)


def _compress(xk, w1big, pos8, w1, b1, w2big, transposed):
    B, n, width = xk.shape
    full = lambda a: pl.BlockSpec(a.shape, lambda b: (0,) * a.ndim)
    out = (N_KV_GROUPS * VT_ROWS, n) if transposed else (n, 2 * KV_WIDTH)
    return pl.pallas_call(
        functools.partial(_compress_kernel, transposed),
        grid=(B,),
        in_specs=[pl.BlockSpec((None, n, width), lambda b: (b, 0, 0)),
                  full(w1big), full(pos8), full(w1), full(b1), full(w2big)],
        out_specs=pl.BlockSpec((None,) + out, lambda b: (b, 0, 0)),
        out_shape=jax.ShapeDtypeStruct((B,) + out, BF),
        compiler_params=_params("arbitrary"),
        name="compress",
    )(xk, w1big, pos8, w1, b1, w2big)


def _masked_exp(s, mask):
    sm = jnp.where(mask, s, NEG)
    m = jnp.max(sm, axis=-1, keepdims=True)
    p = jnp.where(mask, jnp.exp(sm - m), 0.0)
    return p, jnp.sum(p, axis=-1, keepdims=True)


def _safe_inv(l):
    return jnp.where(l > 0.0, 1.0 / jnp.where(l > 0.0, l, 1.0), 0.0)


def _softmax_tile(s, m_old):
    m_new = jnp.maximum(m_old, jnp.max(s, axis=-1, keepdims=True))
    return m_new, jnp.exp2(s - m_new)


def _attn_kernel(q_ref, kc_ref, vct_ref, ks_ref, vst_ref, kw_ref, vwt_ref, bg_ref, ovt_ref, o_ref):
    ci = pl.program_id(1)
    q0 = ci * Q_CHUNK
    Q, H, G = Q_CHUNK, HEADS_PER_GROUP, N_KV_GROUPS
    R = H * Q
    n_blk = ovt_ref.shape[0]
    units = [(bb, g) for bb in range(q_ref.shape[0]) for g in range(G)]
    U = len(units)
    sig = [jax.nn.sigmoid(bg_ref[bb]) for bb in range(q_ref.shape[0])]
    t_q = q0 + lax.broadcasted_iota(I32, (Q, 1), 0)
    t_r = jnp.concatenate([t_q] * H, axis=0)

    def rows4(a):
        return jnp.concatenate([a] * H, axis=0)

    def q_rows(bb, g):
        return jnp.concatenate(
            [q_ref[bb, :, (g * H + h) * LANES:(g * H + h + 1) * LANES] for h in range(H)], axis=0)

    gcs = [slice(g * LANES, (g + 1) * LANES) for g in range(G)]
    grs = [slice(g * VT_ROWS, (g + 1) * VT_ROWS) for g in range(G)]
    qp = [q_rows(bb, g) for bb, g in units]

    def pv_t(vt1, p):
        return _dot_nt(vt1, p.astype(BF))

    w0 = pl.multiple_of((jnp.maximum(q0 - WINDOW, 0) // LANES) * LANES, LANES)
    wkeys = WINDOW + 2 * Q_CHUNK
    s3 = [_dot_nt(qp[u], kw_ref[bb, pl.ds(w0, wkeys), gcs[g]]) for u, (bb, g) in enumerate(units)]
    s1 = [_dot_nt(qp[u], kc_ref[bb, :, gcs[g]]) for u, (bb, g) in enumerate(units)]

    n_idx = lax.broadcasted_iota(I32, s1[0].shape, 1)
    m1 = (n_idx * CMP_STRIDE + (CMP_BLOCK - 1)) <= t_r
    o1, psums = [], []
    for u, (bb, g) in enumerate(units):
        p1, l1 = _masked_exp(s1[u], m1)
        iv = _safe_inv(l1)
        o1.append(pv_t(vct_ref[bb, grs[g], :], p1))
        p1n = p1 * iv
        psum = p1n[0:Q]
        for h in range(1, H):
            psum = psum + p1n[h * Q:(h + 1) * Q]
        psums.append(psum)
    psum = jnp.concatenate(psums, axis=0)
    hi = psum.astype(BF)
    lo = (psum - hi.astype(F32)).astype(BF)
    ps_t = _dot_nt(ovt_ref[...], hi) + _dot_nt(ovt_ref[...], lo)

    kpos3 = w0 + lax.broadcasted_iota(I32, (Q, wkeys), 1)
    bias3 = rows4(jnp.where((kpos3 <= t_q) & (kpos3 > t_q - WINDOW), 0.0, NEG))
    win = []
    for u, (bb, g) in enumerate(units):
        _, p3 = _softmax_tile(s3[u] + bias3, jnp.full((R, 1), NEG, F32))
        win.append(pv_t(vwt_ref[bb, grs[g], pl.ds(w0, wkeys)], p3))

    sig_t = [jnp.concatenate([s, s], axis=0).T for s in sig]
    lane_lo = lax.broadcasted_iota(I32, (1, LANES), 1) < HEAD_DIM

    def gate_row(bb, g, branch):
        rows = [sig_t[bb][branch * N_HEADS + g * H + h:branch * N_HEADS + g * H + h + 1, :]
                for h in range(H)]
        return jnp.concatenate([jnp.where(lane_lo, rows[2 * k], rows[2 * k + 1])
                                for k in range(H // 2)], axis=1)

    def normalised(ot, may_be_empty):
        den = ot[HEAD_DIM:HEAD_DIM + 1, :]
        return ot[:HEAD_DIM] * (_safe_inv(den) if may_be_empty else 1.0 / den)

    early = [gate_row(bb, g, 0) * normalised(o1[u], True)
             + gate_row(bb, g, 2) * normalised(win[u], False)
             for u, (bb, g) in enumerate(units)]
    gate_sel = [gate_row(bb, g, 1) for bb, g in units]

    j = lax.broadcasted_iota(I32, ps_t.shape, 0)
    forced = (j == 0) | (j == ci) | (j == ci - 1)
    score = jnp.where(forced, FORCE_SCORE, jnp.where(j <= ci, ps_t, NEG))
    rank = jnp.zeros(ps_t.shape, I32)
    for jp in range(n_blk):
        c = score[jp:jp + 1, :]
        beats = (c > score) | ((c == score) & (j > jp))
        rank = rank + beats.astype(I32)
    bias_t = jnp.where(rank < N_SELECT, 0.0, NEG)
    pad_t = jnp.concatenate([jnp.zeros((HEAD_DIM, U * Q), F32), bias_t,
                             jnp.zeros((LANES - HEAD_DIM - n_blk, U * Q), F32)], axis=0)
    sel_bias = pad_t.T.astype(BF)

    qa = [qp[u] + rows4(sel_bias[u * Q:(u + 1) * Q]) for u in range(U)]

    def sweep(n_tiles):
        past = (n_tiles - 1) * SEL_KEY_TILE
        keys = n_tiles * SEL_KEY_TILE

        def run():
            kpos = past + lax.broadcasted_iota(I32, (Q, SEL_KEY_TILE), 1)
            bias = rows4(jnp.where(kpos <= t_q, 0.0, NEG))
            s = [_dot_nt(qa[u], ks_ref[bb, 0:keys, gcs[g]]) for u, (bb, g) in enumerate(units)]
            out = []
            for u, (bb, g) in enumerate(units):
                s_last = s[u][:, past:] + bias
                m = jnp.max(s_last, axis=-1, keepdims=True)
                if past:
                    m = jnp.maximum(m, jnp.max(s[u][:, :past], axis=-1, keepdims=True))
                acc = pv_t(vst_ref[bb, grs[g], past:keys], jnp.exp2(s_last - m))
                if past:
                    acc = acc + pv_t(vst_ref[bb, grs[g], 0:past], jnp.exp2(s[u][:, :past] - m))
                out.append(acc)
            return tuple(out)
        return run

    blocks_per_tile = SEL_KEY_TILE // SEL_BLOCK
    max_tiles = n_blk // blocks_per_tile
    sel = lax.switch(ci // blocks_per_tile, [sweep(n) for n in range(1, max_tiles + 1)])

    for u, (bb, g) in enumerate(units):
        out = (early[u] + gate_sel[u] * normalised(sel[u], False)).T
        for k in range(H // 2):
            slab = jnp.concatenate([out[(2 * k) * Q:(2 * k + 1) * Q],
                                    out[(2 * k + 1) * Q:(2 * k + 2) * Q]], axis=1)
            c0 = (g * (H // 2) + k) * LANES
            o_ref[bb, :, c0:c0 + LANES] = slab.astype(BF)


def _attention(q, kc, vc, ks, vs, kw, vw, bg, ovt):
    B, S, _ = q.shape
    nq = S // Q_CHUNK
    nb = ATTN_BATCH if B % ATTN_BATCH == 0 else 1
    per_b = lambda a: pl.BlockSpec((nb,) + a.shape[1:], lambda b, i: (b, 0, 0))
    full = lambda a: pl.BlockSpec(a.shape, lambda b, i: (0,) * a.ndim)
    return pl.pallas_call(
        _attn_kernel,
        grid=(B // nb, nq),
        in_specs=[pl.BlockSpec((nb, Q_CHUNK, N_HEADS * LANES), lambda b, i: (b, i, 0)),
                  per_b(kc), per_b(vc), per_b(ks), per_b(vs), per_b(kw), per_b(vw),
                  pl.BlockSpec((nb, Q_CHUNK, LANES), lambda b, i: (b, i, 0)),
                  full(ovt)],
        out_specs=pl.BlockSpec((nb, Q_CHUNK, ATTN_WIDTH), lambda b, i: (b, i, 0)),
        out_shape=jax.ShapeDtypeStruct((B, S, ATTN_WIDTH), BF),
        compiler_params=_params("arbitrary", "arbitrary"),
        name="nsa_attention",
    )(q, kc, vc, ks, vs, kw, vw, bg, ovt)


def _mixer_out_kernel(upc_ref, upp_ref, o_ref, gm_ref, x_ref, mod_ref, pw_ref, psc_ref,
                      wup_ref, wua_ref, wo_ref, g2_ref, wr_ref, br_ref, tri_ref,
                      x1_ref, h2t_ref, crk_ref, rw_ref, cnt_ref, cnt_scr):
    i = pl.program_id(1)

    @pl.when((pl.program_id(0) == 0) & (i == 0))
    def _():
        cnt_scr[...] = jnp.zeros_like(cnt_scr)

    tm = upc_ref.shape[0]
    prev = upp_ref[...] * (i > 0).astype(F32)
    ext = jnp.concatenate([prev, upc_ref[...]], axis=0)
    t = i * tm + lax.broadcasted_iota(I32, (tm, 1), 0)
    ys = []
    for gi, w in enumerate(POOL_WINDOWS):
        u = ext[:, gi * POOL_GROUP:(gi + 1) * POOL_GROUP]
        acc = u
        shift = 1
        while shift < w:
            acc = acc + pltpu.roll(acc, shift, 0)
            shift *= 2
        inv_cnt = 1.0 / jnp.minimum(t + 1, w).astype(F32)
        p = acc[POOL_HALO:] * inv_cnt - u[POOL_HALO:]
        ys.append(_dot(p.astype(BF), pw_ref[gi]))
    y = jnp.concatenate(ys, axis=1) * psc_ref[...]
    y_pool = _dot(y.astype(BF), wup_ref[...])
    y_attn = _dot(o_ref[...], wua_ref[...])
    gm = gm_ref[...].astype(F32)
    mix = gm[:, :D_MODEL] * y_pool + gm[:, D_MODEL:] * y_attn
    x1 = x_ref[...] + mod_ref[2:3, :] * _dot(mix.astype(BF), wo_ref[...])
    x1_ref[...] = x1
    r = lax.rsqrt(jnp.mean(x1 * x1, axis=-1, keepdims=True) + EPS)
    h2f = x1 * r * g2_ref[...] * (1.0 + mod_ref[4:5, :]) + mod_ref[3:4, :]
    h2 = h2f.astype(BF)
    for k in range(ROW_TILES):
        h2t_ref[pl.ds(k, tm, stride=ROW_TILES), :] = h2f[:, k * LANES:(k + 1) * LANES]

    lt = _dot_nt(wr_ref[...], h2) + br_ref[...]
    lg = lt[0:N_GROUPS]
    gmax = jnp.max(lg, axis=0, keepdims=True)
    gi_ = lax.broadcasted_iota(I32, lg.shape, 0)
    gidx = jnp.min(jnp.where(lg == gmax, gi_, N_GROUPS), axis=0, keepdims=True)
    gp = 1.0 / jnp.sum(jnp.exp(lg - gmax), axis=0, keepdims=True)
    E = EXPERTS_PER_GROUP
    le = jnp.zeros((E, tm), F32)
    for gg in range(N_GROUPS):
        le = jnp.where(gidx == gg, lt[8 + gg * E:8 + (gg + 1) * E], le)
    ei = lax.broadcasted_iota(I32, le.shape, 0)
    v1 = jnp.max(le, axis=0, keepdims=True)
    i1 = jnp.min(jnp.where(le == v1, ei, E), axis=0, keepdims=True)
    rest = jnp.where(ei == i1, -jnp.inf, le)
    v2 = jnp.max(rest, axis=0, keepdims=True)
    i2 = jnp.min(jnp.where(rest == v2, ei, E), axis=0, keepdims=True)
    e = jnp.exp(v2 - v1)
    wa = gp / (1.0 + e)
    wb = gp * e / (1.0 + e)
    lo = jnp.minimum(i1, i2)
    hi = jnp.maximum(i1, i2)
    pair = lax.shift_right_logical(lo * (2 * E - 1 - lo), 1) + hi - lo - 1
    cls = gidx * PAIRS_PER_GROUP + pair
    first_lo = i1 < i2
    w_lo = jnp.where(first_lo, wa, wb)
    w_hi = jnp.where(first_lo, wb, wa)
    oh = lax.broadcasted_iota(I32, (N_CLASS_PAD, tm), 0) == cls
    before = _dot(oh.astype(BF), tri_ref[...]) + cnt_scr[:, 0:1]
    rank = jnp.sum(jnp.where(oh, before, 0.0), axis=0, keepdims=True).astype(I32)
    cnt_scr[...] = cnt_scr[...] + jnp.sum(oh.astype(F32), axis=1, keepdims=True)
    cnt_ref[...] = cnt_scr[...]
    row = lax.broadcasted_iota(I32, (8, tm), 0)
    crk_ref[...] = jnp.where(row == 0, cls, jnp.where(row == 1, rank, 0))
    rw_ref[...] = jnp.where(row == 0, w_lo, jnp.where(row == 1, w_hi, 0.0)).T


def _mixer_out(up, o, gm, x, mod, pw, psc, wup, wua, wo, g2, wr, br):
    B, S, D = x.shape
    tm = ROUTE_TILE
    nt = S // tm
    tri = jnp.asarray(np.triu(np.ones((tm, tm), np.float32), k=1), BF)
    row = lambda n: pl.BlockSpec((None, tm, n), lambda b, i: (b, i, 0))
    full = lambda a: pl.BlockSpec(a.shape, lambda b, i: (0,) * a.ndim)
    per = tm // POOL_HALO
    return pl.pallas_call(
        _mixer_out_kernel,
        grid=(B, nt),
        in_specs=[row(POOL_WIDTH),
                  pl.BlockSpec((None, POOL_HALO, POOL_WIDTH),
                               lambda b, i: (b, jnp.maximum(i * per - 1, 0), 0)),
                  row(ATTN_WIDTH), row(2 * D_MODEL), row(D),
                  pl.BlockSpec((None, 6, D), lambda b, i: (b, 0, 0)),
                  full(pw), full(psc), full(wup), full(wua), full(wo), full(g2),
                  full(wr), full(br), full(tri)],
        out_specs=[row(D),
                   pl.BlockSpec((tm * ROW_TILES, LANES), lambda b, i: (b * nt + i, 0)),
                   pl.BlockSpec((None, 8, tm), lambda b, i: (b * nt + i, 0, 0)),
                   pl.BlockSpec((tm, 8), lambda b, i: (b * nt + i, 0)),
                   pl.BlockSpec((N_CLASS_PAD, LANES), lambda b, i: (0, 0))],
        out_shape=[jax.ShapeDtypeStruct((B, S, D), F32),
                   jax.ShapeDtypeStruct((B * S * ROW_TILES, LANES), F32),
                   jax.ShapeDtypeStruct((B * nt, 8, tm), I32),
                   jax.ShapeDtypeStruct((B * S, 8), F32),
                   jax.ShapeDtypeStruct((N_CLASS_PAD, LANES), F32)],
        scratch_shapes=[pltpu.VMEM((N_CLASS_PAD, LANES), F32)],
        compiler_params=_params("arbitrary", "arbitrary"),
        name="mixer_out_router",
    )(up, up, o, gm, x, mod, pw, psc, wup, wua, wo, g2, wr, br, tri)


def _plan_kernel(crk_ref, cnt_ref, etab_ref, pos_ref, tmap_ref):
    C = N_CLASS_PAD
    cnt = cnt_ref[:, 0:1].astype(I32)
    ntile = lax.shift_right_logical(cnt + (EXPERT_TILE - 1), EXPERT_TILE_LOG2)
    ntile_f = ntile.astype(F32)
    r = lax.broadcasted_iota(I32, (C, C), 0)
    c = lax.broadcasted_iota(I32, (C, C), 1)
    lower = (c < r).astype(BF)
    first = _dot(lower, jnp.broadcast_to(ntile_f, (C, LANES)).astype(BF))[:, 0:1]
    last = first + ntile_f
    total = jnp.sum(ntile_f, axis=0, keepdims=True)
    off = (first * EXPERT_TILE).astype(I32)

    def body(i, carry):
        cls = crk_ref[i, 0:1, :]
        rank = crk_ref[i, 1:2, :]
        oh = lax.broadcasted_iota(I32, (C, cls.shape[1]), 0) == cls
        pos_ref[pl.ds(i, 1), :] = jnp.sum(jnp.where(oh, off, 0), axis=0, keepdims=True) + rank
        return carry

    lax.fori_loop(0, crk_ref.shape[0], body, 0)

    nj = tmap_ref.shape[1]
    j = lax.broadcasted_iota(I32, (1, nj), 1).astype(F32)
    jj = jnp.minimum(j, total - 1.0)
    tcls = jnp.sum((last <= jj).astype(I32), axis=0, keepdims=True)
    oh2 = lax.broadcasted_iota(I32, (C, nj), 0) == tcls
    elo = jnp.sum(jnp.where(oh2, etab_ref[:, 0:1], 0), axis=0, keepdims=True)
    ehi = jnp.sum(jnp.where(oh2, etab_ref[:, 1:2], 0), axis=0, keepdims=True)
    row = lax.broadcasted_iota(I32, (8, nj), 0)
    tmap_ref[...] = jnp.where(
        row == 0, elo, jnp.where(row == 1, ehi, jnp.where(
            row == 2, (j < total).astype(I32), jnp.where(row == 3, jj.astype(I32), 0))))


def _plan(crk, cnt, n_tiles_pad):
    nt, _, tm = crk.shape
    etab = np.zeros((N_CLASS_PAD, LANES), np.int32)
    cid = 0
    for g in range(N_GROUPS):
        for lo in range(EXPERTS_PER_GROUP):
            for hi in range(lo + 1, EXPERTS_PER_GROUP):
                etab[cid, 0] = g * EXPERTS_PER_GROUP + lo
                etab[cid, 1] = g * EXPERTS_PER_GROUP + hi
                cid += 1
    etab = jnp.asarray(etab)
    full = lambda a: pl.BlockSpec(a.shape, lambda i: (0,) * a.ndim)
    return pl.pallas_call(
        _plan_kernel,
        grid=(1,),
        in_specs=[full(crk), full(cnt), full(etab)],
        out_specs=[pl.BlockSpec((nt, tm), lambda i: (0, 0)),
                   pl.BlockSpec((8, n_tiles_pad), lambda i: (0, 0))],
        out_shape=[jax.ShapeDtypeStruct((nt, tm), I32),
                   jax.ShapeDtypeStruct((8, n_tiles_pad), I32)],
        compiler_params=_params("arbitrary"),
        name="moe_plan",
    )(crk, cnt, etab)


def _dispatch_kernel(pos_ref, h_ref, xs_in_ref, xs_ref, buf, load_sem, scatter_sem):
    del xs_in_ref
    i = pl.program_id(0)
    n = pl.num_programs(0)
    rows = DISPATCH_ROWS
    step_rows = rows * ROW_TILES

    def load(step, slot):
        start = pl.multiple_of(step * step_rows, step_rows)
        return pltpu.make_async_copy(h_ref.at[pl.ds(start, step_rows)], buf.at[slot],
                                     load_sem.at[slot])

    def wait_scatter(slot):
        pltpu.make_async_copy(buf.at[slot], xs_ref.at[pl.ds(0, step_rows)],
                              scatter_sem.at[slot]).wait()

    slot = i % DISPATCH_BUFFERS
    nslot = (i + 1) % DISPATCH_BUFFERS

    @pl.when(i == 0)
    def _():
        load(0, 0).start()

    @pl.when(i >= DISPATCH_BUFFERS - 1)
    def _():
        wait_scatter(nslot)

    @pl.when(i + 1 < n)
    def _():
        load(i + 1, nslot).start()

    load(i, slot).wait()

    def issue(r8, carry):
        for k in range(DMA_UNROLL):
            r = r8 * DMA_UNROLL + k
            p = pos_ref[i * rows + r]
            pltpu.make_async_copy(
                buf.at[slot, pl.ds(pl.multiple_of(r * ROW_TILES, ROW_TILES), ROW_TILES)],
                xs_ref.at[pl.ds(pl.multiple_of(p * ROW_TILES, ROW_TILES), ROW_TILES)],
                scatter_sem.at[slot]).start()
        return carry

    lax.fori_loop(0, rows // DMA_UNROLL, issue, 0)

    @pl.when(i == n - 1)
    def _():
        wait_scatter(slot)

        @pl.when(n >= 2)
        def _():
            wait_scatter((i + DISPATCH_BUFFERS - 1) % DISPATCH_BUFFERS)


def _dispatch(pos, h2t, n_sorted_rows):
    rows = DISPATCH_ROWS
    n = h2t.shape[0] // ROW_TILES
    zeros = jnp.zeros((n_sorted_rows * ROW_TILES, LANES), F32)
    return pl.pallas_call(
        _dispatch_kernel,
        grid_spec=pltpu.PrefetchScalarGridSpec(
            num_scalar_prefetch=1,
            grid=(n // rows,),
            in_specs=[pl.BlockSpec(memory_space=pl.ANY), pl.BlockSpec(memory_space=pl.ANY)],
            out_specs=pl.BlockSpec(memory_space=pl.ANY),
            scratch_shapes=[pltpu.VMEM((DISPATCH_BUFFERS, rows * ROW_TILES, LANES), F32),
                            pltpu.SemaphoreType.DMA((DISPATCH_BUFFERS,)),
                            pltpu.SemaphoreType.DMA((DISPATCH_BUFFERS,))]),
        out_shape=jax.ShapeDtypeStruct(zeros.shape, F32),
        input_output_aliases={2: 0},
        compiler_params=_params("arbitrary"),
        name="moe_dispatch",
    )(pos, h2t, zeros)


def _expert_kernel(elo_ref, ehi_ref, valid_ref, blk_ref, x_ref, wg0, wu0, wd0, wg1, wu1, wd1, o_ref):
    del elo_ref, ehi_ref, blk_ref
    T = EXPERT_TILE
    valid = valid_ref[pl.program_id(0)] > 0

    @pl.when(jnp.logical_not(valid))
    def _():
        o_ref[...] = jnp.zeros_like(o_ref)

    @pl.when(valid)
    def _():
        x = jnp.concatenate([x_ref[pl.ds(k, T, stride=ROW_TILES), :] for k in range(ROW_TILES)],
                            axis=1).astype(BF)
        for half, (wg, wu, wd) in enumerate(((wg0, wu0, wd0), (wg1, wu1, wd1))):
            a = _dot(x, wg[...])
            b = _dot(x, wu[...])
            he = (a * jax.nn.sigmoid(a)) * b
            y = _dot(he.astype(BF), wd[...])
            for k in range(ROW_TILES):
                o_ref[pl.ds(half * ROW_TILES + k, T, stride=2 * ROW_TILES), :] = (
                    y[:, k * LANES:(k + 1) * LANES])


def _experts(tmap, xs, wg, wu, wd, n_tiles):
    T = EXPERT_TILE
    D, F = D_MODEL, EXPERT_FF
    lo = lambda shape: pl.BlockSpec((None,) + shape, lambda j, elo, ehi, v, blk: (elo[j], 0, 0))
    hi = lambda shape: pl.BlockSpec((None,) + shape, lambda j, elo, ehi, v, blk: (ehi[j], 0, 0))
    return pl.pallas_call(
        _expert_kernel,
        grid_spec=pltpu.PrefetchScalarGridSpec(
            num_scalar_prefetch=4,
            grid=(n_tiles,),
            in_specs=[pl.BlockSpec((T * ROW_TILES, LANES), lambda j, elo, ehi, v, blk: (blk[j], 0)),
                      lo((D, F)), lo((D, F)), lo((F, D)), hi((D, F)), hi((D, F)), hi((F, D))],
            out_specs=pl.BlockSpec((T * 2 * ROW_TILES, LANES), lambda j, elo, ehi, v, blk: (j, 0))),
        out_shape=jax.ShapeDtypeStruct((n_tiles * T * 2 * ROW_TILES, LANES), F32),
        compiler_params=_params("arbitrary"),
        name="moe_experts",
    )(tmap[0], tmap[1], tmap[2], tmap[3], xs, wg, wu, wd, wg, wu, wd)


def _combine_kernel(pos_ref, ys_ref, w_ref, x1_ref, mod_ref, fg_ref, o_ref, buf, sem):
    i = pl.program_id(0)
    tm = x1_ref.shape[0]
    R2 = 2 * ROW_TILES

    def gather(step, slot):
        def issue(r8, carry):
            for k in range(DMA_UNROLL):
                r = r8 * DMA_UNROLL + k
                p = pos_ref[step * tm + r]
                pltpu.make_async_copy(ys_ref.at[pl.ds(pl.multiple_of(p * R2, R2), R2)],
                                      buf.at[slot, pl.ds(pl.multiple_of(r * R2, R2), R2)],
                                      sem.at[slot]).start()
            return carry
        lax.fori_loop(0, tm // DMA_UNROLL, issue, 0)

    @pl.when(i == 0)
    def _():
        gather(0, 0)

    slot = i % 2

    @pl.when(i + 1 < pl.num_programs(0))
    def _():
        gather(i + 1, 1 - slot)

    pltpu.make_async_copy(ys_ref.at[pl.ds(0, tm * R2)], buf.at[slot], sem.at[slot]).wait()
    ylo = jnp.concatenate([buf[slot, pl.ds(k, tm, stride=R2), :] for k in range(ROW_TILES)], axis=1)
    yhi = jnp.concatenate([buf[slot, pl.ds(ROW_TILES + k, tm, stride=R2), :] for k in range(ROW_TILES)],
                          axis=1)
    y = w_ref[:, 0:1] * ylo + w_ref[:, 1:2] * yhi
    x2 = x1_ref[...] + mod_ref[5:6, :] * y
    r = lax.rsqrt(jnp.mean(x2 * x2, axis=-1, keepdims=True) + EPS)
    o_ref[...] = x2 * r * fg_ref[...]


def _combine(pos, ys, w, x1, mod, fg):
    N, D = x1.shape
    B = mod.shape[0]
    tm = COMBINE_ROWS
    per_b = (N // B) // tm
    return pl.pallas_call(
        _combine_kernel,
        grid_spec=pltpu.PrefetchScalarGridSpec(
            num_scalar_prefetch=1,
            grid=(N // tm,),
            in_specs=[pl.BlockSpec(memory_space=pl.ANY),
                      pl.BlockSpec((tm, w.shape[1]), lambda i, p: (i, 0)),
                      pl.BlockSpec((tm, D), lambda i, p: (i, 0)),
                      pl.BlockSpec((None, 6, D), lambda i, p: (i // per_b, 0, 0)),
                      pl.BlockSpec((1, D), lambda i, p: (0, 0))],
            out_specs=pl.BlockSpec((tm, D), lambda i, p: (i, 0)),
            scratch_shapes=[pltpu.VMEM((2, tm * 2 * ROW_TILES, LANES), F32),
                            pltpu.SemaphoreType.DMA((2,))]),
        out_shape=jax.ShapeDtypeStruct((N, D), F32),
        compiler_params=_params("arbitrary"),
        name="moe_combine",
    )(pos, ys, w, x1, mod, fg)


def _arrange_w_in(w):
    cuts = np.cumsum([0, POOL_WIDTH, ATTN_WIDTH] + [KV_WIDTH] * 6 + [N_BRANCH * N_HEADS, 2 * D_MODEL])
    pool, q, kc, vc, ks, vs, kw, vw, bg, mg = [
        w[:, int(cuts[i]):int(cuts[i + 1])] for i in range(len(cuts) - 1)]
    bg = jnp.pad(bg, ((0, 0), (0, LANES - bg.shape[1])))
    main = jnp.concatenate([pool, q, kc, vc, ks, kw, bg, mg], axis=1).astype(BF)
    return main, jnp.concatenate([vs, vw], axis=1).T.astype(BF)


def _compress_weights(pos, w1, b1, w2, transposed):
    eye = jnp.eye(N_KV_GROUPS, dtype=F32)
    halves = CMP_BLOCK // CMP_STRIDE
    w1r = w1.reshape(halves, CMP_STRIDE, HEAD_DIM, CMP_HIDDEN)
    w1big = jnp.einsum('hidc,gk->igdkhc', w1r, eye).reshape(
        CMP_STRIDE * N_KV_GROUPS * HEAD_DIM, N_KV_GROUPS * halves * CMP_HIDDEN)
    if transposed:
        w2big = jnp.einsum('cd,gk->kdgc', w2, eye).reshape(
            N_KV_GROUPS * HEAD_DIM, N_KV_GROUPS * CMP_HIDDEN)
    else:
        w2big = jnp.einsum('cd,gk,r->gckrd', w2, eye, jnp.asarray([1.0, 0.0], F32)).reshape(
            N_KV_GROUPS * CMP_HIDDEN, N_KV_GROUPS * 2 * HEAD_DIM)
    pos8 = jnp.broadcast_to(pos.reshape(1, CMP_BLOCK * HEAD_DIM), (8, CMP_BLOCK * HEAD_DIM))
    return (w1big.astype(BF), pos8.astype(BF), w1.astype(BF), b1.reshape(1, CMP_HIDDEN),
            w2big.astype(BF))


def _selection_tables(S):
    n_chunks = S // CMP_STRIDE
    n_cmp = n_chunks - CMP_BLOCK // CMP_STRIDE + 1
    n_blk = S // SEL_BLOCK
    s1 = np.arange(n_cmp)[:, None] * CMP_STRIDE
    s2 = np.arange(n_blk)[None, :] * SEL_BLOCK
    ovl = np.clip(np.minimum(s1 + CMP_BLOCK, s2 + SEL_BLOCK) - np.maximum(s1, s2), 0, None) / CMP_BLOCK
    ovt = np.zeros((n_blk, n_chunks), np.float32)
    ovt[:, :n_cmp] = ovl.T
    return jnp.asarray(ovt, BF)


def kernel(x, c, ada_w, ada_b, norm1_g, w_in, pool_w, pool_scale, cmp_pos, cmp_w1, cmp_b1, cmp_w2,
           w_up_pool, w_up_attn, w_out, norm2_g, router_g_w, router_g_b, router_e_w, router_e_b,
           exp_w_gate, exp_w_up, exp_w_down, final_g):
    B, S, D = x.shape
    N = B * S
    assert ada_w.shape[0] == 1, "the final norm is fused into the last layer's combine step"
    for l in range(ada_w.shape[0]):
        mod = _ada(c, ada_w[l], ada_b[l]).reshape(B, 6, D)
        (up, q, kc, vc, ks, kw, bg, gm, vst, vwt) = _inproj(
            x, mod, norm1_g[l].reshape(1, D), *_arrange_w_in(w_in[l]))
        kcc = _compress(kc, *_compress_weights(cmp_pos[l, 0], cmp_w1[l, 0], cmp_b1[l, 0],
                                               cmp_w2[l, 0], False), False)
        vct = _compress(vc, *_compress_weights(cmp_pos[l, 1], cmp_w1[l, 1], cmp_b1[l, 1],
                                               cmp_w2[l, 1], True), True)
        o = _attention(q, kcc, vct, ks, vst, kw, vwt, bg, _selection_tables(S))
        wr = jnp.zeros((8 + N_EXPERTS, D), F32)
        wr = wr.at[0:N_GROUPS].set(router_g_w[l].T).at[8:].set(router_e_w[l].T).astype(BF)
        br = jnp.zeros((8 + N_EXPERTS, 1), F32)
        br = br.at[0:N_GROUPS, 0].set(router_g_b[l]).at[8:, 0].set(router_e_b[l])
        x1, h2t, crk, rw, cnt = _mixer_out(
            up, o, gm, x, mod, pool_w[l].astype(BF), pool_scale[l].reshape(1, POOL_WIDTH),
            w_up_pool[l].astype(BF), w_up_attn[l].astype(BF), w_out[l].astype(BF),
            norm2_g[l].reshape(1, D), wr, br)
        n_tiles = N // EXPERT_TILE + N_CLASS
        n_tiles_pad = -(-n_tiles // LANES) * LANES
        pos, tmap = _plan(crk, cnt, n_tiles_pad)
        pos = pos.reshape(N)
        xs = _dispatch(pos, h2t, n_tiles * EXPERT_TILE)
        ys = _experts(tmap, xs, exp_w_gate[l].astype(BF), exp_w_up[l].astype(BF),
                      exp_w_down[l].astype(BF), n_tiles)
        y = _combine(pos, ys, rw, x1.reshape(N, D), mod, final_g.reshape(1, D))
        x = y.reshape(B, S, D)
    return x
```

```python
import functools

import numpy as np
import jax
import jax.numpy as jnp
from jax import lax
from jax.experimental import pallas as pl
from jax.experimental.pallas import tpu as pltpu

BF = jnp.bfloat16
F32 = jnp.float32
I32 = jnp.int32

D_MODEL = 1024
POOL_WIDTH = 512
POOL_WINDOWS = (2, 4, 8, 16)
POOL_GROUP = 128
POOL_HALO = 16
N_HEADS = 8
HEAD_DIM = 64
N_KV_GROUPS = 2
HEADS_PER_GROUP = 4
ATTN_WIDTH = 512
KV_WIDTH = 128
CMP_BLOCK = 32
CMP_STRIDE = 16
CMP_HIDDEN = 256
SEL_BLOCK = 64
N_SELECT = 8
WINDOW = 512
Q_CHUNK = 64
N_BRANCH = 3
N_GROUPS = 4
EXPERTS_PER_GROUP = 8
N_EXPERTS = 32
EXPERT_FF = 512
EPS = 1e-6
NEG = -1e30
FORCE_SCORE = 1e4
QK_SCALE = HEAD_DIM ** -0.5
LOG2E = 1.4426950408889634

PAIRS_PER_GROUP = EXPERTS_PER_GROUP * (EXPERTS_PER_GROUP - 1) // 2
N_CLASS = N_GROUPS * PAIRS_PER_GROUP
N_CLASS_PAD = 128
ROUTE_TILE = 512
EXPERT_TILE_LOG2 = 8
EXPERT_TILE = 1 << EXPERT_TILE_LOG2
DISPATCH_ROWS = 2048
DISPATCH_BUFFERS = 3
COMBINE_ROWS = 512
DMA_UNROLL = 8

LANES = 128
ROW_TILES = D_MODEL // LANES
SEL_KEY_TILE = 256
ATTN_BATCH = 2
DEN_ROWS = 16
VT_ROWS = HEAD_DIM + DEN_ROWS
VMEM_LIMIT = 56 * 1024 * 1024

C_POOL = 0
C_Q = C_POOL + POOL_WIDTH
C_KC = C_Q + ATTN_WIDTH
C_VC = C_KC + KV_WIDTH
C_KS = C_VC + KV_WIDTH
C_KW = C_KS + KV_WIDTH
C_BG = C_KW + KV_WIDTH
C_MG = C_BG + LANES
C_END = C_MG + 2 * D_MODEL


def _dot(a, b):
    return jnp.dot(a, b, preferred_element_type=F32)


def _dot_nt(a, b):
    return lax.dot_general(a, b, (((1,), (1,)), ((), ())), preferred_element_type=F32)


def _params(*sem):
    return pltpu.CompilerParams(dimension_semantics=sem, vmem_limit_bytes=VMEM_LIMIT)


def _ada_kernel(c_ref, w_ref, b_ref, o_ref):
    o_ref[...] = _dot(c_ref[...].astype(BF), w_ref[...].astype(BF)) + b_ref[...]


def _ada(c, w, b):
    B, D = c.shape
    n = w.shape[1]
    tn = 1024
    return pl.pallas_call(
        _ada_kernel,
        grid=(n // tn,),
        in_specs=[pl.BlockSpec((B, D), lambda j: (0, 0)),
                  pl.BlockSpec((D, tn), lambda j: (0, j)),
                  pl.BlockSpec((1, tn), lambda j: (0, j))],
        out_specs=pl.BlockSpec((B, tn), lambda j: (0, j)),
        out_shape=jax.ShapeDtypeStruct((B, n), F32),
        compiler_params=_params("arbitrary"),
        name="ada_mod",
    )(c, w, b.reshape(1, n))


def _inproj_kernel(x_ref, mod_ref, g_ref, w_ref, wvt_ref, blk_ref, up_ref, q_ref, kc_ref, vc_ref,
                   ks_ref, kw_ref, bg_ref, gm_ref, vst_ref, vwt_ref, stage_ref):
    x = x_ref[...]
    r = lax.rsqrt(jnp.mean(x * x, axis=-1, keepdims=True) + EPS)
    h = x * r * g_ref[...] * (1.0 + mod_ref[1:2, :]) + mod_ref[0:1, :]
    hb = h.astype(BF)

    def proj(a, b):
        return _dot(hb, w_ref[:, a:b])

    def spread(v, fill):
        pad = jnp.full((v.shape[0], HEAD_DIM), fill, v.dtype)
        pieces = []
        for i in range(v.shape[1] // HEAD_DIM):
            pieces += [v[:, i * HEAD_DIM:(i + 1) * HEAD_DIM], pad]
        return jnp.concatenate(pieces, axis=1)

    up_ref[...] = proj(C_POOL, C_Q)
    q_ref[...] = spread(proj(C_Q, C_KC) * QK_SCALE, 0.0).astype(BF)
    for src, ref in ((C_KC, kc_ref), (C_VC, vc_ref)):
        stage_ref[...] = proj(src, src + KV_WIDTH)
        for t in range(CMP_STRIDE):
            ref[:, t * KV_WIDTH:(t + 1) * KV_WIDTH] = (
                stage_ref[pl.ds(t, ref.shape[0], stride=CMP_STRIDE), :].astype(BF))
    ks_ref[...] = (spread(proj(C_KS, C_KW) * LOG2E, 0.0) + blk_ref[...]).astype(BF)
    kw_ref[...] = spread(proj(C_KW, C_BG) * LOG2E, 0.0).astype(BF)
    vt = _dot_nt(wvt_ref[...], hb)
    ones = jnp.ones((DEN_ROWS, vt.shape[1]), BF)
    for a, ref in enumerate((vst_ref, vwt_ref)):
        for g in range(N_KV_GROUPS):
            r0 = a * KV_WIDTH + g * HEAD_DIM
            ref[g * VT_ROWS:g * VT_ROWS + HEAD_DIM, :] = vt[r0:r0 + HEAD_DIM].astype(BF)
            ref[g * VT_ROWS + HEAD_DIM:(g + 1) * VT_ROWS, :] = ones
    bg_ref[...] = proj(C_BG, C_MG)
    gm_ref[...] = jax.nn.sigmoid(proj(C_MG, C_END)).astype(BF)


def _inproj(x, mod, g, w, wvt):
    B, S, D = x.shape
    tm = 512
    blk = np.zeros((S, 2 * LANES), np.float32)
    for gg in range(N_KV_GROUPS):
        blk[np.arange(S), gg * LANES + HEAD_DIM + np.arange(S) // SEL_BLOCK] = 1.0
    blk = jnp.asarray(blk)
    outs = [(1, POOL_WIDTH, F32), (1, N_HEADS * LANES, BF),
            (CMP_STRIDE, CMP_STRIDE * KV_WIDTH, BF), (CMP_STRIDE, CMP_STRIDE * KV_WIDTH, BF),
            (1, 2 * KV_WIDTH, BF), (1, 2 * KV_WIDTH, BF), (1, LANES, F32), (1, 2 * D_MODEL, BF)]
    row = lambda n, per=1: pl.BlockSpec((None, tm // per, n), lambda b, i: (b, i, 0))
    col = pl.BlockSpec((None, N_KV_GROUPS * VT_ROWS, tm), lambda b, i: (b, 0, i))
    return pl.pallas_call(
        _inproj_kernel,
        grid=(B, S // tm),
        in_specs=[row(D),
                  pl.BlockSpec((None, 6, D), lambda b, i: (b, 0, 0)),
                  pl.BlockSpec((1, D), lambda b, i: (0, 0)),
                  pl.BlockSpec((D, C_END), lambda b, i: (0, 0)),
                  pl.BlockSpec((2 * KV_WIDTH, D), lambda b, i: (0, 0)),
                  pl.BlockSpec((tm, 2 * LANES), lambda b, i: (i, 0))],
        out_specs=[row(n, per) for per, n, _ in outs] + [col, col],
        out_shape=[jax.ShapeDtypeStruct((B, S // per, n), dt) for per, n, dt in outs]
        + [jax.ShapeDtypeStruct((B, N_KV_GROUPS * VT_ROWS, S), BF)] * 2,
        scratch_shapes=[pltpu.VMEM((tm, KV_WIDTH), F32)],
        compiler_params=_params("arbitrary", "arbitrary"),
        name="norm1_inproj",
    )(x, mod, g, w, wvt, blk)


def _gelu_tanh(x):
    return 0.5 * x * (1.0 + jnp.tanh(0.7978845608028654 * (x + 0.044715 * x * x * x)))


def _compress_kernel(transposed, x_ref, w1b_ref, pos_ref, w1_ref, b1_ref, w2b_ref, o_ref):
    y = _dot(x_ref[...], w1b_ref[...])
    posc = _dot(pos_ref[...], w1_ref[...])[0:1, :] + b1_ref[...]
    n = y.shape[0]
    acts = []
    for g in range(N_KV_GROUPS):
        first = y[:, g * 2 * CMP_HIDDEN: g * 2 * CMP_HIDDEN + CMP_HIDDEN]
        second = y[:, g * 2 * CMP_HIDDEN + CMP_HIDDEN: (g + 1) * 2 * CMP_HIDDEN]
        pre = first + pltpu.roll(second, n - 1, 0) + posc
        acts.append(_gelu_tanh(pre).astype(BF))
    act = jnp.concatenate(acts, axis=1)
    if transposed:
        vt = _dot_nt(w2b_ref[...], act)
        for g in range(N_KV_GROUPS):
            o_ref[g * VT_ROWS:g * VT_ROWS + HEAD_DIM, :] = (
                vt[g * HEAD_DIM:(g + 1) * HEAD_DIM].astype(BF))
            o_ref[g * VT_ROWS + HEAD_DIM:(g + 1) * VT_ROWS, :] = jnp.ones((DEN_ROWS, n), BF)
    else:
        o_ref[...] = _dot(act, w2b_ref[...]).astype(BF)


def _compress(xk, w1big, pos8, w1, b1, w2big, transposed):
    B, n, width = xk.shape
    full = lambda a: pl.BlockSpec(a.shape, lambda b: (0,) * a.ndim)
    out = (N_KV_GROUPS * VT_ROWS, n) if transposed else (n, 2 * KV_WIDTH)
    return pl.pallas_call(
        functools.partial(_compress_kernel, transposed),
        grid=(B,),
        in_specs=[pl.BlockSpec((None, n, width), lambda b: (b, 0, 0)),
                  full(w1big), full(pos8), full(w1), full(b1), full(w2big)],
        out_specs=pl.BlockSpec((None,) + out, lambda b: (b, 0, 0)),
        out_shape=jax.ShapeDtypeStruct((B,) + out, BF),
        compiler_params=_params("arbitrary"),
        name="compress",
    )(xk, w1big, pos8, w1, b1, w2big)


def _masked_exp(s, mask):
    sm = jnp.where(mask, s, NEG)
    m = jnp.max(sm, axis=-1, keepdims=True)
    p = jnp.where(mask, jnp.exp(sm - m), 0.0)
    return p, jnp.sum(p, axis=-1, keepdims=True)


def _safe_inv(l):
    return jnp.where(l > 0.0, 1.0 / jnp.where(l > 0.0, l, 1.0), 0.0)


def _softmax_tile(s, m_old):
    m_new = jnp.maximum(m_old, jnp.max(s, axis=-1, keepdims=True))
    return m_new, jnp.exp2(s - m_new)


def _attn_kernel(q_ref, kc_ref, vct_ref, ks_ref, vst_ref, kw_ref, vwt_ref, bg_ref, ovt_ref, o_ref):
    ci = pl.program_id(1)
    q0 = ci * Q_CHUNK
    Q, H, G = Q_CHUNK, HEADS_PER_GROUP, N_KV_GROUPS
    R = H * Q
    n_blk = ovt_ref.shape[0]
    units = [(bb, g) for bb in range(q_ref.shape[0]) for g in range(G)]
    U = len(units)
    sig = [jax.nn.sigmoid(bg_ref[bb]) for bb in range(q_ref.shape[0])]
    t_q = q0 + lax.broadcasted_iota(I32, (Q, 1), 0)
    t_r = jnp.concatenate([t_q] * H, axis=0)

    def rows4(a):
        return jnp.concatenate([a] * H, axis=0)

    def q_rows(bb, g):
        return jnp.concatenate(
            [q_ref[bb, :, (g * H + h) * LANES:(g * H + h + 1) * LANES] for h in range(H)], axis=0)

    gcs = [slice(g * LANES, (g + 1) * LANES) for g in range(G)]
    grs = [slice(g * VT_ROWS, (g + 1) * VT_ROWS) for g in range(G)]
    qp = [q_rows(bb, g) for bb, g in units]

    def pv_t(vt1, p):
        return _dot_nt(vt1, p.astype(BF))

    w0 = pl.multiple_of((jnp.maximum(q0 - WINDOW, 0) // LANES) * LANES, LANES)
    wkeys = WINDOW + 2 * Q_CHUNK
    s3 = [_dot_nt(qp[u], kw_ref[bb, pl.ds(w0, wkeys), gcs[g]]) for u, (bb, g) in enumerate(units)]
    s1 = [_dot_nt(qp[u], kc_ref[bb, :, gcs[g]]) for u, (bb, g) in enumerate(units)]

    n_idx = lax.broadcasted_iota(I32, s1[0].shape, 1)
    m1 = (n_idx * CMP_STRIDE + (CMP_BLOCK - 1)) <= t_r
    o1, psums = [], []
    for u, (bb, g) in enumerate(units):
        p1, l1 = _masked_exp(s1[u], m1)
        iv = _safe_inv(l1)
        o1.append(pv_t(vct_ref[bb, grs[g], :], p1))
        p1n = p1 * iv
        psum = p1n[0:Q]
        for h in range(1, H):
            psum = psum + p1n[h * Q:(h + 1) * Q]
        psums.append(psum)
    psum = jnp.concatenate(psums, axis=0)
    hi = psum.astype(BF)
    lo = (psum - hi.astype(F32)).astype(BF)
    ps_t = _dot_nt(ovt_ref[...], hi) + _dot_nt(ovt_ref[...], lo)

    kpos3 = w0 + lax.broadcasted_iota(I32, (Q, wkeys), 1)
    bias3 = rows4(jnp.where((kpos3 <= t_q) & (kpos3 > t_q - WINDOW), 0.0, NEG))
    win = []
    for u, (bb, g) in enumerate(units):
        _, p3 = _softmax_tile(s3[u] + bias3, jnp.full((R, 1), NEG, F32))
        win.append(pv_t(vwt_ref[bb, grs[g], pl.ds(w0, wkeys)], p3))

    sig_t = [jnp.concatenate([s, s], axis=0).T for s in sig]
    lane_lo = lax.broadcasted_iota(I32, (1, LANES), 1) < HEAD_DIM

    def gate_row(bb, g, branch):
        rows = [sig_t[bb][branch * N_HEADS + g * H + h:branch * N_HEADS + g * H + h + 1, :]
                for h in range(H)]
        return jnp.concatenate([jnp.where(lane_lo, rows[2 * k], rows[2 * k + 1])
                                for k in range(H // 2)], axis=1)

    def normalised(ot, may_be_empty):
        den = ot[HEAD_DIM:HEAD_DIM + 1, :]
        return ot[:HEAD_DIM] * (_safe_inv(den) if may_be_empty else 1.0 / den)

    early = [gate_row(bb, g, 0) * normalised(o1[u], True)
             + gate_row(bb, g, 2) * normalised(win[u], False)
             for u, (bb, g) in enumerate(units)]
    gate_sel = [gate_row(bb, g, 1) for bb, g in units]

    j = lax.broadcasted_iota(I32, ps_t.shape, 0)
    forced = (j == 0) | (j == ci) | (j == ci - 1)
    score = jnp.where(forced, FORCE_SCORE, jnp.where(j <= ci, ps_t, NEG))
    rank = jnp.zeros(ps_t.shape, I32)
    for jp in range(n_blk):
        c = score[jp:jp + 1, :]
        beats = (c > score) | ((c == score) & (j > jp))
        rank = rank + beats.astype(I32)
    bias_t = jnp.where(rank < N_SELECT, 0.0, NEG)
    pad_t = jnp.concatenate([jnp.zeros((HEAD_DIM, U * Q), F32), bias_t,
                             jnp.zeros((LANES - HEAD_DIM - n_blk, U * Q), F32)], axis=0)
    sel_bias = pad_t.T.astype(BF)

    qa = [qp[u] + rows4(sel_bias[u * Q:(u + 1) * Q]) for u in range(U)]

    def sweep(n_tiles):
        past = (n_tiles - 1) * SEL_KEY_TILE
        keys = n_tiles * SEL_KEY_TILE

        def run():
            kpos = past + lax.broadcasted_iota(I32, (Q, SEL_KEY_TILE), 1)
            bias = rows4(jnp.where(kpos <= t_q, 0.0, NEG))
            s = [_dot_nt(qa[u], ks_ref[bb, 0:keys, gcs[g]]) for u, (bb, g) in enumerate(units)]
            out = []
            for u, (bb, g) in enumerate(units):
                s_last = s[u][:, past:] + bias
                m = jnp.max(s_last, axis=-1, keepdims=True)
                if past:
                    m = jnp.maximum(m, jnp.max(s[u][:, :past], axis=-1, keepdims=True))
                acc = pv_t(vst_ref[bb, grs[g], past:keys], jnp.exp2(s_last - m))
                if past:
                    acc = acc + pv_t(vst_ref[bb, grs[g], 0:past], jnp.exp2(s[u][:, :past] - m))
                out.append(acc)
            return tuple(out)
        return run

    blocks_per_tile = SEL_KEY_TILE // SEL_BLOCK
    max_tiles = n_blk // blocks_per_tile
    sel = lax.switch(ci // blocks_per_tile, [sweep(n) for n in range(1, max_tiles + 1)])

    for u, (bb, g) in enumerate(units):
        out = (early[u] + gate_sel[u] * normalised(sel[u], False)).T
        for k in range(H // 2):
            slab = jnp.concatenate([out[(2 * k) * Q:(2 * k + 1) * Q],
                                    out[(2 * k + 1) * Q:(2 * k + 2) * Q]], axis=1)
            c0 = (g * (H // 2) + k) * LANES
            o_ref[bb, :, c0:c0 + LANES] = slab.astype(BF)


def _attention(q, kc, vc, ks, vs, kw, vw, bg, ovt):
    B, S, _ = q.shape
    nq = S // Q_CHUNK
    nb = ATTN_BATCH if B % ATTN_BATCH == 0 else 1
    per_b = lambda a: pl.BlockSpec((nb,) + a.shape[1:], lambda b, i: (b, 0, 0))
    full = lambda a: pl.BlockSpec(a.shape, lambda b, i: (0,) * a.ndim)
    return pl.pallas_call(
        _attn_kernel,
        grid=(B // nb, nq),
        in_specs=[pl.BlockSpec((nb, Q_CHUNK, N_HEADS * LANES), lambda b, i: (b, i, 0)),
                  per_b(kc), per_b(vc), per_b(ks), per_b(vs), per_b(kw), per_b(vw),
                  pl.BlockSpec((nb, Q_CHUNK, LANES), lambda b, i: (b, i, 0)),
                  full(ovt)],
        out_specs=pl.BlockSpec((nb, Q_CHUNK, ATTN_WIDTH), lambda b, i: (b, i, 0)),
        out_shape=jax.ShapeDtypeStruct((B, S, ATTN_WIDTH), BF),
        compiler_params=_params("arbitrary", "arbitrary"),
        name="nsa_attention",
    )(q, kc, vc, ks, vs, kw, vw, bg, ovt)


def _mixer_out_kernel(upc_ref, upp_ref, o_ref, gm_ref, x_ref, mod_ref, pw_ref, psc_ref,
                      wup_ref, wua_ref, wo_ref, g2_ref, wr_ref, br_ref, tri_ref,
                      x1_ref, h2t_ref, crk_ref, rw_ref, cnt_ref, cnt_scr):
    i = pl.program_id(1)

    @pl.when((pl.program_id(0) == 0) & (i == 0))
    def _():
        cnt_scr[...] = jnp.zeros_like(cnt_scr)

    tm = upc_ref.shape[0]
    prev = upp_ref[...] * (i > 0).astype(F32)
    ext = jnp.concatenate([prev, upc_ref[...]], axis=0)
    t = i * tm + lax.broadcasted_iota(I32, (tm, 1), 0)
    ys = []
    for gi, w in enumerate(POOL_WINDOWS):
        u = ext[:, gi * POOL_GROUP:(gi + 1) * POOL_GROUP]
        acc = u
        shift = 1
        while shift < w:
            acc = acc + pltpu.roll(acc, shift, 0)
            shift *= 2
        inv_cnt = 1.0 / jnp.minimum(t + 1, w).astype(F32)
        p = acc[POOL_HALO:] * inv_cnt - u[POOL_HALO:]
        ys.append(_dot(p.astype(BF), pw_ref[gi]))
    y = jnp.concatenate(ys, axis=1) * psc_ref[...]
    y_pool = _dot(y.astype(BF), wup_ref[...])
    y_attn = _dot(o_ref[...], wua_ref[...])
    gm = gm_ref[...].astype(F32)
    mix = gm[:, :D_MODEL] * y_pool + gm[:, D_MODEL:] * y_attn
    x1 = x_ref[...] + mod_ref[2:3, :] * _dot(mix.astype(BF), wo_ref[...])
    x1_ref[...] = x1
    r = lax.rsqrt(jnp.mean(x1 * x1, axis=-1, keepdims=True) + EPS)
    h2f = x1 * r * g2_ref[...] * (1.0 + mod_ref[4:5, :]) + mod_ref[3:4, :]
    h2 = h2f.astype(BF)
    for k in range(ROW_TILES):
        h2t_ref[pl.ds(k, tm, stride=ROW_TILES), :] = h2f[:, k * LANES:(k + 1) * LANES]

    lt = _dot_nt(wr_ref[...], h2) + br_ref[...]
    lg = lt[0:N_GROUPS]
    gmax = jnp.max(lg, axis=0, keepdims=True)
    gi_ = lax.broadcasted_iota(I32, lg.shape, 0)
    gidx = jnp.min(jnp.where(lg == gmax, gi_, N_GROUPS), axis=0, keepdims=True)
    gp = 1.0 / jnp.sum(jnp.exp(lg - gmax), axis=0, keepdims=True)
    E = EXPERTS_PER_GROUP
    le = jnp.zeros((E, tm), F32)
    for gg in range(N_GROUPS):
        le = jnp.where(gidx == gg, lt[8 + gg * E:8 + (gg + 1) * E], le)
    ei = lax.broadcasted_iota(I32, le.shape, 0)
    v1 = jnp.max(le, axis=0, keepdims=True)
    i1 = jnp.min(jnp.where(le == v1, ei, E), axis=0, keepdims=True)
    rest = jnp.where(ei == i1, -jnp.inf, le)
    v2 = jnp.max(rest, axis=0, keepdims=True)
    i2 = jnp.min(jnp.where(rest == v2, ei, E), axis=0, keepdims=True)
    e = jnp.exp(v2 - v1)
    wa = gp / (1.0 + e)
    wb = gp * e / (1.0 + e)
    lo = jnp.minimum(i1, i2)
    hi = jnp.maximum(i1, i2)
    pair = lax.shift_right_logical(lo * (2 * E - 1 - lo), 1) + hi - lo - 1
    cls = gidx * PAIRS_PER_GROUP + pair
    first_lo = i1 < i2
    w_lo = jnp.where(first_lo, wa, wb)
    w_hi = jnp.where(first_lo, wb, wa)
    oh = lax.broadcasted_iota(I32, (N_CLASS_PAD, tm), 0) == cls
    before = _dot(oh.astype(BF), tri_ref[...]) + cnt_scr[:, 0:1]
    rank = jnp.sum(jnp.where(oh, before, 0.0), axis=0, keepdims=True).astype(I32)
    cnt_scr[...] = cnt_scr[...] + jnp.sum(oh.astype(F32), axis=1, keepdims=True)
    cnt_ref[...] = cnt_scr[...]
    row = lax.broadcasted_iota(I32, (8, tm), 0)
    crk_ref[...] = jnp.where(row == 0, cls, jnp.where(row == 1, rank, 0))
    rw_ref[...] = jnp.where(row == 0, w_lo, jnp.where(row == 1, w_hi, 0.0)).T


def _mixer_out(up, o, gm, x, mod, pw, psc, wup, wua, wo, g2, wr, br):
    B, S, D = x.shape
    tm = ROUTE_TILE
    nt = S // tm
    tri = jnp.asarray(np.triu(np.ones((tm, tm), np.float32), k=1), BF)
    row = lambda n: pl.BlockSpec((None, tm, n), lambda b, i: (b, i, 0))
    full = lambda a: pl.BlockSpec(a.shape, lambda b, i: (0,) * a.ndim)
    per = tm // POOL_HALO
    return pl.pallas_call(
        _mixer_out_kernel,
        grid=(B, nt),
        in_specs=[row(POOL_WIDTH),
                  pl.BlockSpec((None, POOL_HALO, POOL_WIDTH),
                               lambda b, i: (b, jnp.maximum(i * per - 1, 0), 0)),
                  row(ATTN_WIDTH), row(2 * D_MODEL), row(D),
                  pl.BlockSpec((None, 6, D), lambda b, i: (b, 0, 0)),
                  full(pw), full(psc), full(wup), full(wua), full(wo), full(g2),
                  full(wr), full(br), full(tri)],
        out_specs=[row(D),
                   pl.BlockSpec((tm * ROW_TILES, LANES), lambda b, i: (b * nt + i, 0)),
                   pl.BlockSpec((None, 8, tm), lambda b, i: (b * nt + i, 0, 0)),
                   pl.BlockSpec((tm, 8), lambda b, i: (b * nt + i, 0)),
                   pl.BlockSpec((N_CLASS_PAD, LANES), lambda b, i: (0, 0))],
        out_shape=[jax.ShapeDtypeStruct((B, S, D), F32),
                   jax.ShapeDtypeStruct((B * S * ROW_TILES, LANES), F32),
                   jax.ShapeDtypeStruct((B * nt, 8, tm), I32),
                   jax.ShapeDtypeStruct((B * S, 8), F32),
                   jax.ShapeDtypeStruct((N_CLASS_PAD, LANES), F32)],
        scratch_shapes=[pltpu.VMEM((N_CLASS_PAD, LANES), F32)],
        compiler_params=_params("arbitrary", "arbitrary"),
        name="mixer_out_router",
    )(up, up, o, gm, x, mod, pw, psc, wup, wua, wo, g2, wr, br, tri)


def _plan_kernel(crk_ref, cnt_ref, etab_ref, pos_ref, tmap_ref):
    C = N_CLASS_PAD
    cnt = cnt_ref[:, 0:1].astype(I32)
    ntile = lax.shift_right_logical(cnt + (EXPERT_TILE - 1), EXPERT_TILE_LOG2)
    ntile_f = ntile.astype(F32)
    r = lax.broadcasted_iota(I32, (C, C), 0)
    c = lax.broadcasted_iota(I32, (C, C), 1)
    lower = (c < r).astype(BF)
    first = _dot(lower, jnp.broadcast_to(ntile_f, (C, LANES)).astype(BF))[:, 0:1]
    last = first + ntile_f
    total = jnp.sum(ntile_f, axis=0, keepdims=True)
    off = (first * EXPERT_TILE).astype(I32)

    def body(i, carry):
        cls = crk_ref[i, 0:1, :]
        rank = crk_ref[i, 1:2, :]
        oh = lax.broadcasted_iota(I32, (C, cls.shape[1]), 0) == cls
        pos_ref[pl.ds(i, 1), :] = jnp.sum(jnp.where(oh, off, 0), axis=0, keepdims=True) + rank
        return carry

    lax.fori_loop(0, crk_ref.shape[0], body, 0)

    nj = tmap_ref.shape[1]
    j = lax.broadcasted_iota(I32, (1, nj), 1).astype(F32)
    jj = jnp.minimum(j, total - 1.0)
    tcls = jnp.sum((last <= jj).astype(I32), axis=0, keepdims=True)
    oh2 = lax.broadcasted_iota(I32, (C, nj), 0) == tcls
    elo = jnp.sum(jnp.where(oh2, etab_ref[:, 0:1], 0), axis=0, keepdims=True)
    ehi = jnp.sum(jnp.where(oh2, etab_ref[:, 1:2], 0), axis=0, keepdims=True)
    row = lax.broadcasted_iota(I32, (8, nj), 0)
    tmap_ref[...] = jnp.where(
        row == 0, elo, jnp.where(row == 1, ehi, jnp.where(
            row == 2, (j < total).astype(I32), jnp.where(row == 3, jj.astype(I32), 0))))


def _plan(crk, cnt, n_tiles_pad):
    nt, _, tm = crk.shape
    etab = np.zeros((N_CLASS_PAD, LANES), np.int32)
    cid = 0
    for g in range(N_GROUPS):
        for lo in range(EXPERTS_PER_GROUP):
            for hi in range(lo + 1, EXPERTS_PER_GROUP):
                etab[cid, 0] = g * EXPERTS_PER_GROUP + lo
                etab[cid, 1] = g * EXPERTS_PER_GROUP + hi
                cid += 1
    etab = jnp.asarray(etab)
    full = lambda a: pl.BlockSpec(a.shape, lambda i: (0,) * a.ndim)
    return pl.pallas_call(
        _plan_kernel,
        grid=(1,),
        in_specs=[full(crk), full(cnt), full(etab)],
        out_specs=[pl.BlockSpec((nt, tm), lambda i: (0, 0)),
                   pl.BlockSpec((8, n_tiles_pad), lambda i: (0, 0))],
        out_shape=[jax.ShapeDtypeStruct((nt, tm), I32),
                   jax.ShapeDtypeStruct((8, n_tiles_pad), I32)],
        compiler_params=_params("arbitrary"),
        name="moe_plan",
    )(crk, cnt, etab)


def _dispatch_kernel(pos_ref, h_ref, xs_in_ref, xs_ref, buf, load_sem, scatter_sem):
    del xs_in_ref
    i = pl.program_id(0)
    n = pl.num_programs(0)
    rows = DISPATCH_ROWS
    step_rows = rows * ROW_TILES

    def load(step, slot):
        start = pl.multiple_of(step * step_rows, step_rows)
        return pltpu.make_async_copy(h_ref.at[pl.ds(start, step_rows)], buf.at[slot],
                                     load_sem.at[slot])

    def wait_scatter(slot):
        pltpu.make_async_copy(buf.at[slot], xs_ref.at[pl.ds(0, step_rows)],
                              scatter_sem.at[slot]).wait()

    slot = i % DISPATCH_BUFFERS
    nslot = (i + 1) % DISPATCH_BUFFERS

    @pl.when(i == 0)
    def _():
        load(0, 0).start()

    @pl.when(i >= DISPATCH_BUFFERS - 1)
    def _():
        wait_scatter(nslot)

    @pl.when(i + 1 < n)
    def _():
        load(i + 1, nslot).start()

    load(i, slot).wait()

    def issue(r8, carry):
        for k in range(DMA_UNROLL):
            r = r8 * DMA_UNROLL + k
            p = pos_ref[i * rows + r]
            pltpu.make_async_copy(
                buf.at[slot, pl.ds(pl.multiple_of(r * ROW_TILES, ROW_TILES), ROW_TILES)],
                xs_ref.at[pl.ds(pl.multiple_of(p * ROW_TILES, ROW_TILES), ROW_TILES)],
                scatter_sem.at[slot]).start()
        return carry

    lax.fori_loop(0, rows // DMA_UNROLL, issue, 0)

    @pl.when(i == n - 1)
    def _():
        wait_scatter(slot)

        @pl.when(n >= 2)
        def _():
            wait_scatter((i + DISPATCH_BUFFERS - 1) % DISPATCH_BUFFERS)


def _dispatch(pos, h2t, n_sorted_rows):
    rows = DISPATCH_ROWS
    n = h2t.shape[0] // ROW_TILES
    zeros = jnp.zeros((n_sorted_rows * ROW_TILES, LANES), F32)
    return pl.pallas_call(
        _dispatch_kernel,
        grid_spec=pltpu.PrefetchScalarGridSpec(
            num_scalar_prefetch=1,
            grid=(n // rows,),
            in_specs=[pl.BlockSpec(memory_space=pl.ANY), pl.BlockSpec(memory_space=pl.ANY)],
            out_specs=pl.BlockSpec(memory_space=pl.ANY),
            scratch_shapes=[pltpu.VMEM((DISPATCH_BUFFERS, rows * ROW_TILES, LANES), F32),
                            pltpu.SemaphoreType.DMA((DISPATCH_BUFFERS,)),
                            pltpu.SemaphoreType.DMA((DISPATCH_BUFFERS,))]),
        out_shape=jax.ShapeDtypeStruct(zeros.shape, F32),
        input_output_aliases={2: 0},
        compiler_params=_params("arbitrary"),
        name="moe_dispatch",
    )(pos, h2t, zeros)


def _expert_kernel(elo_ref, ehi_ref, valid_ref, blk_ref, x_ref, wg0, wu0, wd0, wg1, wu1, wd1, o_ref):
    del elo_ref, ehi_ref, blk_ref
    T = EXPERT_TILE
    valid = valid_ref[pl.program_id(0)] > 0

    @pl.when(jnp.logical_not(valid))
    def _():
        o_ref[...] = jnp.zeros_like(o_ref)

    @pl.when(valid)
    def _():
        x = jnp.concatenate([x_ref[pl.ds(k, T, stride=ROW_TILES), :] for k in range(ROW_TILES)],
                            axis=1).astype(BF)
        for half, (wg, wu, wd) in enumerate(((wg0, wu0, wd0), (wg1, wu1, wd1))):
            a = _dot(x, wg[...])
            b = _dot(x, wu[...])
            he = (a * jax.nn.sigmoid(a)) * b
            y = _dot(he.astype(BF), wd[...])
            for k in range(ROW_TILES):
                o_ref[pl.ds(half * ROW_TILES + k, T, stride=2 * ROW_TILES), :] = (
                    y[:, k * LANES:(k + 1) * LANES])


def _experts(tmap, xs, wg, wu, wd, n_tiles):
    T = EXPERT_TILE
    D, F = D_MODEL, EXPERT_FF
    lo = lambda shape: pl.BlockSpec((None,) + shape, lambda j, elo, ehi, v, blk: (elo[j], 0, 0))
    hi = lambda shape: pl.BlockSpec((None,) + shape, lambda j, elo, ehi, v, blk: (ehi[j], 0, 0))
    return pl.pallas_call(
        _expert_kernel,
        grid_spec=pltpu.PrefetchScalarGridSpec(
            num_scalar_prefetch=4,
            grid=(n_tiles,),
            in_specs=[pl.BlockSpec((T * ROW_TILES, LANES), lambda j, elo, ehi, v, blk: (blk[j], 0)),
                      lo((D, F)), lo((D, F)), lo((F, D)), hi((D, F)), hi((D, F)), hi((F, D))],
            out_specs=pl.BlockSpec((T * 2 * ROW_TILES, LANES), lambda j, elo, ehi, v, blk: (j, 0))),
        out_shape=jax.ShapeDtypeStruct((n_tiles * T * 2 * ROW_TILES, LANES), F32),
        compiler_params=_params("arbitrary"),
        name="moe_experts",
    )(tmap[0], tmap[1], tmap[2], tmap[3], xs, wg, wu, wd, wg, wu, wd)


def _combine_kernel(pos_ref, ys_ref, w_ref, x1_ref, mod_ref, fg_ref, o_ref, buf, sem):
    i = pl.program_id(0)
    tm = x1_ref.shape[0]
    R2 = 2 * ROW_TILES

    def gather(step, slot):
        def issue(r8, carry):
            for k in range(DMA_UNROLL):
                r = r8 * DMA_UNROLL + k
                p = pos_ref[step * tm + r]
                pltpu.make_async_copy(ys_ref.at[pl.ds(pl.multiple_of(p * R2, R2), R2)],
                                      buf.at[slot, pl.ds(pl.multiple_of(r * R2, R2), R2)],
                                      sem.at[slot]).start()
            return carry
        lax.fori_loop(0, tm // DMA_UNROLL, issue, 0)

    @pl.when(i == 0)
    def _():
        gather(0, 0)

    slot = i % 2

    @pl.when(i + 1 < pl.num_programs(0))
    def _():
        gather(i + 1, 1 - slot)

    pltpu.make_async_copy(ys_ref.at[pl.ds(0, tm * R2)], buf.at[slot], sem.at[slot]).wait()
    ylo = jnp.concatenate([buf[slot, pl.ds(k, tm, stride=R2), :] for k in range(ROW_TILES)], axis=1)
    yhi = jnp.concatenate([buf[slot, pl.ds(ROW_TILES + k, tm, stride=R2), :] for k in range(ROW_TILES)],
                          axis=1)
    y = w_ref[:, 0:1] * ylo + w_ref[:, 1:2] * yhi
    x2 = x1_ref[...] + mod_ref[5:6, :] * y
    r = lax.rsqrt(jnp.mean(x2 * x2, axis=-1, keepdims=True) + EPS)
    o_ref[...] = x2 * r * fg_ref[...]


def _combine(pos, ys, w, x1, mod, fg):
    N, D = x1.shape
    B = mod.shape[0]
    tm = COMBINE_ROWS
    per_b = (N // B) // tm
    return pl.pallas_call(
        _combine_kernel,
        grid_spec=pltpu.PrefetchScalarGridSpec(
            num_scalar_prefetch=1,
            grid=(N // tm,),
            in_specs=[pl.BlockSpec(memory_space=pl.ANY),
                      pl.BlockSpec((tm, w.shape[1]), lambda i, p: (i, 0)),
                      pl.BlockSpec((tm, D), lambda i, p: (i, 0)),
                      pl.BlockSpec((None, 6, D), lambda i, p: (i // per_b, 0, 0)),
                      pl.BlockSpec((1, D), lambda i, p: (0, 0))],
            out_specs=pl.BlockSpec((tm, D), lambda i, p: (i, 0)),
            scratch_shapes=[pltpu.VMEM((2, tm * 2 * ROW_TILES, LANES), F32),
                            pltpu.SemaphoreType.DMA((2,))]),
        out_shape=jax.ShapeDtypeStruct((N, D), F32),
        compiler_params=_params("arbitrary"),
        name="moe_combine",
    )(pos, ys, w, x1, mod, fg)


def _arrange_w_in(w):
    cuts = np.cumsum([0, POOL_WIDTH, ATTN_WIDTH] + [KV_WIDTH] * 6 + [N_BRANCH * N_HEADS, 2 * D_MODEL])
    pool, q, kc, vc, ks, vs, kw, vw, bg, mg = [
        w[:, int(cuts[i]):int(cuts[i + 1])] for i in range(len(cuts) - 1)]
    bg = jnp.pad(bg, ((0, 0), (0, LANES - bg.shape[1])))
    main = jnp.concatenate([pool, q, kc, vc, ks, kw, bg, mg], axis=1).astype(BF)
    return main, jnp.concatenate([vs, vw], axis=1).T.astype(BF)


def _compress_weights(pos, w1, b1, w2, transposed):
    eye = jnp.eye(N_KV_GROUPS, dtype=F32)
    halves = CMP_BLOCK // CMP_STRIDE
    w1r = w1.reshape(halves, CMP_STRIDE, HEAD_DIM, CMP_HIDDEN)
    w1big = jnp.einsum('hidc,gk->igdkhc', w1r, eye).reshape(
        CMP_STRIDE * N_KV_GROUPS * HEAD_DIM, N_KV_GROUPS * halves * CMP_HIDDEN)
    if transposed:
        w2big = jnp.einsum('cd,gk->kdgc', w2, eye).reshape(
            N_KV_GROUPS * HEAD_DIM, N_KV_GROUPS * CMP_HIDDEN)
    else:
        w2big = jnp.einsum('cd,gk,r->gckrd', w2, eye, jnp.asarray([1.0, 0.0], F32)).reshape(
            N_KV_GROUPS * CMP_HIDDEN, N_KV_GROUPS * 2 * HEAD_DIM)
    pos8 = jnp.broadcast_to(pos.reshape(1, CMP_BLOCK * HEAD_DIM), (8, CMP_BLOCK * HEAD_DIM))
    return (w1big.astype(BF), pos8.astype(BF), w1.astype(BF), b1.reshape(1, CMP_HIDDEN),
            w2big.astype(BF))


def _selection_tables(S):
    n_chunks = S // CMP_STRIDE
    n_cmp = n_chunks - CMP_BLOCK // CMP_STRIDE + 1
    n_blk = S // SEL_BLOCK
    s1 = np.arange(n_cmp)[:, None] * CMP_STRIDE
    s2 = np.arange(n_blk)[None, :] * SEL_BLOCK
    ovl = np.clip(np.minimum(s1 + CMP_BLOCK, s2 + SEL_BLOCK) - np.maximum(s1, s2), 0, None) / CMP_BLOCK
    ovt = np.zeros((n_blk, n_chunks), np.float32)
    ovt[:, :n_cmp] = ovl.T
    return jnp.asarray(ovt, BF)


def kernel(x, c, ada_w, ada_b, norm1_g, w_in, pool_w, pool_scale, cmp_pos, cmp_w1, cmp_b1, cmp_w2,
           w_up_pool, w_up_attn, w_out, norm2_g, router_g_w, router_g_b, router_e_w, router_e_b,
           exp_w_gate, exp_w_up, exp_w_down, final_g):
    B, S, D = x.shape
    N = B * S
    assert ada_w.shape[0] == 1, "the final norm is fused into the last layer's combine step"
    for l in range(ada_w.shape[0]):
        mod = _ada(c, ada_w[l], ada_b[l]).reshape(B, 6, D)
        (up, q, kc, vc, ks, kw, bg, gm, vst, vwt) = _inproj(
            x, mod, norm1_g[l].reshape(1, D), *_arrange_w_in(w_in[l]))
        kcc = _compress(kc, *_compress_weights(cmp_pos[l, 0], cmp_w1[l, 0], cmp_b1[l, 0],
                                               cmp_w2[l, 0], False), False)
        vct = _compress(vc, *_compress_weights(cmp_pos[l, 1], cmp_w1[l, 1], cmp_b1[l, 1],
                                               cmp_w2[l, 1], True), True)
        o = _attention(q, kcc, vct, ks, vst, kw, vwt, bg, _selection_tables(S))
        wr = jnp.zeros((8 + N_EXPERTS, D), F32)
        wr = wr.at[0:N_GROUPS].set(router_g_w[l].T).at[8:].set(router_e_w[l].T).astype(BF)
        br = jnp.zeros((8 + N_EXPERTS, 1), F32)
        br = br.at[0:N_GROUPS, 0].set(router_g_b[l]).at[8:, 0].set(router_e_b[l])
        x1, h2t, crk, rw, cnt = _mixer_out(
            up, o, gm, x, mod, pool_w[l].astype(BF), pool_scale[l].reshape(1, POOL_WIDTH),
            w_up_pool[l].astype(BF), w_up_attn[l].astype(BF), w_out[l].astype(BF),
            norm2_g[l].reshape(1, D), wr, br)
        n_tiles = N // EXPERT_TILE + N_CLASS
        n_tiles_pad = -(-n_tiles // LANES) * LANES
        pos, tmap = _plan(crk, cnt, n_tiles_pad)
        pos = pos.reshape(N)
        xs = _dispatch(pos, h2t, n_tiles * EXPERT_TILE)
        ys = _experts(tmap, xs, exp_w_gate[l].astype(BF), exp_w_up[l].astype(BF),
                      exp_w_down[l].astype(BF), n_tiles)
        y = _combine(pos, ys, rw, x1.reshape(N, D), mod, final_g.reshape(1, D))
        x = y.reshape(B, S, D)
    return x
```
